```python
import jax, jax.numpy as jnp
from jax import lax
import numpy as np

D_MODEL = 2048
BATCH = 2
SEQ = 4096
DEPTH = 4
DEC_BATCH = 8
DEC_SEQ = 4
PAST_LEN = 16384
PAGE_SIZE = 128

N_MIXERS = 3
N_A = (DEPTH + 2) // 3
N_B = (DEPTH + 1) // 3
N_C = DEPTH // 3
PLE_DIM = 256
ROPE_THETA = 10000.0
EPS = 1e-6
NEG = -1e30

RET_HEADS = 8
RET_DK = D_MODEL // RET_HEADS
RET_DV = 2 * RET_DK
RET_QK = RET_HEADS * RET_DK
RET_V = RET_HEADS * RET_DV
RET_CHUNK = 128
RET_IN = 2 * RET_QK + 2 * RET_V

DIL_WINDOWS = (128, 512, 2048)
DIL_RATES = (1, 4, 16)
DIL_GROUPS = 3
DIL_SPAN = 128
DIL_HEADS = 16
DIL_HD = 128
DIL_W = DIL_HEADS * DIL_HD
DIL_IN = DIL_GROUPS * 3 * DIL_W + DIL_W

GM_WIDTH = 2 * D_MODEL
GM_GROUPS = 16
GM_GD = GM_WIDTH // GM_GROUPS
GM_CHUNK = 128
GM_IN = 3 * GM_WIDTH

kernel_name = 'hybrid_retention_dilated_gmlp_decoder_step'


def _rmsnorm(x, g):
    xf = x.astype(jnp.float32)
    y = xf * lax.rsqrt(jnp.mean(xf * xf, axis=-1, keepdims=True) + EPS)
    return (y * g.astype(jnp.float32)).astype(x.dtype)


def _layernorm(x, g, b):
    xf = x.astype(jnp.float32)
    mu = jnp.mean(xf, axis=-1, keepdims=True)
    xc = xf - mu
    y = xc * lax.rsqrt(jnp.mean(xc * xc, axis=-1, keepdims=True) + EPS)
    return (y * g.astype(jnp.float32) + b.astype(jnp.float32)).astype(x.dtype)


def _rope(x, pos):
    half = x.shape[-1] // 2
    inv = ROPE_THETA ** (-jnp.arange(half, dtype=jnp.float32) / half)
    ang = pos.astype(jnp.float32)[:, None] * inv[None, :]
    cos = jnp.cos(ang)[:, None, :]
    sin = jnp.sin(ang)[:, None, :]
    xf = x.astype(jnp.float32)
    x1, x2 = xf[..., :half], xf[..., half:]
    return jnp.concatenate([x1 * cos - x2 * sin, x2 * cos + x1 * sin], axis=-1).astype(x.dtype)


def _retention_scan(q, k, v, s0):
    b, t, h, _ = q.shape
    c = RET_CHUNK if t % RET_CHUNK == 0 else t
    nc = t // c
    lg = jnp.log1p(-jnp.exp2(-5.0 - jnp.arange(h, dtype=jnp.float32)))
    i = jnp.arange(c, dtype=jnp.float32)
    diff = i[:, None] - i[None, :]
    dmat = jnp.where(diff[None] >= 0, jnp.exp(lg[:, None, None] * jnp.maximum(diff, 0.0)[None]), 0.0)
    xi = jnp.exp(lg[None, :] * (i[:, None] + 1.0))
    zeta = jnp.exp(lg[None, :] * (c - 1.0 - i[:, None]))
    gc = jnp.exp(lg * c)

    def to_chunks(a):
        return a.reshape(b, nc, c, h, a.shape[-1]).transpose(1, 0, 2, 3, 4)

    def step(s, inp):
        qc, kc, vc = inp
        sc = jnp.einsum('bihd,bjhd->bhij', qc, kc) * dmat
        o = jnp.einsum('bhij,bjhe->bihe', sc, vc) + jnp.einsum('bihd,bhde->bihe', qc, s) * xi[None, :, :, None]
        s = s * gc[None, :, None, None] + jnp.einsum('bjhd,bjhe->bhde', kc * zeta[None, :, :, None], vc)
        return s, o

    s, o = lax.scan(step, s0, (to_chunks(q), to_chunks(k), to_chunks(v)))
    o = o.transpose(1, 0, 2, 3, 4).reshape(b, t, h, v.shape[-1])
    return o, s


def _retention_mixer(xn, pos, s0, w_in, w_out):
    b, t, _ = xn.shape
    proj = xn @ w_in
    q = proj[..., :RET_QK].reshape(b, t, RET_HEADS, RET_DK)
    k = proj[..., RET_QK:2 * RET_QK].reshape(b, t, RET_HEADS, RET_DK)
    v = proj[..., 2 * RET_QK:2 * RET_QK + RET_V].reshape(b, t, RET_HEADS, RET_DV)
    g = proj[..., 2 * RET_QK + RET_V:]
    q = _rope(q, pos).astype(jnp.float32)
    k = _rope(k, pos).astype(jnp.float32) * (RET_DK ** -0.5)
    if s0 is None:
        s0 = jnp.zeros((b, RET_HEADS, RET_DK, RET_DV), jnp.float32)
    o, s_new = _retention_scan(q, k, v.astype(jnp.float32), s0.astype(jnp.float32))
    o = o * lax.rsqrt(jnp.mean(o * o, axis=-1, keepdims=True) + EPS)
    o = o.reshape(b, t, RET_V).astype(xn.dtype)
    return (o * jax.nn.silu(g)) @ w_out, s_new


def _softmax_stats(s, mask):
    s = jnp.where(mask, s, NEG)
    m = jnp.max(s, axis=-1)
    p = jnp.exp(s - m[..., None])
    return p, m, jnp.sum(p, axis=-1)


def _dilated_prompt(q, k, v, d):
    b, s, h, e = q.shape
    blk = d * DIL_SPAN
    s_pad = -(-s // blk) * blk
    nb = s_pad // blk
    pad = [(0, 0), (0, s_pad - s), (0, 0), (0, 0)]

    def split(a):
        return jnp.pad(a, pad).reshape(b, nb, DIL_SPAN, d, h, e)

    def with_prev(a):
        prev = jnp.pad(a, [(0, 0), (1, 0), (0, 0), (0, 0), (0, 0), (0, 0)])[:, :-1]
        return jnp.concatenate([prev, a], axis=2)

    qb = split(q)
    kk = with_prev(split(k))
    vv = with_prev(split(v))
    sc = jnp.einsum('bnirhe,bnjrhe->bnrhij', qb, kk)
    ii = jnp.arange(DIL_SPAN)[:, None]
    jj = jnp.arange(2 * DIL_SPAN)[None, :]
    band = (jj >= ii) & (jj <= ii + DIL_SPAN)
    mask = band[None] & ((jnp.arange(nb)[:, None, None] > 0) | (jj >= DIL_SPAN)[None])
    p, m, den = _softmax_stats(sc, mask[None, :, None, None])
    o = jnp.einsum('bnrhij,bnjrhe->bnirhe', p, vv) / den.transpose(0, 1, 4, 2, 3)[..., None]
    o = o.reshape(b, s_pad, h, e)[:, :s]
    m = m.transpose(0, 1, 4, 2, 3).reshape(b, s_pad, h)[:, :s]
    den = den.transpose(0, 1, 4, 2, 3).reshape(b, s_pad, h)[:, :s]
    return o, m, den


def _dilated_sample(q, k_all, v_all, d, wb):
    t = q.shape[1]
    idx = wb + jnp.arange(t)[:, None] - d * jnp.arange(DIL_SPAN + 1)[None, :]
    valid = idx >= 0
    idx = jnp.maximum(idx, 0)
    kg = k_all[:, idx]
    vg = v_all[:, idx]
    sc = jnp.einsum('bthe,btkhe->bthk', q, kg)
    p, m, den = _softmax_stats(sc, valid[None, :, None, :])
    o = jnp.einsum('bthk,btkhe->bthe', p, vg) / den[..., None]
    return o, m, den


def _dilated_mixer(xn, pos, bufs, w_in, w_out):
    b, t, _ = xn.shape
    proj = xn @ w_in
    qkv = proj[..., :DIL_GROUPS * 3 * DIL_W].reshape(b, t, DIL_GROUPS, 3, DIL_HEADS, DIL_HD)
    gate = proj[..., DIL_GROUPS * 3 * DIL_W:]
    outs, ms, dens, new_bufs = [], [], [], []
    for g in range(DIL_GROUPS):
        d = DIL_RATES[g]
        q = _rope(qkv[:, :, g, 0], pos).astype(jnp.float32) * (DIL_HD ** -0.5)
        k = _rope(qkv[:, :, g, 1], pos)
        v = qkv[:, :, g, 2]
        kv_new = jnp.stack([k, v], axis=2)
        if bufs is None:
            o, m, den = _dilated_prompt(q, k.astype(jnp.float32), v.astype(jnp.float32), d)
            wbp = min(DIL_WINDOWS[g], t)
            new_bufs.append(kv_new[:, t - wbp:])
        else:
            buf = bufs[g]
            wb = buf.shape[1]
            kv_all = jnp.concatenate([buf, kv_new.astype(buf.dtype)], axis=1)
            o, m, den = _dilated_sample(q, kv_all[:, :, 0].astype(jnp.float32), kv_all[:, :, 1].astype(jnp.float32), d, wb)
            new_bufs.append(kv_all[:, t:])
        outs.append(o)
        ms.append(m)
        dens.append(den)
    m_all = jnp.maximum(jnp.maximum(ms[0], ms[1]), ms[2])
    ws = [den * jnp.exp(m - m_all) for m, den in zip(ms, dens)]
    tot = ws[0] + ws[1] + ws[2]
    merged = (ws[0][..., None] * outs[0] + ws[1][..., None] * outs[1] + ws[2][..., None] * outs[2]) / tot[..., None]
    merged = merged.reshape(b, t, DIL_W).astype(xn.dtype)
    return (merged * jax.nn.silu(gate)) @ w_out, new_bufs


def _gmlp_mixer(xn, w_in, ln_g, ln_b, w_s, b_s, w_out):
    b, t, _ = xn.shape
    proj = xn @ w_in
    z = jax.nn.gelu(proj[..., :2 * GM_WIDTH])
    u, v = z[..., :GM_WIDTH], z[..., GM_WIDTH:]
    gate = proj[..., 2 * GM_WIDTH:]
    v = _layernorm(v, ln_g, ln_b)
    c = GM_CHUNK if t % GM_CHUNK == 0 else t
    nc = t // c
    wm = jnp.tril(w_s[:, :c, :c])
    vr = v.reshape(b, nc, c, GM_GROUPS, GM_GD)
    mixed = jnp.einsum('gij,bcjgd->bcigd', wm, vr) + b_s[:, :c].T[None, None, :, :, None]
    mixed = mixed.reshape(b, t, GM_WIDTH).astype(xn.dtype)
    return (u * mixed * jax.nn.silu(gate)) @ w_out, v


def _ple(h, p_i, w_pe, g_pl, w_pg):
    return h + (p_i @ w_pe) * jax.nn.sigmoid(_rmsnorm(h, g_pl) @ w_pg)


def setup_inputs(seed: int = 0) -> dict:
    key = jax.random.key(seed)
    ks = jax.random.split(key, 24)

    def nrm(k, shape, scale):
        return jax.random.normal(k, shape, jnp.float32) * scale

    wins = [min(w, PAST_LEN) for w in DIL_WINDOWS]
    return {
        'x_prompt': nrm(ks[0], (BATCH, SEQ, D_MODEL), 1.0),
        'x_sample': nrm(ks[1], (DEC_BATCH, DEC_SEQ, D_MODEL), 1.0),
        'state_ret': nrm(ks[2], (N_A, DEC_BATCH, RET_HEADS, RET_DK, RET_DV), 0.1),
        'cache_win_g0': nrm(ks[3], (N_B, DEC_BATCH, wins[0], 2, DIL_HEADS, DIL_HD), 1.0),
        'cache_win_g1': nrm(ks[4], (N_B, DEC_BATCH, wins[1], 2, DIL_HEADS, DIL_HD), 1.0),
        'cache_win_g2': nrm(ks[5], (N_B, DEC_BATCH, wins[2], 2, DIL_HEADS, DIL_HD), 1.0),
        'p_prompt': nrm(ks[6], (DEPTH, BATCH, SEQ, PLE_DIM), 1.0),
        'p_sample': nrm(ks[7], (DEPTH, DEC_BATCH, DEC_SEQ, PLE_DIM), 1.0),
        'norm_g': 1.0 + nrm(ks[8], (DEPTH, D_MODEL), 0.02),
        'ret_w_in': nrm(ks[9], (N_A, D_MODEL, RET_IN), D_MODEL ** -0.5),
        'ret_w_out': nrm(ks[10], (N_A, RET_V, D_MODEL), RET_V ** -0.5),
        'dil_w_in': nrm(ks[11], (N_B, D_MODEL, DIL_IN), D_MODEL ** -0.5),
        'dil_w_out': nrm(ks[12], (N_B, DIL_W, D_MODEL), DIL_W ** -0.5),
        'gm_w_in': nrm(ks[13], (N_C, D_MODEL, GM_IN), D_MODEL ** -0.5),
        'gm_ln_g': 1.0 + nrm(ks[14], (N_C, GM_WIDTH), 0.02),
        'gm_ln_b': nrm(ks[15], (N_C, GM_WIDTH), 0.02),
        'gm_w_s': nrm(ks[16], (N_C, GM_GROUPS, GM_CHUNK, GM_CHUNK), GM_CHUNK ** -0.5),
        'gm_b_s': nrm(ks[17], (N_C, GM_GROUPS, GM_CHUNK), 0.02),
        'gm_w_out': nrm(ks[18], (N_C, GM_WIDTH, D_MODEL), GM_WIDTH ** -0.5),
        'ple_w': nrm(ks[19], (DEPTH, PLE_DIM, D_MODEL), PLE_DIM ** -0.5),
        'ple_norm_g': 1.0 + nrm(ks[20], (DEPTH, D_MODEL), 0.02),
        'ple_w_gate': nrm(ks[21], (DEPTH, D_MODEL, D_MODEL), D_MODEL ** -0.5),
        'final_norm_g': 1.0 + nrm(ks[22], (D_MODEL,), 0.02),
    }


def reference(x_prompt, x_sample, state_ret, cache_win_g0, cache_win_g1, cache_win_g2, p_prompt, p_sample,
              norm_g, ret_w_in, ret_w_out, dil_w_in, dil_w_out, gm_w_in, gm_ln_g, gm_ln_b, gm_w_s, gm_b_s,
              gm_w_out, ple_w, ple_norm_g, ple_w_gate, final_norm_g):
    pos_p = jnp.arange(x_prompt.shape[1], dtype=jnp.int32)
    pos_s = PAST_LEN + jnp.arange(x_sample.shape[1], dtype=jnp.int32)
    ret_p, ret_s, gm_s = [], [], []
    win_p = [[], [], []]
    win_s = [[], [], []]
    xp, xs = x_prompt, x_sample
    for i in range(DEPTH):
        kind, j = i % N_MIXERS, i // N_MIXERS
        hp = _rmsnorm(xp, norm_g[i])
        hs = _rmsnorm(xs, norm_g[i])
        if kind == 0:
            yp, sp = _retention_mixer(hp, pos_p, None, ret_w_in[j], ret_w_out[j])
            ys, ss = _retention_mixer(hs, pos_s, state_ret[j], ret_w_in[j], ret_w_out[j])
            ret_p.append(sp.astype(x_prompt.dtype))
            ret_s.append(ss.astype(state_ret.dtype))
        elif kind == 1:
            yp, bp = _dilated_mixer(hp, pos_p, None, dil_w_in[j], dil_w_out[j])
            ys, bs = _dilated_mixer(hs, pos_s, (cache_win_g0[j], cache_win_g1[j], cache_win_g2[j]), dil_w_in[j], dil_w_out[j])
            for g in range(DIL_GROUPS):
                win_p[g].append(bp[g])
                win_s[g].append(bs[g])
        else:
            yp, _ = _gmlp_mixer(hp, gm_w_in[j], gm_ln_g[j], gm_ln_b[j], gm_w_s[j], gm_b_s[j], gm_w_out[j])
            ys, vs = _gmlp_mixer(hs, gm_w_in[j], gm_ln_g[j], gm_ln_b[j], gm_w_s[j], gm_b_s[j], gm_w_out[j])
            gm_s.append(vs)
        xp = _ple(xp + yp, p_prompt[i], ple_w[i], ple_norm_g[i], ple_w_gate[i])
        xs = _ple(xs + ys, p_sample[i], ple_w[i], ple_norm_g[i], ple_w_gate[i])
    y_prompt = _rmsnorm(xp, final_norm_g)
    y_sample = _rmsnorm(xs, final_norm_g)
    return (y_prompt, y_sample,
            jnp.stack(ret_p), jnp.stack(ret_s),
            jnp.stack(win_p[0]), jnp.stack(win_s[0]),
            jnp.stack(win_p[1]), jnp.stack(win_s[1]),
            jnp.stack(win_p[2]), jnp.stack(win_s[2]),
            jnp.stack(gm_s))
```

```python
import functools

import jax
import jax.numpy as jnp
from jax import lax
from jax.experimental import pallas as pl
from jax.experimental.pallas import tpu as pltpu

F32 = jnp.float32
BF16 = jnp.bfloat16

D_MODEL = 2048
PAST_LEN = 16384
PLE_DIM = 256
ROPE_THETA = 10000.0
EPS = 1e-6
NEG = -1e30

RET_HEADS = 8
RET_DK = 256
RET_DV = 512
RET_QK = RET_HEADS * RET_DK
RET_V = RET_HEADS * RET_DV
RET_CHUNK = 128

DIL_RATES = (1, 4, 16)
DIL_GROUPS = 3
DIL_SPAN = 128
DIL_HEADS = 16
DIL_HD = 128
DIL_W = DIL_HEADS * DIL_HD

GM_WIDTH = 2 * D_MODEL
GM_GROUPS = 16
GM_GD = GM_WIDTH // GM_GROUPS
GM_CHUNK = 128

LANE = 128
SAMPLE_PAD = 16
VMEM_LIMIT = 56 * 1024 * 1024


def _params(n_axes, vmem=VMEM_LIMIT):
    return pltpu.CompilerParams(dimension_semantics=("arbitrary",) * n_axes, vmem_limit_bytes=vmem)


def _silu(x):
    return x * jax.nn.sigmoid(x)


def _rms_body(x_ref, g_ref, o_ref):
    x = x_ref[...]
    y = x * lax.rsqrt(jnp.mean(x * x, axis=-1, keepdims=True) + EPS)
    o_ref[...] = (y * g_ref[...]).astype(o_ref.dtype)


def _rmsnorm(x, g3, layer, out_dtype, tr):
    r, d = x.shape
    return pl.pallas_call(
        _rms_body,
        grid=(r // tr,),
        in_specs=[pl.BlockSpec((tr, d), lambda i: (i, 0)),
                  pl.BlockSpec((None, 1, d), lambda i: (layer, 0, 0))],
        out_specs=pl.BlockSpec((tr, d), lambda i: (i, 0)),
        out_shape=jax.ShapeDtypeStruct((r, d), out_dtype),
        compiler_params=_params(1),
        name="rmsnorm",
    )(x, g3)


def _mm_body(x_ref, w_ref, *refs, n_extra, epilogue):
    extra = refs[:n_extra]
    outs = refs[n_extra:-1]
    wb_ref = refs[-1]

    @pl.when(pl.program_id(1) == 0)
    def _():
        wb_ref[...] = w_ref[...].astype(BF16)

    acc = jnp.dot(x_ref[...], wb_ref[...], preferred_element_type=F32)
    epilogue(acc, extra, outs)


def _matmul(x, w3, layer, n0, n, tm, tn, epilogue, extras=(), extra_specs=(), out_dtypes=(F32,), name="matmul"):
    m, k = x.shape
    j0 = n0 // tn
    in_specs = [pl.BlockSpec((tm, k), lambda j, i: (i, 0)),
                pl.BlockSpec((None, k, tn), lambda j, i: (layer, 0, j + j0))] + list(extra_specs)
    out_specs = [pl.BlockSpec((tm, tn), lambda j, i: (i, j)) for _ in out_dtypes]
    out_shape = [jax.ShapeDtypeStruct((m, n), dt) for dt in out_dtypes]
    res = pl.pallas_call(
        functools.partial(_mm_body, n_extra=len(extras), epilogue=epilogue),
        grid=(n // tn, m // tm),
        in_specs=in_specs,
        out_specs=out_specs,
        out_shape=out_shape,
        scratch_shapes=[pltpu.VMEM((k, tn), BF16)],
        compiler_params=_params(2),
        name=name,
    )(x, w3, *extras)
    return res[0] if len(res) == 1 else res


def _ep_plain(acc, extra, outs):
    outs[0][...] = acc.astype(outs[0].dtype)


def _ep_gelu(acc, extra, outs):
    outs[0][...] = jax.nn.gelu(acc).astype(outs[0].dtype)


def _ep_residual(acc, extra, outs):
    outs[0][...] = (extra[0][...] + acc).astype(outs[0].dtype)


def _ep_rope_ret(acc, extra, outs, *, scale):
    cos = extra[0][...]
    sin = extra[1][...]
    o = outs[0]
    half = RET_DK // 2
    for h in range(acc.shape[1] // RET_DK):
        x1 = acc[:, h * RET_DK:h * RET_DK + half]
        x2 = acc[:, h * RET_DK + half:(h + 1) * RET_DK]
        o[:, h * RET_DK:h * RET_DK + half] = ((x1 * cos - x2 * sin) * scale).astype(o.dtype)
        o[:, h * RET_DK + half:(h + 1) * RET_DK] = ((x2 * cos + x1 * sin) * scale).astype(o.dtype)


def _ep_rope_dil(acc, extra, outs, *, scale):
    cos2 = extra[0][...]
    sin2 = extra[1][...]
    o = outs[0]
    for h in range(acc.shape[1] // DIL_HD):
        x = acc[:, h * DIL_HD:(h + 1) * DIL_HD]
        swapped = pltpu.roll(x, DIL_HD // 2, axis=1)
        o[:, h * DIL_HD:(h + 1) * DIL_HD] = ((x * cos2 + swapped * sin2) * scale).astype(o.dtype)


def _ep_ple(acc, extra, outs):
    pe = jnp.dot(extra[1][...], extra[2][...].astype(BF16), preferred_element_type=F32)
    outs[0][...] = extra[0][...] + pe * jax.nn.sigmoid(acc)


def _rope_tables(pos, half):
    inv = ROPE_THETA ** (-jnp.arange(half, dtype=F32) / half)
    ang = pos.astype(F32)[:, None] * inv[None, :]
    return jnp.cos(ang), jnp.sin(ang)


def _ret_tables(c, c_pad):
    lg = jnp.log1p(-jnp.exp2(-5.0 - jnp.arange(RET_HEADS, dtype=F32)))
    i = jnp.arange(c_pad, dtype=F32)
    diff = i[:, None] - i[None, :]
    dmat = jnp.where(diff[None] >= 0, jnp.exp(lg[:, None, None] * jnp.maximum(diff, 0.0)[None]), 0.0)
    xi = jnp.exp(lg[:, None] * (i[None, :] + 1.0))[:, :, None]
    zeta = jnp.exp(lg[:, None] * (c - 1.0 - i[None, :]))[:, :, None]
    gc = jnp.broadcast_to(jnp.exp(lg * c)[:, None, None], (RET_HEADS, 1, RET_DV))
    return dmat, xi, zeta, gc


def _ret_finish(o, g):
    on = o * lax.rsqrt(jnp.mean(o * o, axis=-1, keepdims=True) + EPS)
    return on * _silu(g)


_NT = (((1,), (1,)), ((), ()))
_TN = (((0,), (0,)), ((), ()))


def _ret_prompt_body(q_ref, k_ref, v_ref, g_ref, dmat_ref, xi_ref, zeta_ref, gc_ref, o_ref, s_ref, *, c, cb):
    @pl.when(pl.program_id(2) == 0)
    def _():
        s_ref[...] = jnp.zeros_like(s_ref)

    dmat = dmat_ref[...]
    xi = xi_ref[...]
    zeta = zeta_ref[...]
    gc = gc_ref[...]
    for i in range(cb):
        rows = slice(i * c, (i + 1) * c)
        q = q_ref[rows, :]
        k = k_ref[rows, :]
        v = v_ref[rows, :]
        s = s_ref[...]
        sc = lax.dot_general(q, k, _NT, preferred_element_type=F32) * dmat
        o = jnp.dot(sc.astype(BF16), v, preferred_element_type=F32)
        o = o + jnp.dot(q, s.astype(BF16), preferred_element_type=F32) * xi
        kz = (k.astype(F32) * zeta).astype(BF16)
        s_ref[...] = s * gc + lax.dot_general(kz, v, _TN, preferred_element_type=F32)
        o_ref[rows, :] = _ret_finish(o, g_ref[rows, :].astype(F32)).astype(o_ref.dtype)


def _retention_prompt(q, k, vg, batch, seq):
    c = RET_CHUNK
    cb = 8
    rb = c * cb
    ncb = seq // rb
    dmat, xi, zeta, gc = _ret_tables(c, c)
    row = lambda b, h, i: (b * ncb + i, h)
    tab = lambda b, h, i: (h, 0, 0)
    return pl.pallas_call(
        functools.partial(_ret_prompt_body, c=c, cb=cb),
        grid=(batch, RET_HEADS, ncb),
        in_specs=[pl.BlockSpec((rb, RET_DK), row),
                  pl.BlockSpec((rb, RET_DK), row),
                  pl.BlockSpec((rb, RET_DV), row),
                  pl.BlockSpec((rb, RET_DV), lambda b, h, i: (b * ncb + i, RET_HEADS + h)),
                  pl.BlockSpec((None, c, c), tab),
                  pl.BlockSpec((None, c, 1), tab),
                  pl.BlockSpec((None, c, 1), tab),
                  pl.BlockSpec((None, 1, RET_DV), tab)],
        out_specs=[pl.BlockSpec((rb, RET_DV), row),
                   pl.BlockSpec((None, None, RET_DK, RET_DV), lambda b, h, i: (b, h, 0, 0))],
        out_shape=[jax.ShapeDtypeStruct((batch * seq, RET_V), BF16),
                   jax.ShapeDtypeStruct((batch, RET_HEADS, RET_DK, RET_DV), F32)],
        compiler_params=_params(3),
        name="retention_prompt",
    )(q, k, vg, vg, dmat, xi, zeta, gc)


def _ret_sample_body(q_ref, k_ref, v_ref, g_ref, s0_ref, dmat_ref, xi_ref, zeta_ref, gc_ref, o_ref, s_ref, *, nb):
    q = q_ref[...].astype(BF16)
    k = k_ref[...]
    v = v_ref[...].astype(BF16)
    xi = xi_ref[...]
    gc = gc_ref[...]
    sc = lax.dot_general(q, k.astype(BF16), _NT, preferred_element_type=F32) * dmat_ref[...]
    o = jnp.dot(sc.astype(BF16), v, preferred_element_type=F32)
    kz = (k * zeta_ref[...]).astype(BF16)
    row_batch = lax.broadcasted_iota(jnp.int32, kz.shape, 0) // SAMPLE_PAD
    cross = []
    for b in range(nb):
        s0 = s0_ref[b]
        qb = q[b * SAMPLE_PAD:(b + 1) * SAMPLE_PAD, :]
        cross.append(jnp.dot(qb, s0.astype(BF16), preferred_element_type=F32))
        kzb = jnp.where(row_batch == b, kz, jnp.zeros_like(kz))
        s_ref[b] = s0 * gc + lax.dot_general(kzb, v, _TN, preferred_element_type=F32)
    o = o + jnp.concatenate(cross, axis=0) * xi
    o_ref[...] = _ret_finish(o, g_ref[...]).astype(o_ref.dtype)


def _retention_sample(q, k, vg, state, layer, nb, t):
    rows = nb * SAMPLE_PAD
    dmat, xi, zeta, gc = _ret_tables(t, SAMPLE_PAD)
    eye = jnp.eye(nb, dtype=F32)
    dbig = jnp.einsum("ab,hij->haibj", eye, dmat).reshape(RET_HEADS, rows, rows)
    xib = jnp.tile(xi, (1, nb, 1))
    zetab = jnp.tile(zeta, (1, nb, 1))
    col = lambda h: (0, h)
    tab = lambda h: (h, 0, 0)
    return pl.pallas_call(
        functools.partial(_ret_sample_body, nb=nb),
        grid=(RET_HEADS,),
        in_specs=[pl.BlockSpec((rows, RET_DK), col),
                  pl.BlockSpec((rows, RET_DK), col),
                  pl.BlockSpec((rows, RET_DV), col),
                  pl.BlockSpec((rows, RET_DV), lambda h: (0, RET_HEADS + h)),
                  pl.BlockSpec((None, nb, None, RET_DK, RET_DV), lambda h: (layer, 0, h, 0, 0)),
                  pl.BlockSpec((None, rows, rows), tab),
                  pl.BlockSpec((None, rows, 1), tab),
                  pl.BlockSpec((None, rows, 1), tab),
                  pl.BlockSpec((None, 1, RET_DV), tab)],
        out_specs=[pl.BlockSpec((rows, RET_DV), col),
                   pl.BlockSpec((nb, None, RET_DK, RET_DV), lambda h: (0, h, 0, 0))],
        out_shape=[jax.ShapeDtypeStruct((rows, RET_V), BF16),
                   jax.ShapeDtypeStruct((nb, RET_HEADS, RET_DK, RET_DV), F32)],
        compiler_params=_params(1),
        name="retention_sample",
    )(q, k, vg, vg, state, dbig, xib, zetab, gc)


def _dil_prompt_body(q_ref, kc_ref, kp_ref, vc_ref, vp_ref, acc_ref, st_ref):
    has_prev = pl.program_id(2) > 0
    ii = lax.broadcasted_iota(jnp.int32, (DIL_SPAN, DIL_SPAN), 0)
    jj = lax.broadcasted_iota(jnp.int32, (DIL_SPAN, DIL_SPAN), 1)
    mask_c = jj <= ii
    mask_p = jnp.logical_and(jj >= ii, has_prev)
    st_ref[...] = jnp.zeros_like(st_ref)
    for h in range(DIL_HEADS):
        hs = slice(h * DIL_HD, (h + 1) * DIL_HD)
        q = q_ref[:, hs]
        s_c = jnp.where(mask_c, lax.dot_general(q, kc_ref[:, hs], _NT, preferred_element_type=F32), NEG)
        s_p = jnp.where(mask_p, lax.dot_general(q, kp_ref[:, hs], _NT, preferred_element_type=F32), NEG)
        m = jnp.maximum(jnp.max(s_c, axis=-1, keepdims=True), jnp.max(s_p, axis=-1, keepdims=True))
        p_c = jnp.exp(s_c - m)
        p_p = jnp.exp(s_p - m)
        den = jnp.sum(p_c, axis=-1, keepdims=True) + jnp.sum(p_p, axis=-1, keepdims=True)
        acc = jnp.dot(p_c.astype(BF16), vc_ref[:, hs], preferred_element_type=F32)
        acc = acc + jnp.dot(p_p.astype(BF16), vp_ref[:, hs], preferred_element_type=F32)
        acc_ref[:, hs] = acc
        st_ref[:, h:h + 1] = m
        st_ref[:, DIL_HEADS + h:DIL_HEADS + h + 1] = den


def _dilated_prompt(q3, k3, v3, g, batch, seq):
    d = DIL_RATES[g]
    rows = batch * seq
    nb = seq // (DIL_SPAN * d)
    cols = DIL_GROUPS * DIL_W
    qv, kv, vv = (a.reshape(rows // d, d * cols) for a in (q3, k3, v3))
    cur = lambda b, r, n: (b * nb + n, r * DIL_GROUPS + g)
    prev = lambda b, r, n: (b * nb + jnp.maximum(n - 1, 0), r * DIL_GROUPS + g)
    out = lambda b, r, n: (b * nb + n, r)
    blk = (DIL_SPAN, DIL_W)
    acc, st = pl.pallas_call(
        _dil_prompt_body,
        grid=(batch, d, nb),
        in_specs=[pl.BlockSpec(blk, cur), pl.BlockSpec(blk, cur), pl.BlockSpec(blk, prev),
                  pl.BlockSpec(blk, cur), pl.BlockSpec(blk, prev)],
        out_specs=[pl.BlockSpec(blk, out), pl.BlockSpec((DIL_SPAN, LANE), out)],
        out_shape=[jax.ShapeDtypeStruct((rows // d, d * DIL_W), F32),
                   jax.ShapeDtypeStruct((rows // d, d * LANE), F32)],
        compiler_params=_params(3),
        name=f"dilated_prompt_g{g}",
    )(qv, kv, kv, vv, vv)
    return acc.reshape(rows, DIL_W), st.reshape(rows, LANE)


def _dil_merge_body(a0_ref, a1_ref, a2_ref, s0_ref, s1_ref, s2_ref, gate_ref, o_ref):
    st = [s0_ref[...], s1_ref[...], s2_ref[...]]
    ms = [s[:, :DIL_HEADS] for s in st]
    dens = [s[:, DIL_HEADS:2 * DIL_HEADS] for s in st]
    m_all = jnp.maximum(jnp.maximum(ms[0], ms[1]), ms[2])
    es = [jnp.exp(m - m_all) for m in ms]
    tot = es[0] * dens[0] + es[1] * dens[1] + es[2] * dens[2]
    accs = (a0_ref, a1_ref, a2_ref)
    for h in range(DIL_HEADS):
        hs = slice(h * DIL_HD, (h + 1) * DIL_HD)
        num = es[0][:, h:h + 1] * accs[0][:, hs]
        num = num + es[1][:, h:h + 1] * accs[1][:, hs]
        num = num + es[2][:, h:h + 1] * accs[2][:, hs]
        merged = num / tot[:, h:h + 1]
        o_ref[:, hs] = (merged * _silu(gate_ref[:, hs].astype(F32))).astype(o_ref.dtype)


def _dilated_merge(accs, sts, gate, tr):
    rows = gate.shape[0]
    wide = pl.BlockSpec((tr, DIL_W), lambda i: (i, 0))
    narrow = pl.BlockSpec((tr, LANE), lambda i: (i, 0))
    return pl.pallas_call(
        _dil_merge_body,
        grid=(rows // tr,),
        in_specs=[wide, wide, wide, narrow, narrow, narrow, wide],
        out_specs=wide,
        out_shape=jax.ShapeDtypeStruct((rows, DIL_W), BF16),
        compiler_params=_params(1),
        name="dilated_merge",
    )(*accs, *sts, gate)


def _dil_sample_body(q_ref, kn_ref, vn_ref, gate_ref, kc0, vc0, kc1, vc1, kc2, vc2, o_ref):
    t = pl.program_id(1)

    @pl.when(t == 0)
    def _():
        o_ref[...] = jnp.zeros_like(o_ref)

    n_new = kn_ref.shape[0]
    n_keys = 2 * DIL_SPAN
    head_of_lane = lax.broadcasted_iota(jnp.int32, (DIL_HEADS, DIL_W), 1) // DIL_HD
    hmask = head_of_lane == lax.broadcasted_iota(jnp.int32, (DIL_HEADS, DIL_W), 0)
    jk = lax.broadcasted_iota(jnp.int32, (DIL_HEADS, n_keys), 1)
    pad = jnp.zeros((n_keys - DIL_SPAN - n_new, DIL_W), BF16)
    caches = ((kc0, vc0), (kc1, vc1), (kc2, vc2))
    ms, dens, accs = [], [], []
    for g in range(DIL_GROUPS):
        gs = slice(g * DIL_W, (g + 1) * DIL_W)
        qrow = q_ref[pl.ds(t, 1), gs]
        qm = jnp.where(hmask, jnp.broadcast_to(qrow, (DIL_HEADS, DIL_W)), 0.0).astype(BF16)
        k_all = jnp.concatenate([caches[g][0][...].astype(BF16), kn_ref[:, gs].astype(BF16), pad], axis=0)
        v_all = jnp.concatenate([caches[g][1][...].astype(BF16), vn_ref[:, gs].astype(BF16), pad], axis=0)
        s = lax.dot_general(qm, k_all, _NT, preferred_element_type=F32)
        new_idx = jk - DIL_SPAN
        if DIL_RATES[g] == 1:
            valid = jnp.logical_and(jk >= t, new_idx <= t)
        else:
            valid = jnp.logical_or(jk < DIL_SPAN, new_idx == t)
        s = jnp.where(valid, s, NEG)
        m = jnp.max(s, axis=-1, keepdims=True)
        p = jnp.exp(s - m)
        ms.append(m)
        dens.append(jnp.sum(p, axis=-1, keepdims=True))
        accs.append(jnp.dot(p.astype(BF16), v_all, preferred_element_type=F32))
    m_all = jnp.maximum(jnp.maximum(ms[0], ms[1]), ms[2])
    es = [jnp.exp(m - m_all) for m in ms]
    tot = es[0] * dens[0] + es[1] * dens[1] + es[2] * dens[2]
    merged = (es[0] * accs[0] + es[1] * accs[1] + es[2] * accs[2]) / tot
    row = jnp.sum(jnp.where(hmask, merged, 0.0), axis=0, keepdims=True)
    o_ref[pl.ds(t, 1), :] = row * _silu(gate_ref[pl.ds(t, 1), :])


def _dilated_sample(q3, k3, v3, gate, caches, layer, nb, t):
    kv = 2 * DIL_W
    views = []
    specs = []
    for g in range(DIL_GROUPS):
        d = DIL_RATES[g]
        wb = caches[g].shape[2]
        assert wb == DIL_SPAN * d and t <= d * (1 if d > 1 else DIL_SPAN)
        views.append(caches[g].reshape(caches[g].shape[0] * nb, DIL_SPAN, d * kv))
        if d == 1:
            kmap = lambda b, i: (layer * nb + b, 0, 0)
            vmap = lambda b, i: (layer * nb + b, 0, 1)
        else:
            kmap = lambda b, i: (layer * nb + b, 0, 2 * i)
            vmap = lambda b, i: (layer * nb + b, 0, 2 * i + 1)
        specs += [pl.BlockSpec((None, DIL_SPAN, DIL_W), kmap), pl.BlockSpec((None, DIL_SPAN, DIL_W), vmap)]
    rows3 = pl.BlockSpec((SAMPLE_PAD, DIL_GROUPS * DIL_W), lambda b, i: (b, 0))
    rows1 = pl.BlockSpec((SAMPLE_PAD, DIL_W), lambda b, i: (b, 0))
    operands = [q3, k3, v3, gate]
    for g in range(DIL_GROUPS):
        operands += [views[g], views[g]]
    return pl.pallas_call(
        _dil_sample_body,
        grid=(nb, t),
        in_specs=[rows3, rows3, rows3, rows1] + specs,
        out_specs=rows1,
        out_shape=jax.ShapeDtypeStruct((nb * SAMPLE_PAD, DIL_W), F32),
        compiler_params=_params(2),
        name="dilated_sample",
    )(*operands)


def _cache_shift_body(*refs, layer, nb, t):
    caches = refs[0:DIL_GROUPS]
    news = refs[DIL_GROUPS:2 * DIL_GROUPS]
    outs = refs[2 * DIL_GROUPS:3 * DIL_GROUPS]
    sem = refs[-1]
    copies = []
    for g in range(DIL_GROUPS):
        wb = caches[g].shape[2]
        for b in range(nb):
            copies.append(pltpu.make_async_copy(caches[g].at[layer, b, pl.ds(t, wb - t)],
                                                outs[g].at[b, pl.ds(0, wb - t)], sem.at[len(copies)]))
        copies.append(pltpu.make_async_copy(news[g], outs[g].at[:, pl.ds(wb - t, t)], sem.at[len(copies)]))
    for cp in copies:
        cp.start()
    for cp in copies:
        cp.wait()


def _cache_shift(caches, news, layer, nb, t):
    any_spec = pl.BlockSpec(memory_space=pl.ANY)
    return pl.pallas_call(
        functools.partial(_cache_shift_body, layer=layer, nb=nb, t=t),
        in_specs=[any_spec] * (2 * DIL_GROUPS),
        out_specs=[any_spec] * DIL_GROUPS,
        out_shape=[jax.ShapeDtypeStruct(c.shape[1:], c.dtype) for c in caches],
        scratch_shapes=[pltpu.SemaphoreType.DMA((DIL_GROUPS * (nb + 1),))],
        name="cache_shift",
    )(*caches, *news)


def _gmlp_body(u_ref, v_ref, gate_ref, lng_ref, lnb_ref, wm_ref, bs_ref, o_ref, *vn_ref):
    v = v_ref[...].astype(F32)
    mu = jnp.mean(v, axis=-1, keepdims=True)
    xc = v - mu
    vn = xc * lax.rsqrt(jnp.mean(xc * xc, axis=-1, keepdims=True) + EPS) * lng_ref[...] + lnb_ref[...]
    if vn_ref:
        vn_ref[0][...] = vn
    vb = vn.astype(BF16)
    for g in range(GM_GROUPS):
        gs = slice(g * GM_GD, (g + 1) * GM_GD)
        mixed = jnp.dot(wm_ref[g], vb[:, gs], preferred_element_type=F32) + bs_ref[g]
        o_ref[:, gs] = (u_ref[:, gs].astype(F32) * mixed * _silu(gate_ref[:, gs].astype(F32))).astype(o_ref.dtype)


def _gmlp_core(uv, gate, ln_g3, ln_b3, layer, wm, bs, want_vn):
    rows = gate.shape[0]
    c = wm.shape[1]
    row = lambda i: (i, 0)
    out_specs = [pl.BlockSpec((c, GM_WIDTH), row)]
    out_shape = [jax.ShapeDtypeStruct((rows, GM_WIDTH), BF16)]
    if want_vn:
        out_specs.append(pl.BlockSpec((c, GM_WIDTH), row))
        out_shape.append(jax.ShapeDtypeStruct((rows, GM_WIDTH), F32))
    res = pl.pallas_call(
        _gmlp_body,
        grid=(rows // c,),
        in_specs=[pl.BlockSpec((c, GM_WIDTH), row),
                  pl.BlockSpec((c, GM_WIDTH), lambda i: (i, 1)),
                  pl.BlockSpec((c, GM_WIDTH), row),
                  pl.BlockSpec((None, 1, GM_WIDTH), lambda i: (layer, 0, 0)),
                  pl.BlockSpec((None, 1, GM_WIDTH), lambda i: (layer, 0, 0)),
                  pl.BlockSpec((GM_GROUPS, c, c), lambda i: (0, 0, 0)),
                  pl.BlockSpec((GM_GROUPS, c, 1), lambda i: (0, 0, 0))],
        out_specs=out_specs,
        out_shape=out_shape,
        compiler_params=_params(1),
        name="gmlp_core",
    )(uv, uv, gate, ln_g3, ln_b3, wm, bs)
    return res


class _Stream:
    def __init__(self, batch, t, t_pad, pos0, tm, act_dtype):
        self.batch, self.t, self.t_pad, self.tm, self.act = batch, t, t_pad, tm, act_dtype
        self.rows = batch * t_pad
        pos = pos0 + jnp.arange(t_pad, dtype=jnp.int32)
        cos, sin = _rope_tables(pos, RET_DK // 2)
        self.ret_rope = (cos, sin)
        cos, sin = _rope_tables(pos, DIL_HD // 2)
        self.dil_rope = (jnp.concatenate([cos, cos], axis=-1), jnp.concatenate([-sin, sin], axis=-1))
        self.rope_tiles = max(t_pad // tm, 1)
        if t_pad < tm:
            rep = tm // t_pad
            self.ret_rope = tuple(jnp.tile(a, (rep, 1)) for a in self.ret_rope)
            self.dil_rope = tuple(jnp.tile(a, (rep, 1)) for a in self.dil_rope)

    def rope_specs(self):
        nt = self.rope_tiles
        return [pl.BlockSpec((self.tm, LANE), lambda j, i: (i % nt, 0))] * 2


def _proj(st, h, w3, layer, n0, n, tn, epilogue, out_dtype, extras=(), extra_specs=(), name="proj"):
    return _matmul(h, w3, layer, n0, n, st.tm, tn, epilogue, extras, extra_specs, (out_dtype,), name)


def _out_and_ple(st, y_in, w_out3, jl, x, i, p3, ple_w, ple_g3, ple_w_gate):
    tm = st.tm
    tn = 512
    res_spec = pl.BlockSpec((tm, tn), lambda j, m: (m, j))
    x1 = _matmul(y_in, w_out3, jl, 0, D_MODEL, tm, tn, _ep_residual, (x,), (res_spec,), (F32,), "out_proj")
    hn = _rmsnorm(x1, ple_g3, i, BF16, min(256, st.rows))
    tn = 512
    extras = (x1, p3, ple_w)
    specs = (pl.BlockSpec((tm, tn), lambda j, m: (m, j)),
             pl.BlockSpec((None, tm, PLE_DIM), lambda j, m: (i, m, 0)),
             pl.BlockSpec((None, PLE_DIM, tn), lambda j, m: (i, 0, j)))
    return _matmul(hn, ple_w_gate, i, 0, D_MODEL, tm, tn, _ep_ple, extras, specs, (F32,), "ple")


def _retention_layer(st, h, w_in, jl, state):
    tn = 1024
    q = _proj(st, h, w_in, jl, 0, RET_QK, tn, functools.partial(_ep_rope_ret, scale=1.0), st.act,
              st.ret_rope, st.rope_specs(), "ret_q")
    k = _proj(st, h, w_in, jl, RET_QK, RET_QK, tn, functools.partial(_ep_rope_ret, scale=RET_DK ** -0.5), st.act,
              st.ret_rope, st.rope_specs(), "ret_k")
    vg = _proj(st, h, w_in, jl, 2 * RET_QK, 2 * RET_V, tn, _ep_plain, st.act, name="ret_vg")
    if state is None:
        return _retention_prompt(q, k, vg, st.batch, st.t)
    return _retention_sample(q, k, vg, state, jl, st.batch, st.t)


def _dilated_proj(st, h, w_in, jl):
    tn = 1024
    per_group = 3 * DIL_W

    def grouped(kind, epilogue, extras=(), extra_specs=(), name=""):
        tiles = DIL_W // tn
        m, k = h.shape
        in_specs = [pl.BlockSpec((st.tm, k), lambda j, i: (i, 0)),
                    pl.BlockSpec((None, k, tn),
                                 lambda j, i: (jl, 0, (j // tiles) * (per_group // tn) + kind * tiles + j % tiles))]
        return pl.pallas_call(
            functools.partial(_mm_body, n_extra=len(extras), epilogue=epilogue),
            grid=(DIL_GROUPS * tiles, m // st.tm),
            in_specs=in_specs + list(extra_specs),
            out_specs=[pl.BlockSpec((st.tm, tn), lambda j, i: (i, j))],
            out_shape=[jax.ShapeDtypeStruct((m, DIL_GROUPS * DIL_W), st.act)],
            scratch_shapes=[pltpu.VMEM((k, tn), BF16)],
            compiler_params=_params(2),
            name=name,
        )(h, w_in, *extras)[0]

    q3 = grouped(0, functools.partial(_ep_rope_dil, scale=DIL_HD ** -0.5), st.dil_rope, st.rope_specs(), "dil_q")
    k3 = grouped(1, functools.partial(_ep_rope_dil, scale=1.0), st.dil_rope, st.rope_specs(), "dil_k")
    v3 = grouped(2, _ep_plain, name="dil_v")
    gate = _proj(st, h, w_in, jl, DIL_GROUPS * per_group, DIL_W, tn, _ep_plain, st.act, name="dil_gate")
    return q3, k3, v3, gate


def _window_rows(k3, v3, st, g, n_rows):
    gs = slice(g * DIL_W, (g + 1) * DIL_W)
    k = k3[:, gs].reshape(st.batch, st.t_pad, DIL_HEADS, DIL_HD)[:, st.t - n_rows:st.t]
    v = v3[:, gs].reshape(st.batch, st.t_pad, DIL_HEADS, DIL_HD)[:, st.t - n_rows:st.t]
    return jnp.stack([k, v], axis=2).astype(F32)


def _gmlp_layer(st, h, w_in, jl, ln_g3, ln_b3, wm, bs, want_vn):
    tn = 1024
    uv = _proj(st, h, w_in, jl, 0, 2 * GM_WIDTH, tn, _ep_gelu, st.act, name="gm_uv")
    gate = _proj(st, h, w_in, jl, 2 * GM_WIDTH, GM_WIDTH, tn, _ep_plain, st.act, name="gm_gate")
    return _gmlp_core(uv, gate, ln_g3, ln_b3, jl, wm, bs, want_vn)


def kernel(x_prompt, x_sample, state_ret, cache_win_g0, cache_win_g1, cache_win_g2, p_prompt, p_sample, norm_g,
           ret_w_in, ret_w_out, dil_w_in, dil_w_out, gm_w_in, gm_ln_g, gm_ln_b, gm_w_s, gm_b_s, gm_w_out, ple_w,
           ple_norm_g, ple_w_gate, final_norm_g):
    depth = norm_g.shape[0]
    bp, sp, _ = x_prompt.shape
    bs_, ts, _ = x_sample.shape
    assert ts <= SAMPLE_PAD and ts % GM_CHUNK != 0 and sp % (DIL_SPAN * max(DIL_RATES)) == 0
    caches = (cache_win_g0, cache_win_g1, cache_win_g2)

    prm = _Stream(bp, sp, sp, 0, 512, BF16)
    smp = _Stream(bs_, ts, SAMPLE_PAD, PAST_LEN, bs_ * SAMPLE_PAD, F32)

    pad_t = ((0, 0), (0, SAMPLE_PAD - ts), (0, 0))
    xp = x_prompt.reshape(prm.rows, D_MODEL)
    xs = jnp.pad(x_sample, pad_t).reshape(smp.rows, D_MODEL)
    pp = p_prompt.astype(BF16).reshape(depth, prm.rows, PLE_DIM)
    ps = jnp.pad(p_sample, ((0, 0),) + pad_t).astype(BF16).reshape(depth, smp.rows, PLE_DIM)

    norm_g3 = norm_g[:, None, :]
    ple_g3 = ple_norm_g[:, None, :]
    ln_g3 = gm_ln_g[:, None, :]
    ln_b3 = gm_ln_b[:, None, :]
    fin_g3 = final_norm_g[None, None, :]

    ret_p, ret_s, gm_s = [], [], []
    win_p = [[], [], []]
    win_s = [[], [], []]
    for i in range(depth):
        kind, jl = i % 3, i // 3
        hp = _rmsnorm(xp, norm_g3, i, BF16, 256)
        hs = _rmsnorm(xs, norm_g3, i, BF16, smp.rows)
        if kind == 0:
            yp, sp_new = _retention_layer(prm, hp, ret_w_in, jl, None)
            ys, ss_new = _retention_layer(smp, hs, ret_w_in, jl, state_ret)
            ret_p.append(sp_new)
            ret_s.append(ss_new)
            w_out = ret_w_out
        elif kind == 1:
            q3, k3, v3, gate = _dilated_proj(prm, hp, dil_w_in, jl)
            parts = [_dilated_prompt(q3, k3, v3, g, bp, sp) for g in range(DIL_GROUPS)]
            yp = _dilated_merge([a for a, _ in parts], [s for _, s in parts], gate, 256)
            for g in range(DIL_GROUPS):
                win_p[g].append(_window_rows(k3, v3, prm, g, min(DIL_SPAN * DIL_RATES[g], sp)))
            q3s, k3s, v3s, gate_s = _dilated_proj(smp, hs, dil_w_in, jl)
            ys = _dilated_sample(q3s, k3s, v3s, gate_s, caches, jl, bs_, ts).astype(BF16)
            news = [_window_rows(k3s, v3s, smp, g, ts) for g in range(DIL_GROUPS)]
            shifted = _cache_shift(caches, news, jl, bs_, ts)
            for g in range(DIL_GROUPS):
                win_s[g].append(shifted[g])
            w_out = dil_w_out
        else:
            c = GM_CHUNK
            wm_p = jnp.tril(gm_w_s[jl]).astype(BF16)
            bs_p = gm_b_s[jl][:, :, None]
            (yp,) = _gmlp_layer(prm, hp, gm_w_in, jl, ln_g3, ln_b3, wm_p, bs_p, False)
            wm_t = jnp.pad(jnp.tril(gm_w_s[jl][:, :ts, :ts]), ((0, 0), (0, SAMPLE_PAD - ts), (0, SAMPLE_PAD - ts)))
            wm_s = jnp.einsum("ab,gij->gaibj", jnp.eye(bs_, dtype=F32), wm_t).reshape(GM_GROUPS, smp.rows, smp.rows)
            bs_s = jnp.tile(jnp.pad(gm_b_s[jl][:, :ts], ((0, 0), (0, SAMPLE_PAD - ts))), (1, bs_))[:, :, None]
            ys, vn = _gmlp_layer(smp, hs, gm_w_in, jl, ln_g3, ln_b3, wm_s.astype(BF16), bs_s, True)
            gm_s.append(vn.reshape(bs_, SAMPLE_PAD, GM_WIDTH)[:, :ts])
            w_out = gm_w_out
        xp = _out_and_ple(prm, yp, w_out, jl, xp, i, pp, ple_w, ple_g3, ple_w_gate)
        xs = _out_and_ple(smp, ys.astype(BF16), w_out, jl, xs, i, ps, ple_w, ple_g3, ple_w_gate)

    y_prompt = _rmsnorm(xp, fin_g3, 0, F32, 256).reshape(bp, sp, D_MODEL)
    y_sample = _rmsnorm(xs, fin_g3, 0, F32, smp.rows).reshape(bs_, SAMPLE_PAD, D_MODEL)[:, :ts]
    return (y_prompt, y_sample,
            jnp.stack(ret_p), jnp.stack(ret_s),
            jnp.stack(win_p[0]), jnp.stack(win_s[0]),
            jnp.stack(win_p[1]), jnp.stack(win_s[1]),
            jnp.stack(win_p[2]), jnp.stack(win_s[2]),
            jnp.stack(gm_s))
```

```python
import functools

import jax
import jax.numpy as jnp
from jax import lax
from jax.experimental import pallas as pl
from jax.experimental.pallas import tpu as pltpu

F32 = jnp.float32
BF16 = jnp.bfloat16

D_MODEL = 2048
PAST_LEN = 16384
PLE_DIM = 256
ROPE_THETA = 10000.0
EPS = 1e-6
NEG = -1e30

RET_HEADS = 8
RET_DK = 256
RET_DV = 512
RET_QK = RET_HEADS * RET_DK
RET_V = RET_HEADS * RET_DV
RET_CHUNK = 128

DIL_RATES = (1, 4, 16)
DIL_GROUPS = 3
DIL_SPAN = 128
DIL_HEADS = 16
DIL_HD = 128
DIL_W = DIL_HEADS * DIL_HD

GM_WIDTH = 2 * D_MODEL
GM_GROUPS = 16
GM_GD = GM_WIDTH // GM_GROUPS
GM_CHUNK = 128

LANE = 128
SAMPLE_PAD = 16
VMEM_LIMIT = 56 * 1024 * 1024


def _params(n_axes, vmem=VMEM_LIMIT):
    return pltpu.CompilerParams(dimension_semantics=("arbitrary",) * n_axes, vmem_limit_bytes=vmem)


def _silu(x):
    return x * jax.nn.sigmoid(x)


def _rms_body(x_ref, g_ref, o_ref):
    x = x_ref[...]
    y = x * lax.rsqrt(jnp.mean(x * x, axis=-1, keepdims=True) + EPS)
    o_ref[...] = (y * g_ref[...]).astype(o_ref.dtype)


def _rmsnorm(x, g3, layer, out_dtype, tr):
    r, d = x.shape
    return pl.pallas_call(
        _rms_body,
        grid=(r // tr,),
        in_specs=[pl.BlockSpec((tr, d), lambda i: (i, 0)),
                  pl.BlockSpec((None, 1, d), lambda i: (layer, 0, 0))],
        out_specs=pl.BlockSpec((tr, d), lambda i: (i, 0)),
        out_shape=jax.ShapeDtypeStruct((r, d), out_dtype),
        compiler_params=_params(1),
        name="rmsnorm",
    )(x, g3)


def _mm_body(x_ref, w_ref, *refs, n_extra, epilogue):
    extra = refs[:n_extra]
    outs = refs[n_extra:-1]
    wb_ref = refs[-1]

    @pl.when(pl.program_id(1) == 0)
    def _():
        wb_ref[...] = w_ref[...].astype(BF16)

    acc = jnp.dot(x_ref[...], wb_ref[...], preferred_element_type=F32)
    epilogue(acc, extra, outs)


def _matmul(x, w3, layer, n0, n, tm, tn, epilogue, extras=(), extra_specs=(), out_dtypes=(F32,), name="matmul"):
    m, k = x.shape
    j0 = n0 // tn
    in_specs = [pl.BlockSpec((tm, k), lambda j, i: (i, 0)),
                pl.BlockSpec((None, k, tn), lambda j, i: (layer, 0, j + j0))] + list(extra_specs)
    out_specs = [pl.BlockSpec((tm, tn), lambda j, i: (i, j)) for _ in out_dtypes]
    out_shape = [jax.ShapeDtypeStruct((m, n), dt) for dt in out_dtypes]
    res = pl.pallas_call(
        functools.partial(_mm_body, n_extra=len(extras), epilogue=epilogue),
        grid=(n // tn, m // tm),
        in_specs=in_specs,
        out_specs=out_specs,
        out_shape=out_shape,
        scratch_shapes=[pltpu.VMEM((k, tn), BF16)],
        compiler_params=_params(2),
        name=name,
    )(x, w3, *extras)
    return res[0] if len(res) == 1 else res


def _ep_plain(acc, extra, outs):
    outs[0][...] = acc.astype(outs[0].dtype)


def _ep_gelu(acc, extra, outs):
    outs[0][...] = jax.nn.gelu(acc).astype(outs[0].dtype)


def _ep_residual(acc, extra, outs):
    outs[0][...] = (extra[0][...] + acc).astype(outs[0].dtype)


def _ep_rope_ret(acc, extra, outs, *, scale):
    cos = extra[0][...]
    sin = extra[1][...]
    o = outs[0]
    half = RET_DK // 2
    for h in range(acc.shape[1] // RET_DK):
        x1 = acc[:, h * RET_DK:h * RET_DK + half]
        x2 = acc[:, h * RET_DK + half:(h + 1) * RET_DK]
        o[:, h * RET_DK:h * RET_DK + half] = ((x1 * cos - x2 * sin) * scale).astype(o.dtype)
        o[:, h * RET_DK + half:(h + 1) * RET_DK] = ((x2 * cos + x1 * sin) * scale).astype(o.dtype)


def _ep_rope_dil(acc, extra, outs, *, scale):
    cos2 = extra[0][...]
    sin2 = extra[1][...]
    o = outs[0]
    for h in range(acc.shape[1] // DIL_HD):
        x = acc[:, h * DIL_HD:(h + 1) * DIL_HD]
        swapped = pltpu.roll(x, DIL_HD // 2, axis=1)
        o[:, h * DIL_HD:(h + 1) * DIL_HD] = ((x * cos2 + swapped * sin2) * scale).astype(o.dtype)


def _ep_ple(acc, extra, outs):
    pe = jnp.dot(extra[1][...], extra[2][...].astype(BF16), preferred_element_type=F32)
    outs[0][...] = extra[0][...] + pe * jax.nn.sigmoid(acc)


def _rope_tables(pos, half):
    inv = ROPE_THETA ** (-jnp.arange(half, dtype=F32) / half)
    ang = pos.astype(F32)[:, None] * inv[None, :]
    return jnp.cos(ang), jnp.sin(ang)


def _ret_tables(c, c_pad):
    lg = jnp.log1p(-jnp.exp2(-5.0 - jnp.arange(RET_HEADS, dtype=F32)))
    i = jnp.arange(c_pad, dtype=F32)
    diff = i[:, None] - i[None, :]
    dmat = jnp.where(diff[None] >= 0, jnp.exp(lg[:, None, None] * jnp.maximum(diff, 0.0)[None]), 0.0)
    xi = jnp.exp(lg[:, None] * (i[None, :] + 1.0))[:, :, None]
    zeta = jnp.exp(lg[:, None] * (c - 1.0 - i[None, :]))[:, :, None]
    gc = jnp.broadcast_to(jnp.exp(lg * c)[:, None, None], (RET_HEADS, 1, RET_DV))
    return dmat, xi, zeta, gc


def _ret_finish(o, g):
    on = o * lax.rsqrt(jnp.mean(o * o, axis=-1, keepdims=True) + EPS)
    return on * _silu(g)


_NT = (((1,), (1,)), ((), ()))
_TN = (((0,), (0,)), ((), ()))


def _ret_prompt_body(q_ref, k_ref, v_ref, g_ref, dmat_ref, xi_ref, zeta_ref, gc_ref, o_ref, s_ref, *, c, cb):
    @pl.when(pl.program_id(2) == 0)
    def _():
        s_ref[...] = jnp.zeros_like(s_ref)

    dmat = dmat_ref[...]
    xi = xi_ref[...]
    zeta = zeta_ref[...]
    gc = gc_ref[...]
    for i in range(cb):
        rows = slice(i * c, (i + 1) * c)
        q = q_ref[rows, :]
        k = k_ref[rows, :]
        v = v_ref[rows, :]
        s = s_ref[...]
        sc = lax.dot_general(q, k, _NT, preferred_element_type=F32) * dmat
        o = jnp.dot(sc.astype(BF16), v, preferred_element_type=F32)
        o = o + jnp.dot(q, s.astype(BF16), preferred_element_type=F32) * xi
        kz = (k.astype(F32) * zeta).astype(BF16)
        s_ref[...] = s * gc + lax.dot_general(kz, v, _TN, preferred_element_type=F32)
        o_ref[rows, :] = _ret_finish(o, g_ref[rows, :].astype(F32)).astype(o_ref.dtype)


def _retention_prompt(q, k, vg, batch, seq):
    c = RET_CHUNK
    cb = 8
    rb = c * cb
    ncb = seq // rb
    dmat, xi, zeta, gc = _ret_tables(c, c)
    row = lambda b, h, i: (b * ncb + i, h)
    tab = lambda b, h, i: (h, 0, 0)
    return pl.pallas_call(
        functools.partial(_ret_prompt_body, c=c, cb=cb),
        grid=(batch, RET_HEADS, ncb),
        in_specs=[pl.BlockSpec((rb, RET_DK), row),
                  pl.BlockSpec((rb, RET_DK), row),
                  pl.BlockSpec((rb, RET_DV), row),
                  pl.BlockSpec((rb, RET_DV), lambda b, h, i: (b * ncb + i, RET_HEADS + h)),
                  pl.BlockSpec((None, c, c), tab),
                  pl.BlockSpec((None, c, 1), tab),
                  pl.BlockSpec((None, c, 1), tab),
                  pl.BlockSpec((None, 1, RET_DV), tab)],
        out_specs=[pl.BlockSpec((rb, RET_DV), row),
                   pl.BlockSpec((None, None, RET_DK, RET_DV), lambda b, h, i: (b, h, 0, 0))],
        out_shape=[jax.ShapeDtypeStruct((batch * seq, RET_V), BF16),
                   jax.ShapeDtypeStruct((batch, RET_HEADS, RET_DK, RET_DV), F32)],
        compiler_params=_params(3),
        name="retention_prompt",
    )(q, k, vg, vg, dmat, xi, zeta, gc)


def _ret_sample_body(q_ref, k_ref, v_ref, g_ref, s0_ref, dmat_ref, xi_ref, zeta_ref, gc_ref, o_ref, s_ref, *, nb):
    q = q_ref[...].astype(BF16)
    k = k_ref[...]
    v = v_ref[...].astype(BF16)
    xi = xi_ref[...]
    gc = gc_ref[...]
    sc = lax.dot_general(q, k.astype(BF16), _NT, preferred_element_type=F32) * dmat_ref[...]
    o = jnp.dot(sc.astype(BF16), v, preferred_element_type=F32)
    kz = (k * zeta_ref[...]).astype(BF16)
    row_batch = lax.broadcasted_iota(jnp.int32, kz.shape, 0) // SAMPLE_PAD
    cross = []
    for b in range(nb):
        s0 = s0_ref[b]
        qb = q[b * SAMPLE_PAD:(b + 1) * SAMPLE_PAD, :]
        cross.append(jnp.dot(qb, s0.astype(BF16), preferred_element_type=F32))
        kzb = jnp.where(row_batch == b, kz, jnp.zeros_like(kz))
        s_ref[b] = s0 * gc + lax.dot_general(kzb, v, _TN, preferred_element_type=F32)
    o = o + jnp.concatenate(cross, axis=0) * xi
    o_ref[...] = _ret_finish(o, g_ref[...]).astype(o_ref.dtype)


def _retention_sample(q, k, vg, state, layer, nb, t):
    rows = nb * SAMPLE_PAD
    dmat, xi, zeta, gc = _ret_tables(t, SAMPLE_PAD)
    eye = jnp.eye(nb, dtype=F32)
    dbig = jnp.einsum("ab,hij->haibj", eye, dmat).reshape(RET_HEADS, rows, rows)
    xib = jnp.tile(xi, (1, nb, 1))
    zetab = jnp.tile(zeta, (1, nb, 1))
    col = lambda h: (0, h)
    tab = lambda h: (h, 0, 0)
    return pl.pallas_call(
        functools.partial(_ret_sample_body, nb=nb),
        grid=(RET_HEADS,),
        in_specs=[pl.BlockSpec((rows, RET_DK), col),
                  pl.BlockSpec((rows, RET_DK), col),
                  pl.BlockSpec((rows, RET_DV), col),
                  pl.BlockSpec((rows, RET_DV), lambda h: (0, RET_HEADS + h)),
                  pl.BlockSpec((None, nb, None, RET_DK, RET_DV), lambda h: (layer, 0, h, 0, 0)),
                  pl.BlockSpec((None, rows, rows), tab),
                  pl.BlockSpec((None, rows, 1), tab),
                  pl.BlockSpec((None, rows, 1), tab),
                  pl.BlockSpec((None, 1, RET_DV), tab)],
        out_specs=[pl.BlockSpec((rows, RET_DV), col),
                   pl.BlockSpec((nb, None, RET_DK, RET_DV), lambda h: (0, h, 0, 0))],
        out_shape=[jax.ShapeDtypeStruct((rows, RET_V), BF16),
                   jax.ShapeDtypeStruct((nb, RET_HEADS, RET_DK, RET_DV), F32)],
        compiler_params=_params(1),
        name="retention_sample",
    )(q, k, vg, vg, state, dbig, xib, zetab, gc)


def _dil_prompt_body(q_ref, kc_ref, kp_ref, vc_ref, vp_ref, acc_ref, st_ref):
    has_prev = pl.program_id(2) > 0
    ii = lax.broadcasted_iota(jnp.int32, (DIL_SPAN, DIL_SPAN), 0)
    jj = lax.broadcasted_iota(jnp.int32, (DIL_SPAN, DIL_SPAN), 1)
    mask_c = jj <= ii
    mask_p = jnp.logical_and(jj >= ii, has_prev)
    st_ref[...] = jnp.zeros_like(st_ref)
    for h in range(DIL_HEADS):
        hs = slice(h * DIL_HD, (h + 1) * DIL_HD)
        q = q_ref[:, hs]
        s_c = jnp.where(mask_c, lax.dot_general(q, kc_ref[:, hs], _NT, preferred_element_type=F32), NEG)
        s_p = jnp.where(mask_p, lax.dot_general(q, kp_ref[:, hs], _NT, preferred_element_type=F32), NEG)
        m = jnp.maximum(jnp.max(s_c, axis=-1, keepdims=True), jnp.max(s_p, axis=-1, keepdims=True))
        p_c = jnp.exp(s_c - m)
        p_p = jnp.exp(s_p - m)
        den = jnp.sum(p_c, axis=-1, keepdims=True) + jnp.sum(p_p, axis=-1, keepdims=True)
        acc = jnp.dot(p_c.astype(BF16), vc_ref[:, hs], preferred_element_type=F32)
        acc = acc + jnp.dot(p_p.astype(BF16), vp_ref[:, hs], preferred_element_type=F32)
        acc_ref[:, hs] = acc
        st_ref[:, h:h + 1] = m
        st_ref[:, DIL_HEADS + h:DIL_HEADS + h + 1] = den


def _dilated_prompt(q3, k3, v3, g, batch, seq):
    d = DIL_RATES[g]
    rows = batch * seq
    nb = seq // (DIL_SPAN * d)
    cols = DIL_GROUPS * DIL_W
    qv, kv, vv = (a.reshape(rows // d, d * cols) for a in (q3, k3, v3))
    cur = lambda b, r, n: (b * nb + n, r * DIL_GROUPS + g)
    prev = lambda b, r, n: (b * nb + jnp.maximum(n - 1, 0), r * DIL_GROUPS + g)
    out = lambda b, r, n: (b * nb + n, r)
    blk = (DIL_SPAN, DIL_W)
    acc, st = pl.pallas_call(
        _dil_prompt_body,
        grid=(batch, d, nb),
        in_specs=[pl.BlockSpec(blk, cur), pl.BlockSpec(blk, cur), pl.BlockSpec(blk, prev),
                  pl.BlockSpec(blk, cur), pl.BlockSpec(blk, prev)],
        out_specs=[pl.BlockSpec(blk, out), pl.BlockSpec((DIL_SPAN, LANE), out)],
        out_shape=[jax.ShapeDtypeStruct((rows // d, d * DIL_W), F32),
                   jax.ShapeDtypeStruct((rows // d, d * LANE), F32)],
        compiler_params=_params(3),
        name=f"dilated_prompt_g{g}",
    )(qv, kv, kv, vv, vv)
    return acc.reshape(rows, DIL_W), st.reshape(rows, LANE)


def _dil_merge_body(a0_ref, a1_ref, a2_ref, s0_ref, s1_ref, s2_ref, gate_ref, o_ref):
    st = [s0_ref[...], s1_ref[...], s2_ref[...]]
    ms = [s[:, :DIL_HEADS] for s in st]
    dens = [s[:, DIL_HEADS:2 * DIL_HEADS] for s in st]
    m_all = jnp.maximum(jnp.maximum(ms[0], ms[1]), ms[2])
    es = [jnp.exp(m - m_all) for m in ms]
    tot = es[0] * dens[0] + es[1] * dens[1] + es[2] * dens[2]
    accs = (a0_ref, a1_ref, a2_ref)
    for h in range(DIL_HEADS):
        hs = slice(h * DIL_HD, (h + 1) * DIL_HD)
        num = es[0][:, h:h + 1] * accs[0][:, hs]
        num = num + es[1][:, h:h + 1] * accs[1][:, hs]
        num = num + es[2][:, h:h + 1] * accs[2][:, hs]
        merged = num / tot[:, h:h + 1]
        o_ref[:, hs] = (merged * _silu(gate_ref[:, hs].astype(F32))).astype(o_ref.dtype)


def _dilated_merge(accs, sts, gate, tr):
    rows = gate.shape[0]
    wide = pl.BlockSpec((tr, DIL_W), lambda i: (i, 0))
    narrow = pl.BlockSpec((tr, LANE), lambda i: (i, 0))
    return pl.pallas_call(
        _dil_merge_body,
        grid=(rows // tr,),
        in_specs=[wide, wide, wide, narrow, narrow, narrow, wide],
        out_specs=wide,
        out_shape=jax.ShapeDtypeStruct((rows, DIL_W), BF16),
        compiler_params=_params(1),
        name="dilated_merge",
    )(*accs, *sts, gate)


def _dil_sample_body(q_ref, kn_ref, vn_ref, gate_ref, kc0, vc0, kc1, vc1, kc2, vc2, o_ref):
    t = pl.program_id(1)

    @pl.when(t == 0)
    def _():
        o_ref[...] = jnp.zeros_like(o_ref)

    n_new = kn_ref.shape[0]
    n_keys = 2 * DIL_SPAN
    head_of_lane = lax.broadcasted_iota(jnp.int32, (DIL_HEADS, DIL_W), 1) // DIL_HD
    hmask = head_of_lane == lax.broadcasted_iota(jnp.int32, (DIL_HEADS, DIL_W), 0)
    jk = lax.broadcasted_iota(jnp.int32, (DIL_HEADS, n_keys), 1)
    pad = jnp.zeros((n_keys - DIL_SPAN - n_new, DIL_W), BF16)
    caches = ((kc0, vc0), (kc1, vc1), (kc2, vc2))
    ms, dens, accs = [], [], []
    for g in range(DIL_GROUPS):
        gs = slice(g * DIL_W, (g + 1) * DIL_W)
        qrow = q_ref[pl.ds(t, 1), gs]
        qm = jnp.where(hmask, jnp.broadcast_to(qrow, (DIL_HEADS, DIL_W)), 0.0).astype(BF16)
        k_all = jnp.concatenate([caches[g][0][...].astype(BF16), kn_ref[:, gs].astype(BF16), pad], axis=0)
        v_all = jnp.concatenate([caches[g][1][...].astype(BF16), vn_ref[:, gs].astype(BF16), pad], axis=0)
        s = lax.dot_general(qm, k_all, _NT, preferred_element_type=F32)
        new_idx = jk - DIL_SPAN
        if DIL_RATES[g] == 1:
            valid = jnp.logical_and(jk >= t, new_idx <= t)
        else:
            valid = jnp.logical_or(jk < DIL_SPAN, new_idx == t)
        s = jnp.where(valid, s, NEG)
        m = jnp.max(s, axis=-1, keepdims=True)
        p = jnp.exp(s - m)
        ms.append(m)
        dens.append(jnp.sum(p, axis=-1, keepdims=True))
        accs.append(jnp.dot(p.astype(BF16), v_all, preferred_element_type=F32))
    m_all = jnp.maximum(jnp.maximum(ms[0], ms[1]), ms[2])
    es = [jnp.exp(m - m_all) for m in ms]
    tot = es[0] * dens[0] + es[1] * dens[1] + es[2] * dens[2]
    merged = (es[0] * accs[0] + es[1] * accs[1] + es[2] * accs[2]) / tot
    row = jnp.sum(jnp.where(hmask, merged, 0.0), axis=0, keepdims=True)
    o_ref[pl.ds(t, 1), :] = row * _silu(gate_ref[pl.ds(t, 1), :])


def _dilated_sample(q3, k3, v3, gate, caches, layer, nb, t):
    kv = 2 * DIL_W
    views = []
    specs = []
    for g in range(DIL_GROUPS):
        d = DIL_RATES[g]
        wb = caches[g].shape[2]
        assert wb == DIL_SPAN * d and t <= d * (1 if d > 1 else DIL_SPAN)
        views.append(caches[g].reshape(caches[g].shape[0] * nb, DIL_SPAN, d * kv))
        if d == 1:
            kmap = lambda b, i: (layer * nb + b, 0, 0)
            vmap = lambda b, i: (layer * nb + b, 0, 1)
        else:
            kmap = lambda b, i: (layer * nb + b, 0, 2 * i)
            vmap = lambda b, i: (layer * nb + b, 0, 2 * i + 1)
        specs += [pl.BlockSpec((None, DIL_SPAN, DIL_W), kmap), pl.BlockSpec((None, DIL_SPAN, DIL_W), vmap)]
    rows3 = pl.BlockSpec((SAMPLE_PAD, DIL_GROUPS * DIL_W), lambda b, i: (b, 0))
    rows1 = pl.BlockSpec((SAMPLE_PAD, DIL_W), lambda b, i: (b, 0))
    operands = [q3, k3, v3, gate]
    for g in range(DIL_GROUPS):
        operands += [views[g], views[g]]
    return pl.pallas_call(
        _dil_sample_body,
        grid=(nb, t),
        in_specs=[rows3, rows3, rows3, rows1] + specs,
        out_specs=rows1,
        out_shape=jax.ShapeDtypeStruct((nb * SAMPLE_PAD, DIL_W), F32),
        compiler_params=_params(2),
        name="dilated_sample",
    )(*operands)


def _cache_shift_body(a_ref, nxt_ref, new_ref, o_ref, *, t, tb):
    last = pl.program_id(1) == pl.num_programs(1) - 1
    o_ref[0:tb - t] = a_ref[t:tb]

    @pl.when(last)
    def _():
        o_ref[tb - t:tb] = new_ref[...]

    @pl.when(jnp.logical_not(last))
    def _():
        o_ref[tb - t:tb] = nxt_ref[...]


def _cache_shift(cache, new, layer, nb, t):
    wb = cache.shape[2]
    tail = cache.shape[3:]
    tb = min(wb, 256)
    assert wb % tb == 0 and tb % t == 0
    zeros = (0,) * len(tail)
    return pl.pallas_call(
        functools.partial(_cache_shift_body, t=t, tb=tb),
        grid=(nb, wb // tb),
        in_specs=[pl.BlockSpec((None, None, tb) + tail, lambda b, i: (layer, b, i) + zeros),
                  pl.BlockSpec((None, None, t) + tail,
                               lambda b, i: (layer, b, jnp.minimum((i + 1) * (tb // t), wb // t - 1)) + zeros),
                  pl.BlockSpec((None, t) + tail, lambda b, i: (b, 0) + zeros)],
        out_specs=pl.BlockSpec((None, tb) + tail, lambda b, i: (b, i) + zeros),
        out_shape=jax.ShapeDtypeStruct(cache.shape[1:], cache.dtype),
        compiler_params=_params(2),
        name="cache_shift",
    )(cache, cache, new)


def _gmlp_body(u_ref, v_ref, gate_ref, lng_ref, lnb_ref, wm_ref, bs_ref, o_ref, *vn_ref):
    v = v_ref[...].astype(F32)
    mu = jnp.mean(v, axis=-1, keepdims=True)
    xc = v - mu
    vn = xc * lax.rsqrt(jnp.mean(xc * xc, axis=-1, keepdims=True) + EPS) * lng_ref[...] + lnb_ref[...]
    if vn_ref:
        vn_ref[0][...] = vn
    vb = vn.astype(BF16)
    for g in range(GM_GROUPS):
        gs = slice(g * GM_GD, (g + 1) * GM_GD)
        mixed = jnp.dot(wm_ref[g], vb[:, gs], preferred_element_type=F32) + bs_ref[g]
        o_ref[:, gs] = (u_ref[:, gs].astype(F32) * mixed * _silu(gate_ref[:, gs].astype(F32))).astype(o_ref.dtype)


def _gmlp_core(uv, gate, ln_g3, ln_b3, layer, wm, bs, want_vn):
    rows = gate.shape[0]
    c = wm.shape[1]
    row = lambda i: (i, 0)
    out_specs = [pl.BlockSpec((c, GM_WIDTH), row)]
    out_shape = [jax.ShapeDtypeStruct((rows, GM_WIDTH), BF16)]
    if want_vn:
        out_specs.append(pl.BlockSpec((c, GM_WIDTH), row))
        out_shape.append(jax.ShapeDtypeStruct((rows, GM_WIDTH), F32))
    res = pl.pallas_call(
        _gmlp_body,
        grid=(rows // c,),
        in_specs=[pl.BlockSpec((c, GM_WIDTH), row),
                  pl.BlockSpec((c, GM_WIDTH), lambda i: (i, 1)),
                  pl.BlockSpec((c, GM_WIDTH), row),
                  pl.BlockSpec((None, 1, GM_WIDTH), lambda i: (layer, 0, 0)),
                  pl.BlockSpec((None, 1, GM_WIDTH), lambda i: (layer, 0, 0)),
                  pl.BlockSpec((GM_GROUPS, c, c), lambda i: (0, 0, 0)),
                  pl.BlockSpec((GM_GROUPS, c, 1), lambda i: (0, 0, 0))],
        out_specs=out_specs,
        out_shape=out_shape,
        compiler_params=_params(1),
        name="gmlp_core",
    )(uv, uv, gate, ln_g3, ln_b3, wm, bs)
    return res


class _Stream:
    def __init__(self, batch, t, t_pad, pos0, tm, act_dtype):
        self.batch, self.t, self.t_pad, self.tm, self.act = batch, t, t_pad, tm, act_dtype
        self.rows = batch * t_pad
        pos = pos0 + jnp.arange(t_pad, dtype=jnp.int32)
        cos, sin = _rope_tables(pos, RET_DK // 2)
        self.ret_rope = (cos, sin)
        cos, sin = _rope_tables(pos, DIL_HD // 2)
        self.dil_rope = (jnp.concatenate([cos, cos], axis=-1), jnp.concatenate([-sin, sin], axis=-1))
        self.rope_tiles = max(t_pad // tm, 1)
        if t_pad < tm:
            rep = tm // t_pad
            self.ret_rope = tuple(jnp.tile(a, (rep, 1)) for a in self.ret_rope)
            self.dil_rope = tuple(jnp.tile(a, (rep, 1)) for a in self.dil_rope)

    def rope_specs(self):
        nt = self.rope_tiles
        return [pl.BlockSpec((self.tm, LANE), lambda j, i: (i % nt, 0))] * 2


def _proj(st, h, w3, layer, n0, n, tn, epilogue, out_dtype, extras=(), extra_specs=(), name="proj"):
    return _matmul(h, w3, layer, n0, n, st.tm, tn, epilogue, extras, extra_specs, (out_dtype,), name)


def _out_and_ple(st, y_in, w_out3, jl, x, i, p3, ple_w, ple_g3, ple_w_gate):
    tm = st.tm
    tn = 512
    res_spec = pl.BlockSpec((tm, tn), lambda j, m: (m, j))
    x1 = _matmul(y_in, w_out3, jl, 0, D_MODEL, tm, tn, _ep_residual, (x,), (res_spec,), (F32,), "out_proj")
    hn = _rmsnorm(x1, ple_g3, i, BF16, min(256, st.rows))
    tn = 512
    extras = (x1, p3, ple_w)
    specs = (pl.BlockSpec((tm, tn), lambda j, m: (m, j)),
             pl.BlockSpec((None, tm, PLE_DIM), lambda j, m: (i, m, 0)),
             pl.BlockSpec((None, PLE_DIM, tn), lambda j, m: (i, 0, j)))
    return _matmul(hn, ple_w_gate, i, 0, D_MODEL, tm, tn, _ep_ple, extras, specs, (F32,), "ple")


def _retention_layer(st, h, w_in, jl, state):
    tn = 1024
    q = _proj(st, h, w_in, jl, 0, RET_QK, tn, functools.partial(_ep_rope_ret, scale=1.0), st.act,
              st.ret_rope, st.rope_specs(), "ret_q")
    k = _proj(st, h, w_in, jl, RET_QK, RET_QK, tn, functools.partial(_ep_rope_ret, scale=RET_DK ** -0.5), st.act,
              st.ret_rope, st.rope_specs(), "ret_k")
    vg = _proj(st, h, w_in, jl, 2 * RET_QK, 2 * RET_V, tn, _ep_plain, st.act, name="ret_vg")
    if state is None:
        return _retention_prompt(q, k, vg, st.batch, st.t)
    return _retention_sample(q, k, vg, state, jl, st.batch, st.t)


def _dilated_proj(st, h, w_in, jl):
    tn = 1024
    per_group = 3 * DIL_W

    def grouped(kind, epilogue, extras=(), extra_specs=(), name=""):
        tiles = DIL_W // tn
        m, k = h.shape
        in_specs = [pl.BlockSpec((st.tm, k), lambda j, i: (i, 0)),
                    pl.BlockSpec((None, k, tn),
                                 lambda j, i: (jl, 0, (j // tiles) * (per_group // tn) + kind * tiles + j % tiles))]
        return pl.pallas_call(
            functools.partial(_mm_body, n_extra=len(extras), epilogue=epilogue),
            grid=(DIL_GROUPS * tiles, m // st.tm),
            in_specs=in_specs + list(extra_specs),
            out_specs=[pl.BlockSpec((st.tm, tn), lambda j, i: (i, j))],
            out_shape=[jax.ShapeDtypeStruct((m, DIL_GROUPS * DIL_W), st.act)],
            scratch_shapes=[pltpu.VMEM((k, tn), BF16)],
            compiler_params=_params(2),
            name=name,
        )(h, w_in, *extras)[0]

    q3 = grouped(0, functools.partial(_ep_rope_dil, scale=DIL_HD ** -0.5), st.dil_rope, st.rope_specs(), "dil_q")
    k3 = grouped(1, functools.partial(_ep_rope_dil, scale=1.0), st.dil_rope, st.rope_specs(), "dil_k")
    v3 = grouped(2, _ep_plain, name="dil_v")
    gate = _proj(st, h, w_in, jl, DIL_GROUPS * per_group, DIL_W, tn, _ep_plain, st.act, name="dil_gate")
    return q3, k3, v3, gate


def _window_rows(k3, v3, st, g, n_rows):
    gs = slice(g * DIL_W, (g + 1) * DIL_W)
    k = k3[:, gs].reshape(st.batch, st.t_pad, DIL_HEADS, DIL_HD)[:, st.t - n_rows:st.t]
    v = v3[:, gs].reshape(st.batch, st.t_pad, DIL_HEADS, DIL_HD)[:, st.t - n_rows:st.t]
    return jnp.stack([k, v], axis=2).astype(F32)


def _gmlp_layer(st, h, w_in, jl, ln_g3, ln_b3, wm, bs, want_vn):
    tn = 1024
    uv = _proj(st, h, w_in, jl, 0, 2 * GM_WIDTH, tn, _ep_gelu, st.act, name="gm_uv")
    gate = _proj(st, h, w_in, jl, 2 * GM_WIDTH, GM_WIDTH, tn, _ep_plain, st.act, name="gm_gate")
    return _gmlp_core(uv, gate, ln_g3, ln_b3, jl, wm, bs, want_vn)


def kernel(x_prompt, x_sample, state_ret, cache_win_g0, cache_win_g1, cache_win_g2, p_prompt, p_sample, norm_g,
           ret_w_in, ret_w_out, dil_w_in, dil_w_out, gm_w_in, gm_ln_g, gm_ln_b, gm_w_s, gm_b_s, gm_w_out, ple_w,
           ple_norm_g, ple_w_gate, final_norm_g):
    depth = norm_g.shape[0]
    bp, sp, _ = x_prompt.shape
    bs_, ts, _ = x_sample.shape
    assert ts <= SAMPLE_PAD and ts % GM_CHUNK != 0 and sp % (DIL_SPAN * max(DIL_RATES)) == 0
    caches = (cache_win_g0, cache_win_g1, cache_win_g2)

    prm = _Stream(bp, sp, sp, 0, 512, BF16)
    smp = _Stream(bs_, ts, SAMPLE_PAD, PAST_LEN, bs_ * SAMPLE_PAD, F32)

    pad_t = ((0, 0), (0, SAMPLE_PAD - ts), (0, 0))
    xp = x_prompt.reshape(prm.rows, D_MODEL)
    xs = jnp.pad(x_sample, pad_t).reshape(smp.rows, D_MODEL)
    pp = p_prompt.astype(BF16).reshape(depth, prm.rows, PLE_DIM)
    ps = jnp.pad(p_sample, ((0, 0),) + pad_t).astype(BF16).reshape(depth, smp.rows, PLE_DIM)

    norm_g3 = norm_g[:, None, :]
    ple_g3 = ple_norm_g[:, None, :]
    ln_g3 = gm_ln_g[:, None, :]
    ln_b3 = gm_ln_b[:, None, :]
    fin_g3 = final_norm_g[None, None, :]

    ret_p, ret_s, gm_s = [], [], []
    win_p = [[], [], []]
    win_s = [[], [], []]
    for i in range(depth):
        kind, jl = i % 3, i // 3
        hp = _rmsnorm(xp, norm_g3, i, BF16, 256)
        hs = _rmsnorm(xs, norm_g3, i, BF16, smp.rows)
        if kind == 0:
            yp, sp_new = _retention_layer(prm, hp, ret_w_in, jl, None)
            ys, ss_new = _retention_layer(smp, hs, ret_w_in, jl, state_ret)
            ret_p.append(sp_new)
            ret_s.append(ss_new)
            w_out = ret_w_out
        elif kind == 1:
            q3, k3, v3, gate = _dilated_proj(prm, hp, dil_w_in, jl)
            parts = [_dilated_prompt(q3, k3, v3, g, bp, sp) for g in range(DIL_GROUPS)]
            yp = _dilated_merge([a for a, _ in parts], [s for _, s in parts], gate, 256)
            for g in range(DIL_GROUPS):
                win_p[g].append(_window_rows(k3, v3, prm, g, min(DIL_SPAN * DIL_RATES[g], sp)))
            q3s, k3s, v3s, gate_s = _dilated_proj(smp, hs, dil_w_in, jl)
            ys = _dilated_sample(q3s, k3s, v3s, gate_s, caches, jl, bs_, ts).astype(BF16)
            for g in range(DIL_GROUPS):
                win_s[g].append(_cache_shift(caches[g], _window_rows(k3s, v3s, smp, g, ts), jl, bs_, ts))
            w_out = dil_w_out
        else:
            c = GM_CHUNK
            wm_p = jnp.tril(gm_w_s[jl]).astype(BF16)
            bs_p = gm_b_s[jl][:, :, None]
            (yp,) = _gmlp_layer(prm, hp, gm_w_in, jl, ln_g3, ln_b3, wm_p, bs_p, False)
            wm_t = jnp.pad(jnp.tril(gm_w_s[jl][:, :ts, :ts]), ((0, 0), (0, SAMPLE_PAD - ts), (0, SAMPLE_PAD - ts)))
            wm_s = jnp.einsum("ab,gij->gaibj", jnp.eye(bs_, dtype=F32), wm_t).reshape(GM_GROUPS, smp.rows, smp.rows)
            bs_s = jnp.tile(jnp.pad(gm_b_s[jl][:, :ts], ((0, 0), (0, SAMPLE_PAD - ts))), (1, bs_))[:, :, None]
            ys, vn = _gmlp_layer(smp, hs, gm_w_in, jl, ln_g3, ln_b3, wm_s.astype(BF16), bs_s, True)
            gm_s.append(vn.reshape(bs_, SAMPLE_PAD, GM_WIDTH)[:, :ts])
            w_out = gm_w_out
        xp = _out_and_ple(prm, yp, w_out, jl, xp, i, pp, ple_w, ple_g3, ple_w_gate)
        xs = _out_and_ple(smp, ys.astype(BF16), w_out, jl, xs, i, ps, ple_w, ple_g3, ple_w_gate)

    y_prompt = _rmsnorm(xp, fin_g3, 0, F32, 256).reshape(bp, sp, D_MODEL)
    y_sample = _rmsnorm(xs, fin_g3, 0, F32, smp.rows).reshape(bs_, SAMPLE_PAD, D_MODEL)[:, :ts]
    return (y_prompt, y_sample,
            jnp.stack(ret_p), jnp.stack(ret_s),
            jnp.stack(win_p[0]), jnp.stack(win_s[0]),
            jnp.stack(win_p[1]), jnp.stack(win_s[1]),
            jnp.stack(win_p[2]), jnp.stack(win_s[2]),
            jnp.stack(gm_s))
```

```python
import functools

import jax
import jax.numpy as jnp
from jax import lax
from jax.experimental import pallas as pl
from jax.experimental.pallas import tpu as pltpu

F32 = jnp.float32
BF16 = jnp.bfloat16

D_MODEL = 2048
PAST_LEN = 16384
PLE_DIM = 256
ROPE_THETA = 10000.0
EPS = 1e-6
NEG = -1e30

RET_HEADS = 8
RET_DK = 256
RET_DV = 512
RET_QK = RET_HEADS * RET_DK
RET_V = RET_HEADS * RET_DV
RET_CHUNK = 128

DIL_RATES = (1, 4, 16)
DIL_GROUPS = 3
DIL_SPAN = 128
DIL_HEADS = 16
DIL_HD = 128
DIL_W = DIL_HEADS * DIL_HD

GM_WIDTH = 2 * D_MODEL
GM_GROUPS = 16
GM_GD = GM_WIDTH // GM_GROUPS
GM_CHUNK = 128

LANE = 128
SAMPLE_PAD = 16
VMEM_LIMIT = 56 * 1024 * 1024


def _params(n_axes, vmem=VMEM_LIMIT):
    return pltpu.CompilerParams(dimension_semantics=("arbitrary",) * n_axes, vmem_limit_bytes=vmem)


def _silu(x):
    return x * jax.nn.sigmoid(x)


def _rms_body(x_ref, g_ref, o_ref):
    x = x_ref[...]
    y = x * lax.rsqrt(jnp.mean(x * x, axis=-1, keepdims=True) + EPS)
    o_ref[...] = (y * g_ref[...]).astype(o_ref.dtype)


def _rmsnorm(x, g3, layer, out_dtype, tr):
    r, d = x.shape
    return pl.pallas_call(
        _rms_body,
        grid=(r // tr,),
        in_specs=[pl.BlockSpec((tr, d), lambda i: (i, 0)),
                  pl.BlockSpec((None, 1, d), lambda i: (layer, 0, 0))],
        out_specs=pl.BlockSpec((tr, d), lambda i: (i, 0)),
        out_shape=jax.ShapeDtypeStruct((r, d), out_dtype),
        compiler_params=_params(1),
        name="rmsnorm",
    )(x, g3)


def _rms_orders_body(x_ref, g_ref, perm_ref, o_ref, *perm_out_refs, rates):
    x = x_ref[...]
    y = (x * lax.rsqrt(jnp.mean(x * x, axis=-1, keepdims=True) + EPS) * g_ref[...]).astype(o_ref.dtype)
    o_ref[...] = y
    tr = x.shape[0]
    for k, (p_ref, d) in enumerate(zip(perm_out_refs, rates)):
        yp = jnp.dot(perm_ref[k], y, preferred_element_type=F32).astype(o_ref.dtype)
        for r in range(d):
            p_ref[r] = yp[r * (tr // d):(r + 1) * (tr // d), :]


def _rmsnorm_orders(x, g3, layer, batch, seq, rates, tr):
    r, dm = x.shape
    nt = seq // tr
    rows = jnp.arange(tr, dtype=jnp.int32)
    perms = jnp.stack([(rows[None, :] == (rows % (tr // d) * d + rows // (tr // d))[:, None]) for d in rates])
    out_specs = [pl.BlockSpec((tr, dm), lambda i: (i, 0))]
    out_shape = [jax.ShapeDtypeStruct((r, dm), BF16)]
    for d in rates:
        out_specs.append(pl.BlockSpec((None, d, tr // d, dm), lambda i: (i // nt, 0, i % nt, 0)))
        out_shape.append(jax.ShapeDtypeStruct((batch, d, seq // d, dm), BF16))
    res = pl.pallas_call(
        functools.partial(_rms_orders_body, rates=rates),
        grid=(r // tr,),
        in_specs=[pl.BlockSpec((tr, dm), lambda i: (i, 0)),
                  pl.BlockSpec((None, 1, dm), lambda i: (layer, 0, 0)),
                  pl.BlockSpec((len(rates), tr, tr), lambda i: (0, 0, 0))],
        out_specs=out_specs,
        out_shape=out_shape,
        compiler_params=_params(1),
        name="rmsnorm_orders",
    )(x, g3, perms.astype(BF16))
    return [res[0]] + [a.reshape(r, dm) for a in res[1:]]


def _mm_body(x_ref, w_ref, *refs, n_extra, epilogue):
    extra = refs[:n_extra]
    outs = refs[n_extra:-1]
    wb_ref = refs[-1]

    @pl.when(pl.program_id(1) == 0)
    def _():
        wb_ref[...] = w_ref[...].astype(BF16)

    acc = jnp.dot(x_ref[...], wb_ref[...], preferred_element_type=F32)
    epilogue(acc, extra, outs)


def _matmul(x, w3, layer, n0, n, tm, tn, epilogue, extras=(), extra_specs=(), out_dtypes=(F32,), name="matmul"):
    m, k = x.shape
    j0 = n0 // tn
    in_specs = [pl.BlockSpec((tm, k), lambda j, i: (i, 0)),
                pl.BlockSpec((None, k, tn), lambda j, i: (layer, 0, j + j0))] + list(extra_specs)
    out_specs = [pl.BlockSpec((tm, tn), lambda j, i: (i, j)) for _ in out_dtypes]
    out_shape = [jax.ShapeDtypeStruct((m, n), dt) for dt in out_dtypes]
    res = pl.pallas_call(
        functools.partial(_mm_body, n_extra=len(extras), epilogue=epilogue),
        grid=(n // tn, m // tm),
        in_specs=in_specs,
        out_specs=out_specs,
        out_shape=out_shape,
        scratch_shapes=[pltpu.VMEM((k, tn), BF16)],
        compiler_params=_params(2),
        name=name,
    )(x, w3, *extras)
    return res[0] if len(res) == 1 else res


def _ep_plain(acc, extra, outs):
    outs[0][...] = acc.astype(outs[0].dtype)


def _ep_gelu(acc, extra, outs):
    outs[0][...] = jax.nn.gelu(acc).astype(outs[0].dtype)


def _ep_residual(acc, extra, outs):
    outs[0][...] = (extra[0][...] + acc).astype(outs[0].dtype)


def _ep_rope_ret(acc, extra, outs, *, scale):
    cos = extra[0][...]
    sin = extra[1][...]
    o = outs[0]
    half = RET_DK // 2
    for h in range(acc.shape[1] // RET_DK):
        x1 = acc[:, h * RET_DK:h * RET_DK + half]
        x2 = acc[:, h * RET_DK + half:(h + 1) * RET_DK]
        o[:, h * RET_DK:h * RET_DK + half] = ((x1 * cos - x2 * sin) * scale).astype(o.dtype)
        o[:, h * RET_DK + half:(h + 1) * RET_DK] = ((x2 * cos + x1 * sin) * scale).astype(o.dtype)


def _ep_rope_dil(acc, extra, outs, *, scale):
    cos2 = extra[0][...]
    sin2 = extra[1][...]
    o = outs[0]
    for h in range(acc.shape[1] // DIL_HD):
        x = acc[:, h * DIL_HD:(h + 1) * DIL_HD]
        swapped = pltpu.roll(x, DIL_HD // 2, axis=1)
        o[:, h * DIL_HD:(h + 1) * DIL_HD] = ((x * cos2 + swapped * sin2) * scale).astype(o.dtype)


def _ep_dil_qkv(acc, extra, outs, *, tiles):
    j = pl.program_id(0)

    @pl.when(j < 2 * tiles)
    def _():
        _ep_rope_dil(acc, extra, outs, scale=jnp.where(j < tiles, DIL_HD ** -0.5, 1.0).astype(F32))

    @pl.when(j >= 2 * tiles)
    def _():
        _ep_plain(acc, extra, outs)


def _ep_ple(acc, extra, outs):
    pe = jnp.dot(extra[1][...], extra[2][...].astype(BF16), preferred_element_type=F32)
    outs[0][...] = extra[0][...] + pe * jax.nn.sigmoid(acc)


def _rope_tables(pos, half):
    inv = ROPE_THETA ** (-jnp.arange(half, dtype=F32) / half)
    ang = pos.astype(F32)[:, None] * inv[None, :]
    return jnp.cos(ang), jnp.sin(ang)


def _ret_tables(c, c_pad):
    lg = jnp.log1p(-jnp.exp2(-5.0 - jnp.arange(RET_HEADS, dtype=F32)))
    i = jnp.arange(c_pad, dtype=F32)
    diff = i[:, None] - i[None, :]
    dmat = jnp.where(diff[None] >= 0, jnp.exp(lg[:, None, None] * jnp.maximum(diff, 0.0)[None]), 0.0)
    xi = jnp.exp(lg[:, None] * (i[None, :] + 1.0))[:, :, None]
    zeta = jnp.exp(lg[:, None] * (c - 1.0 - i[None, :]))[:, :, None]
    gc = jnp.broadcast_to(jnp.exp(lg * c)[:, None, None], (RET_HEADS, 1, RET_DV))
    return dmat, xi, zeta, gc


def _ret_finish(o, g):
    on = o * lax.rsqrt(jnp.mean(o * o, axis=-1, keepdims=True) + EPS)
    return on * _silu(g)


_NT = (((1,), (1,)), ((), ()))
_TN = (((0,), (0,)), ((), ()))


def _ret_prompt_body(q_ref, k_ref, v_ref, g_ref, dmat_ref, xi_ref, zeta_ref, gc_ref, o_ref, s_ref, *, c, cb):
    @pl.when(pl.program_id(2) == 0)
    def _():
        s_ref[...] = jnp.zeros_like(s_ref)

    dmat = dmat_ref[...]
    xi = xi_ref[...]
    zeta = zeta_ref[...]
    gc = gc_ref[...]
    for i in range(cb):
        rows = slice(i * c, (i + 1) * c)
        q = q_ref[rows, :]
        k = k_ref[rows, :]
        v = v_ref[rows, :]
        s = s_ref[...]
        sc = lax.dot_general(q, k, _NT, preferred_element_type=F32) * dmat
        o = jnp.dot(sc.astype(BF16), v, preferred_element_type=F32)
        o = o + jnp.dot(q, s.astype(BF16), preferred_element_type=F32) * xi
        kz = (k.astype(F32) * zeta).astype(BF16)
        s_ref[...] = s * gc + lax.dot_general(kz, v, _TN, preferred_element_type=F32)
        o_ref[rows, :] = _ret_finish(o, g_ref[rows, :].astype(F32)).astype(o_ref.dtype)


def _retention_prompt(q, k, vg, batch, seq):
    c = RET_CHUNK
    cb = 8
    rb = c * cb
    ncb = seq // rb
    dmat, xi, zeta, gc = _ret_tables(c, c)
    row = lambda b, h, i: (b * ncb + i, h)
    tab = lambda b, h, i: (h, 0, 0)
    return pl.pallas_call(
        functools.partial(_ret_prompt_body, c=c, cb=cb),
        grid=(batch, RET_HEADS, ncb),
        in_specs=[pl.BlockSpec((rb, RET_DK), row),
                  pl.BlockSpec((rb, RET_DK), row),
                  pl.BlockSpec((rb, RET_DV), row),
                  pl.BlockSpec((rb, RET_DV), lambda b, h, i: (b * ncb + i, RET_HEADS + h)),
                  pl.BlockSpec((None, c, c), tab),
                  pl.BlockSpec((None, c, 1), tab),
                  pl.BlockSpec((None, c, 1), tab),
                  pl.BlockSpec((None, 1, RET_DV), tab)],
        out_specs=[pl.BlockSpec((rb, RET_DV), row),
                   pl.BlockSpec((None, None, RET_DK, RET_DV), lambda b, h, i: (b, h, 0, 0))],
        out_shape=[jax.ShapeDtypeStruct((batch * seq, RET_V), BF16),
                   jax.ShapeDtypeStruct((batch, RET_HEADS, RET_DK, RET_DV), F32)],
        compiler_params=_params(3),
        name="retention_prompt",
    )(q, k, vg, vg, dmat, xi, zeta, gc)


def _ret_sample_body(q_ref, k_ref, v_ref, g_ref, s0_ref, dmat_ref, xi_ref, zeta_ref, gc_ref, o_ref, s_ref, *, nb):
    q = q_ref[...].astype(BF16)
    k = k_ref[...]
    v = v_ref[...].astype(BF16)
    xi = xi_ref[...]
    gc = gc_ref[...]
    sc = lax.dot_general(q, k.astype(BF16), _NT, preferred_element_type=F32) * dmat_ref[...]
    o = jnp.dot(sc.astype(BF16), v, preferred_element_type=F32)
    kz = (k * zeta_ref[...]).astype(BF16)
    row_batch = lax.broadcasted_iota(jnp.int32, kz.shape, 0) // SAMPLE_PAD
    cross = []
    for b in range(nb):
        s0 = s0_ref[b]
        qb = q[b * SAMPLE_PAD:(b + 1) * SAMPLE_PAD, :]
        cross.append(jnp.dot(qb, s0.astype(BF16), preferred_element_type=F32))
        kzb = jnp.where(row_batch == b, kz, jnp.zeros_like(kz))
        s_ref[b] = s0 * gc + lax.dot_general(kzb, v, _TN, preferred_element_type=F32)
    o = o + jnp.concatenate(cross, axis=0) * xi
    o_ref[...] = _ret_finish(o, g_ref[...]).astype(o_ref.dtype)


def _retention_sample(q, k, vg, state, layer, nb, t):
    rows = nb * SAMPLE_PAD
    dmat, xi, zeta, gc = _ret_tables(t, SAMPLE_PAD)
    eye = jnp.eye(nb, dtype=F32)
    dbig = jnp.einsum("ab,hij->haibj", eye, dmat).reshape(RET_HEADS, rows, rows)
    xib = jnp.tile(xi, (1, nb, 1))
    zetab = jnp.tile(zeta, (1, nb, 1))
    col = lambda h: (0, h)
    tab = lambda h: (h, 0, 0)
    return pl.pallas_call(
        functools.partial(_ret_sample_body, nb=nb),
        grid=(RET_HEADS,),
        in_specs=[pl.BlockSpec((rows, RET_DK), col),
                  pl.BlockSpec((rows, RET_DK), col),
                  pl.BlockSpec((rows, RET_DV), col),
                  pl.BlockSpec((rows, RET_DV), lambda h: (0, RET_HEADS + h)),
                  pl.BlockSpec((None, nb, None, RET_DK, RET_DV), lambda h: (layer, 0, h, 0, 0)),
                  pl.BlockSpec((None, rows, rows), tab),
                  pl.BlockSpec((None, rows, 1), tab),
                  pl.BlockSpec((None, rows, 1), tab),
                  pl.BlockSpec((None, 1, RET_DV), tab)],
        out_specs=[pl.BlockSpec((rows, RET_DV), col),
                   pl.BlockSpec((nb, None, RET_DK, RET_DV), lambda h: (0, h, 0, 0))],
        out_shape=[jax.ShapeDtypeStruct((rows, RET_V), BF16),
                   jax.ShapeDtypeStruct((nb, RET_HEADS, RET_DK, RET_DV), F32)],
        compiler_params=_params(1),
        name="retention_sample",
    )(q, k, vg, vg, state, dbig, xib, zetab, gc)


DIL_PROMPT_SPAN = DIL_SPAN * max(DIL_RATES)
DIL_PROMPT_HEADS = 2


def _dil_prompt_body(*refs, span, hpb):
    ng = DIL_GROUPS
    q_refs, kc_refs, kp_refs = refs[0:ng], refs[ng:2 * ng], refs[2 * ng:3 * ng]
    vc_refs, vp_refs = refs[3 * ng:4 * ng], refs[4 * ng:5 * ng]
    gate_ref, o_ref, acc_s, m_s, l_s = refs[5 * ng:]
    has_prev = pl.program_id(1) > 0
    ii = lax.broadcasted_iota(jnp.int32, (DIL_SPAN, DIL_SPAN), 0)
    jj = lax.broadcasted_iota(jnp.int32, (DIL_SPAN, DIL_SPAN), 1)
    mask_c = jj <= ii
    mask_p = jj >= ii
    mask_p0 = jnp.logical_and(mask_p, has_prev)
    for hl in range(hpb):
        hs = slice(hl * DIL_HD, (hl + 1) * DIL_HD)
        for g in range(ng):
            d = DIL_RATES[g]
            for r in range(d):
                for n in range(span // (DIL_SPAN * d)):
                    rows = slice(n * DIL_SPAN, (n + 1) * DIL_SPAN)
                    q = q_refs[g][r, rows, hs]
                    if n > 0:
                        prows = slice((n - 1) * DIL_SPAN, n * DIL_SPAN)
                        kp, vp, mp = kc_refs[g][r, prows, hs], vc_refs[g][r, prows, hs], mask_p
                    else:
                        kp, vp, mp = kp_refs[g][r, :, hs], vp_refs[g][r, :, hs], mask_p0
                    s_c = jnp.where(mask_c, lax.dot_general(q, kc_refs[g][r, rows, hs], _NT,
                                                            preferred_element_type=F32), NEG)
                    s_p = jnp.where(mp, lax.dot_general(q, kp, _NT, preferred_element_type=F32), NEG)
                    m = jnp.max(jnp.maximum(s_c, s_p), axis=-1, keepdims=True)
                    p_c = jnp.exp(s_c - m)
                    p_p = jnp.exp(s_p - m)
                    den = jnp.sum(p_c + p_p, axis=-1, keepdims=True)
                    acc = jnp.dot(p_c.astype(BF16), vc_refs[g][r, rows, hs], preferred_element_type=F32)
                    acc = acc + jnp.dot(p_p.astype(BF16), vp, preferred_element_type=F32)
                    dst = rows if d == 1 else pl.ds(n * DIL_SPAN * d + r, DIL_SPAN, stride=d)
                    acc_s[g, dst, :] = acc
                    m_s[g, dst, :] = jnp.broadcast_to(m, (DIL_SPAN, LANE))
                    l_s[g, dst, :] = jnp.broadcast_to(den, (DIL_SPAN, LANE))
        ms = [m_s[g] for g in range(ng)]
        m_all = jnp.maximum(jnp.maximum(ms[0], ms[1]), ms[2])
        es = [jnp.exp(m - m_all) for m in ms]
        tot = es[0] * l_s[0] + es[1] * l_s[1] + es[2] * l_s[2]
        num = es[0] * acc_s[0] + es[1] * acc_s[1] + es[2] * acc_s[2]
        o_ref[:, hs] = (num / tot * _silu(gate_ref[:, hs].astype(F32))).astype(o_ref.dtype)


def _dilated_prompt(qkvs, gate, batch, seq):
    span, hpb = DIL_PROMPT_SPAN, DIL_PROMPT_HEADS
    wc = hpb * DIL_HD
    kcol, vcol = DIL_W // wc, 2 * DIL_W // wc
    nsp = seq // span
    views = [a.reshape(batch, d, seq // d, 3 * DIL_W) for a, d in zip(qkvs, DIL_RATES)]
    q_specs, kc_specs, kp_specs, vc_specs, vp_specs = [], [], [], [], []
    for d in DIL_RATES:
        cur = (None, d, span // d, wc)
        prv = (None, d, DIL_SPAN, wc)
        nblk = span // (d * DIL_SPAN)
        q_specs.append(pl.BlockSpec(cur, lambda b, s, h: (b, 0, s, h)))
        kc_specs.append(pl.BlockSpec(cur, lambda b, s, h: (b, 0, s, kcol + h)))
        vc_specs.append(pl.BlockSpec(cur, lambda b, s, h: (b, 0, s, vcol + h)))
        kp_specs.append(pl.BlockSpec(prv, lambda b, s, h, nblk=nblk: (b, 0, jnp.maximum(s * nblk - 1, 0), kcol + h)))
        vp_specs.append(pl.BlockSpec(prv, lambda b, s, h, nblk=nblk: (b, 0, jnp.maximum(s * nblk - 1, 0), vcol + h)))
    rows_spec = pl.BlockSpec((span, wc), lambda b, s, h: (b * nsp + s, h))
    return pl.pallas_call(
        functools.partial(_dil_prompt_body, span=span, hpb=hpb),
        grid=(batch, nsp, DIL_W // wc),
        in_specs=q_specs + kc_specs + kp_specs + vc_specs + vp_specs + [rows_spec],
        out_specs=rows_spec,
        out_shape=jax.ShapeDtypeStruct((batch * seq, DIL_W), BF16),
        scratch_shapes=[pltpu.VMEM((DIL_GROUPS, span, LANE), F32)] * 3,
        compiler_params=_params(3),
        name="dilated_prompt",
    )(*(views * 5), gate)


def _dil_sample_body(q_ref, kn_ref, vn_ref, gate_ref, kc0, vc0, kc1, vc1, kc2, vc2, o_ref):
    t = pl.program_id(1)

    @pl.when(t == 0)
    def _():
        o_ref[...] = jnp.zeros_like(o_ref)

    n_new = kn_ref.shape[0]
    n_keys = 2 * DIL_SPAN
    head_of_lane = lax.broadcasted_iota(jnp.int32, (DIL_HEADS, DIL_W), 1) // DIL_HD
    hmask = head_of_lane == lax.broadcasted_iota(jnp.int32, (DIL_HEADS, DIL_W), 0)
    jk = lax.broadcasted_iota(jnp.int32, (DIL_HEADS, n_keys), 1)
    pad = jnp.zeros((n_keys - DIL_SPAN - n_new, DIL_W), BF16)
    caches = ((kc0, vc0), (kc1, vc1), (kc2, vc2))
    ms, dens, accs = [], [], []
    for g in range(DIL_GROUPS):
        gs = slice(g * DIL_W, (g + 1) * DIL_W)
        qrow = q_ref[pl.ds(t, 1), gs]
        qm = jnp.where(hmask, jnp.broadcast_to(qrow, (DIL_HEADS, DIL_W)), 0.0).astype(BF16)
        k_all = jnp.concatenate([caches[g][0][...].astype(BF16), kn_ref[:, gs].astype(BF16), pad], axis=0)
        v_all = jnp.concatenate([caches[g][1][...].astype(BF16), vn_ref[:, gs].astype(BF16), pad], axis=0)
        s = lax.dot_general(qm, k_all, _NT, preferred_element_type=F32)
        new_idx = jk - DIL_SPAN
        if DIL_RATES[g] == 1:
            valid = jnp.logical_and(jk >= t, new_idx <= t)
        else:
            valid = jnp.logical_or(jk < DIL_SPAN, new_idx == t)
        s = jnp.where(valid, s, NEG)
        m = jnp.max(s, axis=-1, keepdims=True)
        p = jnp.exp(s - m)
        ms.append(m)
        dens.append(jnp.sum(p, axis=-1, keepdims=True))
        accs.append(jnp.dot(p.astype(BF16), v_all, preferred_element_type=F32))
    m_all = jnp.maximum(jnp.maximum(ms[0], ms[1]), ms[2])
    es = [jnp.exp(m - m_all) for m in ms]
    tot = es[0] * dens[0] + es[1] * dens[1] + es[2] * dens[2]
    merged = (es[0] * accs[0] + es[1] * accs[1] + es[2] * accs[2]) / tot
    row = jnp.sum(jnp.where(hmask, merged, 0.0), axis=0, keepdims=True)
    o_ref[pl.ds(t, 1), :] = row * _silu(gate_ref[pl.ds(t, 1), :])


def _dilated_sample(q3, k3, v3, gate, caches, layer, nb, t):
    kv = 2 * DIL_W
    views = []
    specs = []
    for g in range(DIL_GROUPS):
        d = DIL_RATES[g]
        wb = caches[g].shape[2]
        assert wb == DIL_SPAN * d and t <= d * (1 if d > 1 else DIL_SPAN)
        views.append(caches[g].reshape(caches[g].shape[0] * nb, DIL_SPAN, d * kv))
        if d == 1:
            kmap = lambda b, i: (layer * nb + b, 0, 0)
            vmap = lambda b, i: (layer * nb + b, 0, 1)
        else:
            kmap = lambda b, i: (layer * nb + b, 0, 2 * i)
            vmap = lambda b, i: (layer * nb + b, 0, 2 * i + 1)
        specs += [pl.BlockSpec((None, DIL_SPAN, DIL_W), kmap), pl.BlockSpec((None, DIL_SPAN, DIL_W), vmap)]
    rows3 = pl.BlockSpec((SAMPLE_PAD, DIL_GROUPS * DIL_W), lambda b, i: (b, 0))
    rows1 = pl.BlockSpec((SAMPLE_PAD, DIL_W), lambda b, i: (b, 0))
    operands = [q3, k3, v3, gate]
    for g in range(DIL_GROUPS):
        operands += [views[g], views[g]]
    return pl.pallas_call(
        _dil_sample_body,
        grid=(nb, t),
        in_specs=[rows3, rows3, rows3, rows1] + specs,
        out_specs=rows1,
        out_shape=jax.ShapeDtypeStruct((nb * SAMPLE_PAD, DIL_W), F32),
        compiler_params=_params(2),
        name="dilated_sample",
    )(*operands)


def _cache_shift_body(a_ref, nxt_ref, new_ref, o_ref, *, t, tb):
    last = pl.program_id(1) == pl.num_programs(1) - 1
    o_ref[0:tb - t] = a_ref[t:tb]

    @pl.when(last)
    def _():
        o_ref[tb - t:tb] = new_ref[...]

    @pl.when(jnp.logical_not(last))
    def _():
        o_ref[tb - t:tb] = nxt_ref[...]


def _cache_shift(cache, new, layer, nb, t):
    wb = cache.shape[2]
    tail = cache.shape[3:]
    tb = min(wb, 256)
    assert wb % tb == 0 and tb % t == 0
    zeros = (0,) * len(tail)
    return pl.pallas_call(
        functools.partial(_cache_shift_body, t=t, tb=tb),
        grid=(nb, wb // tb),
        in_specs=[pl.BlockSpec((None, None, tb) + tail, lambda b, i: (layer, b, i) + zeros),
                  pl.BlockSpec((None, None, t) + tail,
                               lambda b, i: (layer, b, jnp.minimum((i + 1) * (tb // t), wb // t - 1)) + zeros),
                  pl.BlockSpec((None, t) + tail, lambda b, i: (b, 0) + zeros)],
        out_specs=pl.BlockSpec((None, tb) + tail, lambda b, i: (b, i) + zeros),
        out_shape=jax.ShapeDtypeStruct(cache.shape[1:], cache.dtype),
        compiler_params=_params(2),
        name="cache_shift",
    )(cache, cache, new)


def _gmlp_body(u_ref, v_ref, gate_ref, lng_ref, lnb_ref, wm_ref, bs_ref, o_ref, *vn_ref):
    v = v_ref[...].astype(F32)
    mu = jnp.mean(v, axis=-1, keepdims=True)
    xc = v - mu
    vn = xc * lax.rsqrt(jnp.mean(xc * xc, axis=-1, keepdims=True) + EPS) * lng_ref[...] + lnb_ref[...]
    if vn_ref:
        vn_ref[0][...] = vn
    vb = vn.astype(BF16)
    for g in range(GM_GROUPS):
        gs = slice(g * GM_GD, (g + 1) * GM_GD)
        mixed = jnp.dot(wm_ref[g], vb[:, gs], preferred_element_type=F32) + bs_ref[g]
        o_ref[:, gs] = (u_ref[:, gs].astype(F32) * mixed * _silu(gate_ref[:, gs].astype(F32))).astype(o_ref.dtype)


def _gmlp_core(uv, gate, ln_g3, ln_b3, layer, wm, bs, want_vn):
    rows = gate.shape[0]
    c = wm.shape[1]
    row = lambda i: (i, 0)
    out_specs = [pl.BlockSpec((c, GM_WIDTH), row)]
    out_shape = [jax.ShapeDtypeStruct((rows, GM_WIDTH), BF16)]
    if want_vn:
        out_specs.append(pl.BlockSpec((c, GM_WIDTH), row))
        out_shape.append(jax.ShapeDtypeStruct((rows, GM_WIDTH), F32))
    res = pl.pallas_call(
        _gmlp_body,
        grid=(rows // c,),
        in_specs=[pl.BlockSpec((c, GM_WIDTH), row),
                  pl.BlockSpec((c, GM_WIDTH), lambda i: (i, 1)),
                  pl.BlockSpec((c, GM_WIDTH), row),
                  pl.BlockSpec((None, 1, GM_WIDTH), lambda i: (layer, 0, 0)),
                  pl.BlockSpec((None, 1, GM_WIDTH), lambda i: (layer, 0, 0)),
                  pl.BlockSpec((GM_GROUPS, c, c), lambda i: (0, 0, 0)),
                  pl.BlockSpec((GM_GROUPS, c, 1), lambda i: (0, 0, 0))],
        out_specs=out_specs,
        out_shape=out_shape,
        compiler_params=_params(1),
        name="gmlp_core",
    )(uv, uv, gate, ln_g3, ln_b3, wm, bs)
    return res


class _Stream:
    def __init__(self, batch, t, t_pad, pos0, tm, act_dtype):
        self.batch, self.t, self.t_pad, self.tm, self.act = batch, t, t_pad, tm, act_dtype
        self.rows = batch * t_pad
        pos = pos0 + jnp.arange(t_pad, dtype=jnp.int32)
        cos, sin = _rope_tables(pos, RET_DK // 2)
        self.ret_rope = (cos, sin)
        cos, sin = _rope_tables(pos, DIL_HD // 2)
        self.dil_rope = (jnp.concatenate([cos, cos], axis=-1), jnp.concatenate([-sin, sin], axis=-1))
        self.rope_tiles = max(t_pad // tm, 1)
        if t_pad < tm:
            rep = tm // t_pad
            self.ret_rope = tuple(jnp.tile(a, (rep, 1)) for a in self.ret_rope)
            self.dil_rope = tuple(jnp.tile(a, (rep, 1)) for a in self.dil_rope)

    def rope_specs(self):
        nt = self.rope_tiles
        return [pl.BlockSpec((self.tm, LANE), lambda j, i: (i % nt, 0))] * 2

    def dil_rope_residue_major(self, d):
        t = self.t_pad
        return tuple(a.reshape(t // d, d, LANE).transpose(1, 0, 2).reshape(t, LANE) for a in self.dil_rope)


def _proj(st, h, w3, layer, n0, n, tn, epilogue, out_dtype, extras=(), extra_specs=(), name="proj"):
    return _matmul(h, w3, layer, n0, n, st.tm, tn, epilogue, extras, extra_specs, (out_dtype,), name)


def _out_and_ple(st, y_in, w_out3, jl, x, i, p3, ple_w, ple_g3, ple_w_gate):
    tm = st.tm
    tn = 512
    res_spec = pl.BlockSpec((tm, tn), lambda j, m: (m, j))
    x1 = _matmul(y_in, w_out3, jl, 0, D_MODEL, tm, tn, _ep_residual, (x,), (res_spec,), (F32,), "out_proj")
    hn = _rmsnorm(x1, ple_g3, i, BF16, min(256, st.rows))
    tn = 512
    extras = (x1, p3, ple_w)
    specs = (pl.BlockSpec((tm, tn), lambda j, m: (m, j)),
             pl.BlockSpec((None, tm, PLE_DIM), lambda j, m: (i, m, 0)),
             pl.BlockSpec((None, PLE_DIM, tn), lambda j, m: (i, 0, j)))
    return _matmul(hn, ple_w_gate, i, 0, D_MODEL, tm, tn, _ep_ple, extras, specs, (F32,), "ple")


def _retention_layer(st, h, w_in, jl, state):
    tn = 1024
    q = _proj(st, h, w_in, jl, 0, RET_QK, tn, functools.partial(_ep_rope_ret, scale=1.0), st.act,
              st.ret_rope, st.rope_specs(), "ret_q")
    k = _proj(st, h, w_in, jl, RET_QK, RET_QK, tn, functools.partial(_ep_rope_ret, scale=RET_DK ** -0.5), st.act,
              st.ret_rope, st.rope_specs(), "ret_k")
    vg = _proj(st, h, w_in, jl, 2 * RET_QK, 2 * RET_V, tn, _ep_plain, st.act, name="ret_vg")
    if state is None:
        return _retention_prompt(q, k, vg, st.batch, st.t)
    return _retention_sample(q, k, vg, state, jl, st.batch, st.t)


def _dilated_proj(st, h, w_in, jl):
    tn = 1024
    per_group = 3 * DIL_W

    def grouped(kind, epilogue, extras=(), extra_specs=(), name=""):
        tiles = DIL_W // tn
        m, k = h.shape
        in_specs = [pl.BlockSpec((st.tm, k), lambda j, i: (i, 0)),
                    pl.BlockSpec((None, k, tn),
                                 lambda j, i: (jl, 0, (j // tiles) * (per_group // tn) + kind * tiles + j % tiles))]
        return pl.pallas_call(
            functools.partial(_mm_body, n_extra=len(extras), epilogue=epilogue),
            grid=(DIL_GROUPS * tiles, m // st.tm),
            in_specs=in_specs + list(extra_specs),
            out_specs=[pl.BlockSpec((st.tm, tn), lambda j, i: (i, j))],
            out_shape=[jax.ShapeDtypeStruct((m, DIL_GROUPS * DIL_W), st.act)],
            scratch_shapes=[pltpu.VMEM((k, tn), BF16)],
            compiler_params=_params(2),
            name=name,
        )(h, w_in, *extras)[0]

    q3 = grouped(0, functools.partial(_ep_rope_dil, scale=DIL_HD ** -0.5), st.dil_rope, st.rope_specs(), "dil_q")
    k3 = grouped(1, functools.partial(_ep_rope_dil, scale=1.0), st.dil_rope, st.rope_specs(), "dil_k")
    v3 = grouped(2, _ep_plain, name="dil_v")
    gate = _proj(st, h, w_in, jl, DIL_GROUPS * per_group, DIL_W, tn, _ep_plain, st.act, name="dil_gate")
    return q3, k3, v3, gate


def _dilated_proj_prompt(st, hs_by_rate, w_in, jl):
    tn = 1024
    tiles = DIL_W // tn
    qkvs = []
    for g, d in enumerate(DIL_RATES):
        rope = st.dil_rope if d == 1 else st.dil_rope_residue_major(d)
        qkvs.append(_proj(st, hs_by_rate[g], w_in, jl, g * 3 * DIL_W, 3 * DIL_W, tn,
                          functools.partial(_ep_dil_qkv, tiles=tiles), st.act, rope, st.rope_specs(), f"dil_qkv_g{g}"))
    gate = _proj(st, hs_by_rate[0], w_in, jl, DIL_GROUPS * 3 * DIL_W, DIL_W, tn, _ep_plain, st.act, name="dil_gate")
    return qkvs, gate


def _window_rows_prompt(qkv, st, d):
    kv = qkv.reshape(st.batch, d, st.t // d, 3, DIL_HEADS, DIL_HD)[:, :, st.t // d - DIL_SPAN:, 1:]
    kv = kv.transpose(0, 2, 1, 3, 4, 5).reshape(st.batch, DIL_SPAN * d, 2, DIL_HEADS, DIL_HD)
    return kv.astype(F32)


def _window_rows(k3, v3, st, g, n_rows):
    gs = slice(g * DIL_W, (g + 1) * DIL_W)
    k = k3[:, gs].reshape(st.batch, st.t_pad, DIL_HEADS, DIL_HD)[:, st.t - n_rows:st.t]
    v = v3[:, gs].reshape(st.batch, st.t_pad, DIL_HEADS, DIL_HD)[:, st.t - n_rows:st.t]
    return jnp.stack([k, v], axis=2).astype(F32)


def _gmlp_layer(st, h, w_in, jl, ln_g3, ln_b3, wm, bs, want_vn):
    tn = 1024
    uv = _proj(st, h, w_in, jl, 0, 2 * GM_WIDTH, tn, _ep_gelu, st.act, name="gm_uv")
    gate = _proj(st, h, w_in, jl, 2 * GM_WIDTH, GM_WIDTH, tn, _ep_plain, st.act, name="gm_gate")
    return _gmlp_core(uv, gate, ln_g3, ln_b3, jl, wm, bs, want_vn)


def kernel(x_prompt, x_sample, state_ret, cache_win_g0, cache_win_g1, cache_win_g2, p_prompt, p_sample, norm_g,
           ret_w_in, ret_w_out, dil_w_in, dil_w_out, gm_w_in, gm_ln_g, gm_ln_b, gm_w_s, gm_b_s, gm_w_out, ple_w,
           ple_norm_g, ple_w_gate, final_norm_g):
    depth = norm_g.shape[0]
    bp, sp, _ = x_prompt.shape
    bs_, ts, _ = x_sample.shape
    assert ts <= SAMPLE_PAD and ts % GM_CHUNK != 0 and sp % (DIL_SPAN * max(DIL_RATES)) == 0
    caches = (cache_win_g0, cache_win_g1, cache_win_g2)

    prm = _Stream(bp, sp, sp, 0, 512, BF16)
    smp = _Stream(bs_, ts, SAMPLE_PAD, PAST_LEN, bs_ * SAMPLE_PAD, F32)

    pad_t = ((0, 0), (0, SAMPLE_PAD - ts), (0, 0))
    xp = x_prompt.reshape(prm.rows, D_MODEL)
    xs = jnp.pad(x_sample, pad_t).reshape(smp.rows, D_MODEL)
    pp = p_prompt.astype(BF16).reshape(depth, prm.rows, PLE_DIM)
    ps = jnp.pad(p_sample, ((0, 0),) + pad_t).astype(BF16).reshape(depth, smp.rows, PLE_DIM)

    norm_g3 = norm_g[:, None, :]
    ple_g3 = ple_norm_g[:, None, :]
    ln_g3 = gm_ln_g[:, None, :]
    ln_b3 = gm_ln_b[:, None, :]
    fin_g3 = final_norm_g[None, None, :]

    ret_p, ret_s, gm_s = [], [], []
    win_p = [[], [], []]
    win_s = [[], [], []]
    for i in range(depth):
        kind, jl = i % 3, i // 3
        if kind == 1:
            hp_orders = _rmsnorm_orders(xp, norm_g3, i, bp, sp, DIL_RATES[1:], 256)
        else:
            hp = _rmsnorm(xp, norm_g3, i, BF16, 256)
        hs = _rmsnorm(xs, norm_g3, i, BF16, smp.rows)
        if kind == 0:
            yp, sp_new = _retention_layer(prm, hp, ret_w_in, jl, None)
            ys, ss_new = _retention_layer(smp, hs, ret_w_in, jl, state_ret)
            ret_p.append(sp_new)
            ret_s.append(ss_new)
            w_out = ret_w_out
        elif kind == 1:
            qkvs, gate = _dilated_proj_prompt(prm, hp_orders, dil_w_in, jl)
            yp = _dilated_prompt(qkvs, gate, bp, sp)
            for g in range(DIL_GROUPS):
                win_p[g].append(_window_rows_prompt(qkvs[g], prm, DIL_RATES[g]))
            q3s, k3s, v3s, gate_s = _dilated_proj(smp, hs, dil_w_in, jl)
            ys = _dilated_sample(q3s, k3s, v3s, gate_s, caches, jl, bs_, ts).astype(BF16)
            for g in range(DIL_GROUPS):
                win_s[g].append(_cache_shift(caches[g], _window_rows(k3s, v3s, smp, g, ts), jl, bs_, ts))
            w_out = dil_w_out
        else:
            c = GM_CHUNK
            wm_p = jnp.tril(gm_w_s[jl]).astype(BF16)
            bs_p = gm_b_s[jl][:, :, None]
            (yp,) = _gmlp_layer(prm, hp, gm_w_in, jl, ln_g3, ln_b3, wm_p, bs_p, False)
            wm_t = jnp.pad(jnp.tril(gm_w_s[jl][:, :ts, :ts]), ((0, 0), (0, SAMPLE_PAD - ts), (0, SAMPLE_PAD - ts)))
            wm_s = jnp.einsum("ab,gij->gaibj", jnp.eye(bs_, dtype=F32), wm_t).reshape(GM_GROUPS, smp.rows, smp.rows)
            bs_s = jnp.tile(jnp.pad(gm_b_s[jl][:, :ts], ((0, 0), (0, SAMPLE_PAD - ts))), (1, bs_))[:, :, None]
            ys, vn = _gmlp_layer(smp, hs, gm_w_in, jl, ln_g3, ln_b3, wm_s.astype(BF16), bs_s, True)
            gm_s.append(vn.reshape(bs_, SAMPLE_PAD, GM_WIDTH)[:, :ts])
            w_out = gm_w_out
        xp = _out_and_ple(prm, yp, w_out, jl, xp, i, pp, ple_w, ple_g3, ple_w_gate)
        xs = _out_and_ple(smp, ys.astype(BF16), w_out, jl, xs, i, ps, ple_w, ple_g3, ple_w_gate)

    y_prompt = _rmsnorm(xp, fin_g3, 0, F32, 256).reshape(bp, sp, D_MODEL)
    y_sample = _rmsnorm(xs, fin_g3, 0, F32, smp.rows).reshape(bs_, SAMPLE_PAD, D_MODEL)[:, :ts]
    return (y_prompt, y_sample,
            jnp.stack(ret_p), jnp.stack(ret_s),
            jnp.stack(win_p[0]), jnp.stack(win_s[0]),
            jnp.stack(win_p[1]), jnp.stack(win_s[1]),
            jnp.stack(win_p[2]), jnp.stack(win_s[2]),
            jnp.stack(gm_s))
```

```python
import functools

import jax
import jax.numpy as jnp
from jax import lax
from jax.experimental import pallas as pl
from jax.experimental.pallas import tpu as pltpu

F32 = jnp.float32
BF16 = jnp.bfloat16

D_MODEL = 2048
PAST_LEN = 16384
PLE_DIM = 256
ROPE_THETA = 10000.0
EPS = 1e-6
NEG = -1e30

RET_HEADS = 8
RET_DK = 256
RET_DV = 512
RET_QK = RET_HEADS * RET_DK
RET_V = RET_HEADS * RET_DV
RET_CHUNK = 128

DIL_RATES = (1, 4, 16)
DIL_GROUPS = 3
DIL_SPAN = 128
DIL_HEADS = 16
DIL_HD = 128
DIL_W = DIL_HEADS * DIL_HD

GM_WIDTH = 2 * D_MODEL
GM_GROUPS = 16
GM_GD = GM_WIDTH // GM_GROUPS
GM_CHUNK = 128

LANE = 128
MXU_COLS = 256
SAMPLE_PAD = 16
VMEM_LIMIT = 56 * 1024 * 1024


def _params(n_axes, vmem=VMEM_LIMIT):
    return pltpu.CompilerParams(dimension_semantics=("arbitrary",) * n_axes, vmem_limit_bytes=vmem)


def _silu(x):
    return x * jax.nn.sigmoid(x)


def _rms_body(x_ref, g_ref, o_ref):
    x = x_ref[...]
    y = x * lax.rsqrt(jnp.mean(x * x, axis=-1, keepdims=True) + EPS)
    o_ref[...] = (y * g_ref[...]).astype(o_ref.dtype)


def _rmsnorm(x, g3, layer, out_dtype, tr):
    r, d = x.shape
    return pl.pallas_call(
        _rms_body,
        grid=(r // tr,),
        in_specs=[pl.BlockSpec((tr, d), lambda i: (i, 0)),
                  pl.BlockSpec((None, 1, d), lambda i: (layer, 0, 0))],
        out_specs=pl.BlockSpec((tr, d), lambda i: (i, 0)),
        out_shape=jax.ShapeDtypeStruct((r, d), out_dtype),
        compiler_params=_params(1),
        name="rmsnorm",
    )(x, g3)


def _rms_orders_body(x_ref, g_ref, perm_ref, o_ref, *perm_out_refs, rates):
    x = x_ref[...]
    y = (x * lax.rsqrt(jnp.mean(x * x, axis=-1, keepdims=True) + EPS) * g_ref[...]).astype(o_ref.dtype)
    o_ref[...] = y
    tr = x.shape[0]
    for k, (p_ref, d) in enumerate(zip(perm_out_refs, rates)):
        yp = jnp.dot(perm_ref[k], y, preferred_element_type=F32).astype(o_ref.dtype)
        for r in range(d):
            p_ref[r] = yp[r * (tr // d):(r + 1) * (tr // d), :]


def _rmsnorm_orders(x, g3, layer, batch, seq, rates, tr):
    r, dm = x.shape
    nt = seq // tr
    rows = jnp.arange(tr, dtype=jnp.int32)
    perms = jnp.stack([(rows[None, :] == (rows % (tr // d) * d + rows // (tr // d))[:, None]) for d in rates])
    out_specs = [pl.BlockSpec((tr, dm), lambda i: (i, 0))]
    out_shape = [jax.ShapeDtypeStruct((r, dm), BF16)]
    for d in rates:
        out_specs.append(pl.BlockSpec((None, d, tr // d, dm), lambda i: (i // nt, 0, i % nt, 0)))
        out_shape.append(jax.ShapeDtypeStruct((batch, d, seq // d, dm), BF16))
    res = pl.pallas_call(
        functools.partial(_rms_orders_body, rates=rates),
        grid=(r // tr,),
        in_specs=[pl.BlockSpec((tr, dm), lambda i: (i, 0)),
                  pl.BlockSpec((None, 1, dm), lambda i: (layer, 0, 0)),
                  pl.BlockSpec((len(rates), tr, tr), lambda i: (0, 0, 0))],
        out_specs=out_specs,
        out_shape=out_shape,
        compiler_params=_params(1),
        name="rmsnorm_orders",
    )(x, g3, perms.astype(BF16))
    return [res[0]] + [a.reshape(r, dm) for a in res[1:]]


def _mm_body(x_ref, w_ref, *refs, n_extra, epilogue):
    extra = refs[:n_extra]
    outs = refs[n_extra:-1]
    wb_ref = refs[-1]

    @pl.when(pl.program_id(1) == 0)
    def _():
        wb_ref[...] = w_ref[...].astype(BF16)

    def run(ep):
        for c in range(wb_ref.shape[1] // MXU_COLS):
            cs = slice(c * MXU_COLS, (c + 1) * MXU_COLS)
            acc = jnp.dot(x_ref[...], wb_ref[:, cs], preferred_element_type=F32)
            ep(acc, cs, extra, outs)

    if callable(epilogue):
        run(epilogue)
    else:
        j = pl.program_id(0)
        starts = [s for s, _ in epilogue] + [None]
        for (lo, ep), hi in zip(epilogue, starts[1:]):
            pred = j >= lo if hi is None else jnp.logical_and(j >= lo, j < hi)
            pl.when(pred)(functools.partial(run, ep))


def _matmul(x, w3, layer, n0, n, tm, tn, epilogue, extras=(), extra_specs=(), out_dtypes=(F32,), name="matmul"):
    m, k = x.shape
    j0 = n0 // tn
    in_specs = [pl.BlockSpec((tm, k), lambda j, i: (i, 0)),
                pl.BlockSpec((None, k, tn), lambda j, i: (layer, 0, j + j0))] + list(extra_specs)
    out_specs = [pl.BlockSpec((tm, tn), lambda j, i: (i, j)) for _ in out_dtypes]
    out_shape = [jax.ShapeDtypeStruct((m, n), dt) for dt in out_dtypes]
    res = pl.pallas_call(
        functools.partial(_mm_body, n_extra=len(extras), epilogue=epilogue),
        grid=(n // tn, m // tm),
        in_specs=in_specs,
        out_specs=out_specs,
        out_shape=out_shape,
        scratch_shapes=[pltpu.VMEM((k, tn), BF16)],
        compiler_params=_params(2),
        name=name,
    )(x, w3, *extras)
    return res[0] if len(res) == 1 else res


def _ep_plain(acc, cs, extra, outs):
    outs[0][:, cs] = acc.astype(outs[0].dtype)


def _ep_gelu(acc, cs, extra, outs):
    outs[0][:, cs] = jax.nn.gelu(acc).astype(outs[0].dtype)


def _ep_residual(acc, cs, extra, outs):
    outs[0][:, cs] = (extra[0][:, cs] + acc).astype(outs[0].dtype)


def _ep_rope_ret(acc, cs, extra, outs, *, scale):
    cos = extra[0][...]
    sin = extra[1][...]
    o = outs[0]
    half = RET_DK // 2
    for h in range(acc.shape[1] // RET_DK):
        x1 = acc[:, h * RET_DK:h * RET_DK + half]
        x2 = acc[:, h * RET_DK + half:(h + 1) * RET_DK]
        c0 = cs.start + h * RET_DK
        y1 = x1 * cos - x2 * sin
        y2 = x2 * cos + x1 * sin
        o[:, c0:c0 + half] = (y1 if scale == 1.0 else y1 * scale).astype(o.dtype)
        o[:, c0 + half:c0 + RET_DK] = (y2 if scale == 1.0 else y2 * scale).astype(o.dtype)


def _ep_rope_dil(acc, cs, extra, outs, *, scale):
    cos2 = extra[0][...]
    sin2 = extra[1][...]
    o = outs[0]
    for h in range(acc.shape[1] // DIL_HD):
        x = acc[:, h * DIL_HD:(h + 1) * DIL_HD]
        swapped = pltpu.roll(x, DIL_HD // 2, axis=1)
        c0 = cs.start + h * DIL_HD
        y = x * cos2 + swapped * sin2
        o[:, c0:c0 + DIL_HD] = (y if scale == 1.0 else y * scale).astype(o.dtype)


def _ep_ple(acc, cs, extra, outs):
    pe = jnp.dot(extra[1][...], extra[2][:, cs].astype(BF16), preferred_element_type=F32)
    outs[0][:, cs] = extra[0][:, cs] + pe * jax.nn.sigmoid(acc)


def _rope_tables(pos, half):
    inv = ROPE_THETA ** (-jnp.arange(half, dtype=F32) / half)
    ang = pos.astype(F32)[:, None] * inv[None, :]
    return jnp.cos(ang), jnp.sin(ang)


def _ret_tables(c, c_pad):
    lg = jnp.log1p(-jnp.exp2(-5.0 - jnp.arange(RET_HEADS, dtype=F32)))
    i = jnp.arange(c_pad, dtype=F32)
    diff = i[:, None] - i[None, :]
    dmat = jnp.where(diff[None] >= 0, jnp.exp(lg[:, None, None] * jnp.maximum(diff, 0.0)[None]), 0.0)
    xi = jnp.exp(lg[:, None] * (i[None, :] + 1.0))[:, :, None]
    zeta = jnp.exp(lg[:, None] * (c - 1.0 - i[None, :]))[:, :, None]
    gc = jnp.broadcast_to(jnp.exp(lg * c)[:, None, None], (RET_HEADS, 1, RET_DV))
    return dmat, xi, zeta, gc


def _ret_finish(o, g):
    on = o * lax.rsqrt(jnp.mean(o * o, axis=-1, keepdims=True) + EPS)
    return on * _silu(g)


_NT = (((1,), (1,)), ((), ()))
_TN = (((0,), (0,)), ((), ()))


def _ret_prompt_body(q_ref, k_ref, v_ref, g_ref, dmat_ref, xi_ref, zeta_ref, gc_ref, o_ref, s_ref, *, c, cb):
    @pl.when(pl.program_id(2) == 0)
    def _():
        s_ref[...] = jnp.zeros_like(s_ref)

    dmat = dmat_ref[...]
    xi = xi_ref[...]
    zeta = zeta_ref[...]
    gc = gc_ref[...]
    for i in range(cb):
        rows = slice(i * c, (i + 1) * c)
        q = q_ref[rows, :]
        k = k_ref[rows, :]
        v = v_ref[rows, :]
        s = s_ref[...]
        sc = lax.dot_general(q, k, _NT, preferred_element_type=F32) * dmat
        o = jnp.dot(sc.astype(BF16), v, preferred_element_type=F32)
        o = o + jnp.dot(q, s.astype(BF16), preferred_element_type=F32) * xi
        kz = (k.astype(F32) * zeta).astype(BF16)
        s_ref[...] = s * gc + lax.dot_general(kz, v, _TN, preferred_element_type=F32)
        o_ref[rows, :] = _ret_finish(o, g_ref[rows, :].astype(F32)).astype(o_ref.dtype)


def _retention_prompt(q, k, vg, batch, seq):
    c = RET_CHUNK
    cb = 8
    rb = c * cb
    ncb = seq // rb
    dmat, xi, zeta, gc = _ret_tables(c, c)
    row = lambda b, h, i: (b * ncb + i, h)
    tab = lambda b, h, i: (h, 0, 0)
    return pl.pallas_call(
        functools.partial(_ret_prompt_body, c=c, cb=cb),
        grid=(batch, RET_HEADS, ncb),
        in_specs=[pl.BlockSpec((rb, RET_DK), row),
                  pl.BlockSpec((rb, RET_DK), row),
                  pl.BlockSpec((rb, RET_DV), row),
                  pl.BlockSpec((rb, RET_DV), lambda b, h, i: (b * ncb + i, RET_HEADS + h)),
                  pl.BlockSpec((None, c, c), tab),
                  pl.BlockSpec((None, c, 1), tab),
                  pl.BlockSpec((None, c, 1), tab),
                  pl.BlockSpec((None, 1, RET_DV), tab)],
        out_specs=[pl.BlockSpec((rb, RET_DV), row),
                   pl.BlockSpec((None, None, RET_DK, RET_DV), lambda b, h, i: (b, h, 0, 0))],
        out_shape=[jax.ShapeDtypeStruct((batch * seq, RET_V), BF16),
                   jax.ShapeDtypeStruct((batch, RET_HEADS, RET_DK, RET_DV), F32)],
        compiler_params=_params(3),
        name="retention_prompt",
    )(q, k, vg, vg, dmat, xi, zeta, gc)


def _ret_sample_body(q_ref, k_ref, v_ref, g_ref, s0_ref, dmat_ref, xi_ref, zeta_ref, gc_ref, o_ref, s_ref, *, nb):
    q = q_ref[...].astype(BF16)
    k = k_ref[...]
    v = v_ref[...].astype(BF16)
    xi = xi_ref[...]
    gc = gc_ref[...]
    sc = lax.dot_general(q, k.astype(BF16), _NT, preferred_element_type=F32) * dmat_ref[...]
    o = jnp.dot(sc.astype(BF16), v, preferred_element_type=F32)
    kz = (k * zeta_ref[...]).astype(BF16)
    row_batch = lax.broadcasted_iota(jnp.int32, kz.shape, 0) // SAMPLE_PAD
    cross = []
    for b in range(nb):
        s0 = s0_ref[b]
        qb = q[b * SAMPLE_PAD:(b + 1) * SAMPLE_PAD, :]
        cross.append(jnp.dot(qb, s0.astype(BF16), preferred_element_type=F32))
        kzb = jnp.where(row_batch == b, kz, jnp.zeros_like(kz))
        s_ref[b] = s0 * gc + lax.dot_general(kzb, v, _TN, preferred_element_type=F32)
    o = o + jnp.concatenate(cross, axis=0) * xi
    o_ref[...] = _ret_finish(o, g_ref[...]).astype(o_ref.dtype)


def _retention_sample(q, k, vg, state, layer, nb, t):
    rows = nb * SAMPLE_PAD
    dmat, xi, zeta, gc = _ret_tables(t, SAMPLE_PAD)
    eye = jnp.eye(nb, dtype=F32)
    dbig = jnp.einsum("ab,hij->haibj", eye, dmat).reshape(RET_HEADS, rows, rows)
    xib = jnp.tile(xi, (1, nb, 1))
    zetab = jnp.tile(zeta, (1, nb, 1))
    col = lambda h: (0, h)
    tab = lambda h: (h, 0, 0)
    return pl.pallas_call(
        functools.partial(_ret_sample_body, nb=nb),
        grid=(RET_HEADS,),
        in_specs=[pl.BlockSpec((rows, RET_DK), col),
                  pl.BlockSpec((rows, RET_DK), col),
                  pl.BlockSpec((rows, RET_DV), col),
                  pl.BlockSpec((rows, RET_DV), lambda h: (0, RET_HEADS + h)),
                  pl.BlockSpec((None, nb, None, RET_DK, RET_DV), lambda h: (layer, 0, h, 0, 0)),
                  pl.BlockSpec((None, rows, rows), tab),
                  pl.BlockSpec((None, rows, 1), tab),
                  pl.BlockSpec((None, rows, 1), tab),
                  pl.BlockSpec((None, 1, RET_DV), tab)],
        out_specs=[pl.BlockSpec((rows, RET_DV), col),
                   pl.BlockSpec((nb, None, RET_DK, RET_DV), lambda h: (0, h, 0, 0))],
        out_shape=[jax.ShapeDtypeStruct((rows, RET_V), BF16),
                   jax.ShapeDtypeStruct((nb, RET_HEADS, RET_DK, RET_DV), F32)],
        compiler_params=_params(1),
        name="retention_sample",
    )(q, k, vg, vg, state, dbig, xib, zetab, gc)


DIL_PROMPT_SPAN = DIL_SPAN * max(DIL_RATES)
DIL_PROMPT_HEADS = 2


def _dil_prompt_body(*refs, span, hpb):
    ng = DIL_GROUPS
    q_refs, kc_refs, kp_refs = refs[0:ng], refs[ng:2 * ng], refs[2 * ng:3 * ng]
    vc_refs, vp_refs = refs[3 * ng:4 * ng], refs[4 * ng:5 * ng]
    gate_ref, o_ref, acc_s, m_s, l_s, sc_s, sp_s, pc_s, pp_s = refs[5 * ng:]
    has_prev = pl.program_id(1) > 0
    ii = lax.broadcasted_iota(jnp.int32, (DIL_SPAN, DIL_SPAN), 0)
    jj = lax.broadcasted_iota(jnp.int32, (DIL_SPAN, DIL_SPAN), 1)
    mask_c = jj <= ii
    mask_p = jj >= ii
    mask_p0 = jnp.logical_and(mask_p, has_prev)

    def prev_refs(g, r, n, cur_ref, prv_ref, hs):
        if n > 0:
            return cur_ref[r, (n - 1) * DIL_SPAN:n * DIL_SPAN, hs]
        return prv_ref[r, :, hs]

    for hl in range(hpb):
        hs = slice(hl * DIL_HD, (hl + 1) * DIL_HD)
        for g in range(ng):
            d = DIL_RATES[g]
            blocks = [(r, n) for r in range(d) for n in range(span // (DIL_SPAN * d))]
            for i, (r, n) in enumerate(blocks):
                rows = slice(n * DIL_SPAN, (n + 1) * DIL_SPAN)
                q = q_refs[g][r, rows, hs]
                kp = prev_refs(g, r, n, kc_refs[g], kp_refs[g], hs)
                s_c = lax.dot_general(q, kc_refs[g][r, rows, hs], _NT, preferred_element_type=F32)
                s_p = lax.dot_general(q, kp, _NT, preferred_element_type=F32)
                sc_s[i] = jnp.where(mask_c, s_c, NEG)
                sp_s[i] = jnp.where(mask_p if n > 0 else mask_p0, s_p, NEG)
            for i, (r, n) in enumerate(blocks):
                s_c = sc_s[i]
                s_p = sp_s[i]
                m = jnp.max(jnp.maximum(s_c, s_p), axis=-1, keepdims=True)
                p_c = jnp.exp(s_c - m)
                p_p = jnp.exp(s_p - m)
                den = jnp.sum(p_c + p_p, axis=-1, keepdims=True)
                pc_s[i] = p_c.astype(BF16)
                pp_s[i] = p_p.astype(BF16)
                dst = slice(n * DIL_SPAN, (n + 1) * DIL_SPAN) if d == 1 else pl.ds(n * DIL_SPAN * d + r, DIL_SPAN, stride=d)
                m_s[g, dst, :] = jnp.broadcast_to(m, (DIL_SPAN, LANE))
                l_s[g, dst, :] = jnp.broadcast_to(den, (DIL_SPAN, LANE))
            for i, (r, n) in enumerate(blocks):
                rows = slice(n * DIL_SPAN, (n + 1) * DIL_SPAN)
                vp = prev_refs(g, r, n, vc_refs[g], vp_refs[g], hs)
                acc = jnp.dot(pc_s[i], vc_refs[g][r, rows, hs], preferred_element_type=F32)
                acc = acc + jnp.dot(pp_s[i], vp, preferred_element_type=F32)
                dst = rows if d == 1 else pl.ds(n * DIL_SPAN * d + r, DIL_SPAN, stride=d)
                acc_s[g, dst, :] = acc
        mr = 64
        for c in range(span // mr):
            rs = slice(c * mr, (c + 1) * mr)
            ms = [m_s[g, rs, :] for g in range(ng)]
            m_all = jnp.maximum(jnp.maximum(ms[0], ms[1]), ms[2])
            es = [jnp.exp(m - m_all) for m in ms]
            tot = es[0] * l_s[0, rs, :] + es[1] * l_s[1, rs, :] + es[2] * l_s[2, rs, :]
            num = es[0] * acc_s[0, rs, :] + es[1] * acc_s[1, rs, :] + es[2] * acc_s[2, rs, :]
            o_ref[rs, hs] = (num / tot * _silu(gate_ref[rs, hs].astype(F32))).astype(o_ref.dtype)


def _dilated_prompt(qkvs, gate, batch, seq):
    span, hpb = DIL_PROMPT_SPAN, DIL_PROMPT_HEADS
    wc = hpb * DIL_HD
    kcol, vcol = DIL_W // wc, 2 * DIL_W // wc
    nsp = seq // span
    views = [a.reshape(batch, d, seq // d, 3 * DIL_W) for a, d in zip(qkvs, DIL_RATES)]
    q_specs, kc_specs, kp_specs, vc_specs, vp_specs = [], [], [], [], []
    for d in DIL_RATES:
        cur = (None, d, span // d, wc)
        prv = (None, d, DIL_SPAN, wc)
        nblk = span // (d * DIL_SPAN)
        q_specs.append(pl.BlockSpec(cur, lambda b, s, h: (b, 0, s, h)))
        kc_specs.append(pl.BlockSpec(cur, lambda b, s, h: (b, 0, s, kcol + h)))
        vc_specs.append(pl.BlockSpec(cur, lambda b, s, h: (b, 0, s, vcol + h)))
        kp_specs.append(pl.BlockSpec(prv, lambda b, s, h, nblk=nblk: (b, 0, jnp.maximum(s * nblk - 1, 0), kcol + h)))
        vp_specs.append(pl.BlockSpec(prv, lambda b, s, h, nblk=nblk: (b, 0, jnp.maximum(s * nblk - 1, 0), vcol + h)))
    rows_spec = pl.BlockSpec((span, wc), lambda b, s, h: (b * nsp + s, h))
    return pl.pallas_call(
        functools.partial(_dil_prompt_body, span=span, hpb=hpb),
        grid=(batch, nsp, DIL_W // wc),
        in_specs=q_specs + kc_specs + kp_specs + vc_specs + vp_specs + [rows_spec],
        out_specs=rows_spec,
        out_shape=jax.ShapeDtypeStruct((batch * seq, DIL_W), BF16),
        scratch_shapes=([pltpu.VMEM((DIL_GROUPS, span, LANE), F32)] * 3
                        + [pltpu.VMEM((span // DIL_SPAN, DIL_SPAN, DIL_SPAN), F32)] * 2
                        + [pltpu.VMEM((span // DIL_SPAN, DIL_SPAN, DIL_SPAN), BF16)] * 2),
        compiler_params=_params(3),
        name="dilated_prompt",
    )(*(views * 5), gate)


def _dil_sample_body(q_ref, kn_ref, vn_ref, gate_ref, c0_ref, c1_ref, c2_ref, o_ref, *, n_new):
    t = pl.program_id(1)

    @pl.when(t == 0)
    def _():
        o_ref[...] = jnp.zeros_like(o_ref)

    key_row = lax.broadcasted_iota(jnp.int32, (DIL_SPAN, 1, 1), 0)
    ms, dens, accs = [], [], []
    for g, c_ref in enumerate((c0_ref, c1_ref, c2_ref)):
        hsl = slice(g * DIL_HEADS, (g + 1) * DIL_HEADS)
        q = q_ref[t, hsl, :]
        s = jnp.sum(c_ref[:, 0] * q[None], axis=-1, keepdims=True)
        if DIL_RATES[g] == 1:
            s = jnp.where(key_row >= t, s, NEG)
            new_rows = [(tn, tn <= t) for tn in range(n_new)]
        else:
            new_rows = [(t, None)]
        s_new = []
        for tn, valid in new_rows:
            sn = jnp.sum(kn_ref[tn, hsl, :] * q, axis=-1, keepdims=True)
            s_new.append(sn if valid is None else jnp.where(valid, sn, NEG))
        m = jnp.max(s, axis=0)
        for sn in s_new:
            m = jnp.maximum(m, sn)
        p = jnp.exp(s - m[None])
        den = jnp.sum(p, axis=0)
        acc = jnp.sum(p * c_ref[:, 1], axis=0)
        for (tn, _), sn in zip(new_rows, s_new):
            pn = jnp.exp(sn - m)
            den = den + pn
            acc = acc + pn * vn_ref[tn, hsl, :]
        ms.append(m)
        dens.append(den)
        accs.append(acc)
    m_all = jnp.maximum(jnp.maximum(ms[0], ms[1]), ms[2])
    es = [jnp.exp(m - m_all) for m in ms]
    tot = es[0] * dens[0] + es[1] * dens[1] + es[2] * dens[2]
    merged = (es[0] * accs[0] + es[1] * accs[1] + es[2] * accs[2]) / tot
    o_ref[t] = merged * _silu(gate_ref[t])


def _dilated_sample(q3, k3, v3, gate, caches, layer, nb, t):
    rows = nb * SAMPLE_PAD
    views, specs = [], []
    for g, d in enumerate(DIL_RATES):
        c = caches[g]
        assert c.shape[2] == DIL_SPAN * d and t <= d * (1 if d > 1 else DIL_SPAN)
        views.append(c.reshape(c.shape[0], nb, DIL_SPAN, d, 2, DIL_HEADS, DIL_HD))
        blk = (None, None, DIL_SPAN, None, 2, DIL_HEADS, DIL_HD)
        if d == 1:
            specs.append(pl.BlockSpec(blk, lambda b, i: (layer, b, 0, 0, 0, 0, 0)))
        else:
            specs.append(pl.BlockSpec(blk, lambda b, i: (layer, b, 0, i, 0, 0, 0)))
    heads3 = lambda a: a.reshape(rows, DIL_GROUPS * DIL_HEADS, DIL_HD)
    rows3 = pl.BlockSpec((SAMPLE_PAD, DIL_GROUPS * DIL_HEADS, DIL_HD), lambda b, i: (b, 0, 0))
    rows1 = pl.BlockSpec((SAMPLE_PAD, DIL_HEADS, DIL_HD), lambda b, i: (b, 0, 0))
    out = pl.pallas_call(
        functools.partial(_dil_sample_body, n_new=t),
        grid=(nb, t),
        in_specs=[rows3, rows3, rows3, rows1] + specs,
        out_specs=rows1,
        out_shape=jax.ShapeDtypeStruct((rows, DIL_HEADS, DIL_HD), F32),
        compiler_params=_params(2),
        name="dilated_sample",
    )(heads3(q3), heads3(k3), heads3(v3), gate.reshape(rows, DIL_HEADS, DIL_HD), *views)
    return out.reshape(rows, DIL_W)


def _cache_shift_body(a_ref, nxt_ref, new_ref, o_ref, *, t, tb):
    last = pl.program_id(1) == pl.num_programs(1) - 1
    o_ref[0:tb - t] = a_ref[t:tb]

    @pl.when(last)
    def _():
        o_ref[tb - t:tb] = new_ref[...]

    @pl.when(jnp.logical_not(last))
    def _():
        o_ref[tb - t:tb] = nxt_ref[...]


def _cache_shift(cache, new, layer, nb, t):
    wb = cache.shape[2]
    tail = cache.shape[3:]
    tb = min(wb, 256)
    assert wb % tb == 0 and tb % t == 0
    zeros = (0,) * len(tail)
    return pl.pallas_call(
        functools.partial(_cache_shift_body, t=t, tb=tb),
        grid=(nb, wb // tb),
        in_specs=[pl.BlockSpec((None, None, tb) + tail, lambda b, i: (layer, b, i) + zeros),
                  pl.BlockSpec((None, None, t) + tail,
                               lambda b, i: (layer, b, jnp.minimum((i + 1) * (tb // t), wb // t - 1)) + zeros),
                  pl.BlockSpec((None, t) + tail, lambda b, i: (b, 0) + zeros)],
        out_specs=pl.BlockSpec((None, tb) + tail, lambda b, i: (b, i) + zeros),
        out_shape=jax.ShapeDtypeStruct(cache.shape[1:], cache.dtype),
        compiler_params=_params(2),
        name="cache_shift",
    )(cache, cache, new)


def _gmlp_body(u_ref, v_ref, gate_ref, lng_ref, lnb_ref, wm_ref, bs_ref, o_ref, *vn_ref):
    v = v_ref[...].astype(F32)
    mu = jnp.mean(v, axis=-1, keepdims=True)
    xc = v - mu
    vn = xc * lax.rsqrt(jnp.mean(xc * xc, axis=-1, keepdims=True) + EPS) * lng_ref[...] + lnb_ref[...]
    if vn_ref:
        vn_ref[0][...] = vn
    vb = vn.astype(BF16)
    for g in range(GM_GROUPS):
        gs = slice(g * GM_GD, (g + 1) * GM_GD)
        mixed = jnp.dot(wm_ref[g], vb[:, gs], preferred_element_type=F32) + bs_ref[g]
        o_ref[:, gs] = (u_ref[:, gs].astype(F32) * mixed * _silu(gate_ref[:, gs].astype(F32))).astype(o_ref.dtype)


def _gmlp_core(uv, gate, ln_g3, ln_b3, layer, wm, bs, want_vn):
    rows = gate.shape[0]
    c = wm.shape[1]
    row = lambda i: (i, 0)
    out_specs = [pl.BlockSpec((c, GM_WIDTH), row)]
    out_shape = [jax.ShapeDtypeStruct((rows, GM_WIDTH), BF16)]
    if want_vn:
        out_specs.append(pl.BlockSpec((c, GM_WIDTH), row))
        out_shape.append(jax.ShapeDtypeStruct((rows, GM_WIDTH), F32))
    res = pl.pallas_call(
        _gmlp_body,
        grid=(rows // c,),
        in_specs=[pl.BlockSpec((c, GM_WIDTH), row),
                  pl.BlockSpec((c, GM_WIDTH), lambda i: (i, 1)),
                  pl.BlockSpec((c, GM_WIDTH), row),
                  pl.BlockSpec((None, 1, GM_WIDTH), lambda i: (layer, 0, 0)),
                  pl.BlockSpec((None, 1, GM_WIDTH), lambda i: (layer, 0, 0)),
                  pl.BlockSpec((GM_GROUPS, c, c), lambda i: (0, 0, 0)),
                  pl.BlockSpec((GM_GROUPS, c, 1), lambda i: (0, 0, 0))],
        out_specs=out_specs,
        out_shape=out_shape,
        compiler_params=_params(1),
        name="gmlp_core",
    )(uv, uv, gate, ln_g3, ln_b3, wm, bs)
    return res


class _Stream:
    def __init__(self, batch, t, t_pad, pos0, tm, act_dtype):
        self.batch, self.t, self.t_pad, self.tm, self.act = batch, t, t_pad, tm, act_dtype
        self.rows = batch * t_pad
        pos = pos0 + jnp.arange(t_pad, dtype=jnp.int32)
        cos, sin = _rope_tables(pos, RET_DK // 2)
        self.ret_rope = (cos, sin)
        cos, sin = _rope_tables(pos, DIL_HD // 2)
        self.dil_rope = (jnp.concatenate([cos, cos], axis=-1), jnp.concatenate([-sin, sin], axis=-1))
        self.rope_tiles = max(t_pad // tm, 1)
        if t_pad < tm:
            rep = tm // t_pad
            self.ret_rope = tuple(jnp.tile(a, (rep, 1)) for a in self.ret_rope)
            self.dil_rope = tuple(jnp.tile(a, (rep, 1)) for a in self.dil_rope)

    def rope_specs(self):
        nt = self.rope_tiles
        return [pl.BlockSpec((self.tm, LANE), lambda j, i: (i % nt, 0))] * 2

    def dil_rope_residue_major(self, d):
        t = self.t_pad
        return tuple(a.reshape(t // d, d, LANE).transpose(1, 0, 2).reshape(t, LANE) for a in self.dil_rope)


def _proj(st, h, w3, layer, n0, n, tn, epilogue, out_dtype, extras=(), extra_specs=(), name="proj"):
    return _matmul(h, w3, layer, n0, n, st.tm, tn, epilogue, extras, extra_specs, (out_dtype,), name)


def _out_and_ple(st, y_in, w_out3, jl, x, i, p3, ple_w, ple_g3, ple_w_gate):
    tm = st.tm
    tn = 512
    res_spec = pl.BlockSpec((tm, tn), lambda j, m: (m, j))
    x1 = _matmul(y_in, w_out3, jl, 0, D_MODEL, tm, tn, _ep_residual, (x,), (res_spec,), (F32,), "out_proj")
    hn = _rmsnorm(x1, ple_g3, i, BF16, min(256, st.rows))
    tn = 512
    extras = (x1, p3, ple_w)
    specs = (pl.BlockSpec((tm, tn), lambda j, m: (m, j)),
             pl.BlockSpec((None, tm, PLE_DIM), lambda j, m: (i, m, 0)),
             pl.BlockSpec((None, PLE_DIM, tn), lambda j, m: (i, 0, j)))
    return _matmul(hn, ple_w_gate, i, 0, D_MODEL, tm, tn, _ep_ple, extras, specs, (F32,), "ple")


def _retention_layer(st, h, w_in, jl, state):
    tn = 1024
    q = _proj(st, h, w_in, jl, 0, RET_QK, tn, functools.partial(_ep_rope_ret, scale=1.0), st.act,
              st.ret_rope, st.rope_specs(), "ret_q")
    k = _proj(st, h, w_in, jl, RET_QK, RET_QK, tn, functools.partial(_ep_rope_ret, scale=RET_DK ** -0.5), st.act,
              st.ret_rope, st.rope_specs(), "ret_k")
    vg = _proj(st, h, w_in, jl, 2 * RET_QK, 2 * RET_V, tn, _ep_plain, st.act, name="ret_vg")
    if state is None:
        return _retention_prompt(q, k, vg, st.batch, st.t)
    return _retention_sample(q, k, vg, state, jl, st.batch, st.t)


def _dilated_proj(st, h, w_in, jl):
    tn = 1024
    per_group = 3 * DIL_W

    def grouped(kind, epilogue, extras=(), extra_specs=(), name=""):
        tiles = DIL_W // tn
        m, k = h.shape
        in_specs = [pl.BlockSpec((st.tm, k), lambda j, i: (i, 0)),
                    pl.BlockSpec((None, k, tn),
                                 lambda j, i: (jl, 0, (j // tiles) * (per_group // tn) + kind * tiles + j % tiles))]
        return pl.pallas_call(
            functools.partial(_mm_body, n_extra=len(extras), epilogue=epilogue),
            grid=(DIL_GROUPS * tiles, m // st.tm),
            in_specs=in_specs + list(extra_specs),
            out_specs=[pl.BlockSpec((st.tm, tn), lambda j, i: (i, j))],
            out_shape=[jax.ShapeDtypeStruct((m, DIL_GROUPS * DIL_W), st.act)],
            scratch_shapes=[pltpu.VMEM((k, tn), BF16)],
            compiler_params=_params(2),
            name=name,
        )(h, w_in, *extras)[0]

    q3 = grouped(0, functools.partial(_ep_rope_dil, scale=DIL_HD ** -0.5), st.dil_rope, st.rope_specs(), "dil_q")
    k3 = grouped(1, functools.partial(_ep_rope_dil, scale=1.0), st.dil_rope, st.rope_specs(), "dil_k")
    v3 = grouped(2, _ep_plain, name="dil_v")
    gate = _proj(st, h, w_in, jl, DIL_GROUPS * per_group, DIL_W, tn, _ep_plain, st.act, name="dil_gate")
    return q3, k3, v3, gate


def _dilated_proj_prompt(st, hs_by_rate, w_in, jl):
    tn = 1024
    tiles = DIL_W // tn
    qkvs = []
    for g, d in enumerate(DIL_RATES):
        rope = st.dil_rope if d == 1 else st.dil_rope_residue_major(d)
        epilogues = ((0, functools.partial(_ep_rope_dil, scale=DIL_HD ** -0.5)),
                     (tiles, functools.partial(_ep_rope_dil, scale=1.0)),
                     (2 * tiles, _ep_plain))
        qkvs.append(_proj(st, hs_by_rate[g], w_in, jl, g * 3 * DIL_W, 3 * DIL_W, tn, epilogues, st.act, rope,
                          st.rope_specs(), f"dil_qkv_g{g}"))
    gate = _proj(st, hs_by_rate[0], w_in, jl, DIL_GROUPS * 3 * DIL_W, DIL_W, tn, _ep_plain, st.act, name="dil_gate")
    return qkvs, gate


def _window_rows_prompt(qkv, st, d):
    kv = qkv.reshape(st.batch, d, st.t // d, 3, DIL_HEADS, DIL_HD)[:, :, st.t // d - DIL_SPAN:, 1:]
    kv = kv.transpose(0, 2, 1, 3, 4, 5).reshape(st.batch, DIL_SPAN * d, 2, DIL_HEADS, DIL_HD)
    return kv.astype(F32)


def _window_rows(k3, v3, st, g, n_rows):
    gs = slice(g * DIL_W, (g + 1) * DIL_W)
    k = k3[:, gs].reshape(st.batch, st.t_pad, DIL_HEADS, DIL_HD)[:, st.t - n_rows:st.t]
    v = v3[:, gs].reshape(st.batch, st.t_pad, DIL_HEADS, DIL_HD)[:, st.t - n_rows:st.t]
    return jnp.stack([k, v], axis=2).astype(F32)


def _gmlp_layer(st, h, w_in, jl, ln_g3, ln_b3, wm, bs, want_vn):
    tn = 1024
    uv = _proj(st, h, w_in, jl, 0, 2 * GM_WIDTH, tn, _ep_gelu, st.act, name="gm_uv")
    gate = _proj(st, h, w_in, jl, 2 * GM_WIDTH, GM_WIDTH, tn, _ep_plain, st.act, name="gm_gate")
    return _gmlp_core(uv, gate, ln_g3, ln_b3, jl, wm, bs, want_vn)


def kernel(x_prompt, x_sample, state_ret, cache_win_g0, cache_win_g1, cache_win_g2, p_prompt, p_sample, norm_g,
           ret_w_in, ret_w_out, dil_w_in, dil_w_out, gm_w_in, gm_ln_g, gm_ln_b, gm_w_s, gm_b_s, gm_w_out, ple_w,
           ple_norm_g, ple_w_gate, final_norm_g):
    depth = norm_g.shape[0]
    bp, sp, _ = x_prompt.shape
    bs_, ts, _ = x_sample.shape
    assert ts <= SAMPLE_PAD and ts % GM_CHUNK != 0 and sp % (DIL_SPAN * max(DIL_RATES)) == 0
    caches = (cache_win_g0, cache_win_g1, cache_win_g2)

    prm = _Stream(bp, sp, sp, 0, 1024, BF16)
    smp = _Stream(bs_, ts, SAMPLE_PAD, PAST_LEN, bs_ * SAMPLE_PAD, F32)

    pad_t = ((0, 0), (0, SAMPLE_PAD - ts), (0, 0))
    xp = x_prompt.reshape(prm.rows, D_MODEL)
    xs = jnp.pad(x_sample, pad_t).reshape(smp.rows, D_MODEL)
    pp = p_prompt.astype(BF16).reshape(depth, prm.rows, PLE_DIM)
    ps = jnp.pad(p_sample, ((0, 0),) + pad_t).astype(BF16).reshape(depth, smp.rows, PLE_DIM)

    norm_g3 = norm_g[:, None, :]
    ple_g3 = ple_norm_g[:, None, :]
    ln_g3 = gm_ln_g[:, None, :]
    ln_b3 = gm_ln_b[:, None, :]
    fin_g3 = final_norm_g[None, None, :]

    ret_p, ret_s, gm_s = [], [], []
    win_p = [[], [], []]
    win_s = [[], [], []]
    for i in range(depth):
        kind, jl = i % 3, i // 3
        if kind == 1:
            hp_orders = _rmsnorm_orders(xp, norm_g3, i, bp, sp, DIL_RATES[1:], 256)
        else:
            hp = _rmsnorm(xp, norm_g3, i, BF16, 256)
        hs = _rmsnorm(xs, norm_g3, i, BF16, smp.rows)
        if kind == 0:
            yp, sp_new = _retention_layer(prm, hp, ret_w_in, jl, None)
            ys, ss_new = _retention_layer(smp, hs, ret_w_in, jl, state_ret)
            ret_p.append(sp_new)
            ret_s.append(ss_new)
            w_out = ret_w_out
        elif kind == 1:
            qkvs, gate = _dilated_proj_prompt(prm, hp_orders, dil_w_in, jl)
            yp = _dilated_prompt(qkvs, gate, bp, sp)
            for g in range(DIL_GROUPS):
                win_p[g].append(_window_rows_prompt(qkvs[g], prm, DIL_RATES[g]))
            q3s, k3s, v3s, gate_s = _dilated_proj(smp, hs, dil_w_in, jl)
            ys = _dilated_sample(q3s, k3s, v3s, gate_s, caches, jl, bs_, ts).astype(BF16)
            for g in range(DIL_GROUPS):
                win_s[g].append(_cache_shift(caches[g], _window_rows(k3s, v3s, smp, g, ts), jl, bs_, ts))
            w_out = dil_w_out
        else:
            c = GM_CHUNK
            wm_p = jnp.tril(gm_w_s[jl]).astype(BF16)
            bs_p = gm_b_s[jl][:, :, None]
            (yp,) = _gmlp_layer(prm, hp, gm_w_in, jl, ln_g3, ln_b3, wm_p, bs_p, False)
            wm_t = jnp.pad(jnp.tril(gm_w_s[jl][:, :ts, :ts]), ((0, 0), (0, SAMPLE_PAD - ts), (0, SAMPLE_PAD - ts)))
            wm_s = jnp.einsum("ab,gij->gaibj", jnp.eye(bs_, dtype=F32), wm_t).reshape(GM_GROUPS, smp.rows, smp.rows)
            bs_s = jnp.tile(jnp.pad(gm_b_s[jl][:, :ts], ((0, 0), (0, SAMPLE_PAD - ts))), (1, bs_))[:, :, None]
            ys, vn = _gmlp_layer(smp, hs, gm_w_in, jl, ln_g3, ln_b3, wm_s.astype(BF16), bs_s, True)
            gm_s.append(vn.reshape(bs_, SAMPLE_PAD, GM_WIDTH)[:, :ts])
            w_out = gm_w_out
        xp = _out_and_ple(prm, yp, w_out, jl, xp, i, pp, ple_w, ple_g3, ple_w_gate)
        xs = _out_and_ple(smp, ys.astype(BF16), w_out, jl, xs, i, ps, ple_w, ple_g3, ple_w_gate)

    y_prompt = _rmsnorm(xp, fin_g3, 0, F32, 256).reshape(bp, sp, D_MODEL)
    y_sample = _rmsnorm(xs, fin_g3, 0, F32, smp.rows).reshape(bs_, SAMPLE_PAD, D_MODEL)[:, :ts]
    return (y_prompt, y_sample,
            jnp.stack(ret_p), jnp.stack(ret_s),
            jnp.stack(win_p[0]), jnp.stack(win_s[0]),
            jnp.stack(win_p[1]), jnp.stack(win_s[1]),
            jnp.stack(win_p[2]), jnp.stack(win_s[2]),
            jnp.stack(gm_s))
```

```python
import functools
from typing import NamedTuple

import jax
import jax.numpy as jnp
from jax import lax
from jax.experimental import pallas as pl
from jax.experimental.pallas import tpu as pltpu

F32 = jnp.float32
BF16 = jnp.bfloat16

D_MODEL = 2048
PAST_LEN = 16384
PLE_DIM = 256
ROPE_THETA = 10000.0
EPS = 1e-6
NEG = -1e30

RET_HEADS = 8
RET_DK = 256
RET_DV = 512
RET_QK = RET_HEADS * RET_DK
RET_V = RET_HEADS * RET_DV
RET_CHUNK = 128

DIL_RATES = (1, 4, 16)
DIL_GROUPS = 3
DIL_SPAN = 128
DIL_HEADS = 16
DIL_HD = 128
DIL_W = DIL_HEADS * DIL_HD

GM_WIDTH = 2 * D_MODEL
GM_GROUPS = 16
GM_GD = GM_WIDTH // GM_GROUPS
GM_CHUNK = 128

LANE = 128
MXU_COLS = 256
SAMPLE_PAD = 16
VMEM_LIMIT = 56 * 1024 * 1024


def _params(n_axes, vmem=VMEM_LIMIT):
    return pltpu.CompilerParams(dimension_semantics=("arbitrary",) * n_axes, vmem_limit_bytes=vmem)


def _silu(x):
    return x * jax.nn.sigmoid(x)


def _rms_body(x_ref, g_ref, o_ref):
    x = x_ref[...]
    y = x * lax.rsqrt(jnp.mean(x * x, axis=-1, keepdims=True) + EPS)
    o_ref[...] = (y * g_ref[...]).astype(o_ref.dtype)


def _rmsnorm(x, g3, layer, out_dtype, tr):
    r, d = x.shape
    return pl.pallas_call(
        _rms_body,
        grid=(r // tr,),
        in_specs=[pl.BlockSpec((tr, d), lambda i: (i, 0)),
                  pl.BlockSpec((None, 1, d), lambda i: (layer, 0, 0))],
        out_specs=pl.BlockSpec((tr, d), lambda i: (i, 0)),
        out_shape=jax.ShapeDtypeStruct((r, d), out_dtype),
        compiler_params=_params(1),
        name="rmsnorm",
    )(x, g3)


def _rms_orders_body(x_ref, g_ref, perm_ref, o_ref, *perm_out_refs, rates):
    x = x_ref[...]
    y = (x * lax.rsqrt(jnp.mean(x * x, axis=-1, keepdims=True) + EPS) * g_ref[...]).astype(o_ref.dtype)
    o_ref[...] = y
    tr = x.shape[0]
    for k, (p_ref, d) in enumerate(zip(perm_out_refs, rates)):
        yp = jnp.dot(perm_ref[k], y, preferred_element_type=F32).astype(o_ref.dtype)
        for r in range(d):
            p_ref[r] = yp[r * (tr // d):(r + 1) * (tr // d), :]


def _rmsnorm_orders(x, g3, layer, batch, seq, rates, tr):
    r, dm = x.shape
    nt = seq // tr
    rows = jnp.arange(tr, dtype=jnp.int32)
    perms = jnp.stack([(rows[None, :] == (rows % (tr // d) * d + rows // (tr // d))[:, None]) for d in rates])
    out_specs = [pl.BlockSpec((tr, dm), lambda i: (i, 0))]
    out_shape = [jax.ShapeDtypeStruct((r, dm), BF16)]
    for d in rates:
        out_specs.append(pl.BlockSpec((None, d, tr // d, dm), lambda i: (i // nt, 0, i % nt, 0)))
        out_shape.append(jax.ShapeDtypeStruct((batch, d, seq // d, dm), BF16))
    res = pl.pallas_call(
        functools.partial(_rms_orders_body, rates=rates),
        grid=(r // tr,),
        in_specs=[pl.BlockSpec((tr, dm), lambda i: (i, 0)),
                  pl.BlockSpec((None, 1, dm), lambda i: (layer, 0, 0)),
                  pl.BlockSpec((len(rates), tr, tr), lambda i: (0, 0, 0))],
        out_specs=out_specs,
        out_shape=out_shape,
        compiler_params=_params(1),
        name="rmsnorm_orders",
    )(x, g3, perms.astype(BF16))
    return [res[0]] + [a.reshape(r, dm) for a in res[1:]]


class _Operand(NamedTuple):
    x: jax.Array
    tm: int
    tiled: bool
    epilogue: object
    extras: tuple = ()
    extra_specs: tuple = ()
    outs: tuple = (("tile", F32),)
    ssq: object = None


def _mm_body(w_ref, *refs, layout):
    wb_ref = refs[-1]
    j = pl.program_id(0)
    i = pl.program_id(1)

    @pl.when(i == 0)
    def _():
        wb_ref[...] = w_ref[...].astype(BF16)

    def stream(x_ref, ssq_ref, extra, outs, epilogue):
        rinv = None
        if ssq_ref is not None:
            mean_sq = jnp.sum(jnp.sum(ssq_ref[...], axis=0), axis=-1, keepdims=True) / x_ref.shape[1]
            rinv = lax.rsqrt(mean_sq + EPS)

        def run(ep):
            for c in range(wb_ref.shape[1] // MXU_COLS):
                cs = slice(c * MXU_COLS, (c + 1) * MXU_COLS)
                acc = jnp.dot(x_ref[...], wb_ref[:, cs], preferred_element_type=F32)
                ep(acc if rinv is None else acc * rinv, cs, extra, outs)

        if callable(epilogue):
            run(epilogue)
        else:
            starts = [s for s, _ in epilogue] + [None]
            for (lo, ep), hi in zip(epilogue, starts[1:]):
                pred = j >= lo if hi is None else jnp.logical_and(j >= lo, j < hi)
                pl.when(pred)(functools.partial(run, ep))

    n_in = sum(n for n, _, _, _, _ in layout)
    pos, opos = 0, n_in
    for n, n_out, has_ssq, tiled, epilogue in layout:
        x_ref = refs[pos]
        ssq_ref = refs[pos + 1] if has_ssq else None
        extra = refs[pos + 1 + has_ssq:pos + n]
        outs = refs[opos:opos + n_out]
        pos, opos = pos + n, opos + n_out
        body = functools.partial(stream, x_ref, ssq_ref, extra, outs, epilogue)
        if tiled:
            body()
        else:
            pl.when(i == 0)(body)


def _matmul(w3, layer, n0, n, tn, operands, name):
    k = w3.shape[1]
    j0 = n0 // tn
    steps = [op.x.shape[0] // op.tm for op in operands if op.tiled]
    in_specs = [pl.BlockSpec((None, k, tn), lambda j, i: (layer, 0, j + j0))]
    args, out_specs, out_shape, layout = [w3], [], [], []
    for op in operands:
        rows = op.x.shape[0]
        row = (lambda i: i) if op.tiled else (lambda i: 0)
        in_specs.append(pl.BlockSpec((op.tm, k), lambda j, i, row=row: (row(i), 0)))
        args.append(op.x)
        if op.ssq is not None:
            in_specs.append(pl.BlockSpec((op.ssq.shape[0], op.tm, LANE), lambda j, i, row=row: (0, row(i), 0)))
            args.append(op.ssq)
        in_specs += list(op.extra_specs)
        args += list(op.extras)
        for out in op.outs:
            if out[0] == "tile":
                out_specs.append(pl.BlockSpec((op.tm, tn), lambda j, i, row=row: (row(i), j)))
                out_shape.append(jax.ShapeDtypeStruct((rows, n), out[1]))
            else:
                out_specs.append(pl.BlockSpec((None, op.tm, LANE), lambda j, i, row=row: (j, row(i), 0)))
                out_shape.append(jax.ShapeDtypeStruct((n // tn, rows, LANE), F32))
        layout.append((1 + (op.ssq is not None) + len(op.extras), len(op.outs), op.ssq is not None, op.tiled,
                       op.epilogue))
    res = pl.pallas_call(
        functools.partial(_mm_body, layout=tuple(layout)),
        grid=(n // tn, steps[0]),
        in_specs=in_specs,
        out_specs=out_specs,
        out_shape=out_shape,
        scratch_shapes=[pltpu.VMEM((k, tn), BF16)],
        compiler_params=_params(2),
        name=name,
    )(*args)
    groups, pos = [], 0
    for op in operands:
        groups.append(tuple(res[pos:pos + len(op.outs)]))
        pos += len(op.outs)
    return groups


def _ep_plain(acc, cs, extra, outs):
    outs[0][:, cs] = acc.astype(outs[0].dtype)


def _ep_gelu(acc, cs, extra, outs):
    outs[0][:, cs] = jax.nn.gelu(acc).astype(outs[0].dtype)


def _emit_stream(x_new, cs, g_ref, outs):
    outs[0][:, cs] = x_new
    if len(outs) == 1:
        return
    outs[1][:, cs] = (x_new * g_ref[:, cs]).astype(outs[1].dtype)
    sq = x_new * x_new
    part = sq[:, :LANE] + sq[:, LANE:]
    if cs.start == 0:
        outs[2][...] = part
    else:
        outs[2][...] += part


def _ep_residual(acc, cs, extra, outs):
    _emit_stream(extra[0][:, cs] + acc, cs, extra[1] if len(extra) > 1 else None, outs)


def _ep_rope_ret(acc, cs, extra, outs, *, scale):
    cos = extra[0][...]
    sin = extra[1][...]
    o = outs[0]
    half = RET_DK // 2
    for h in range(acc.shape[1] // RET_DK):
        x1 = acc[:, h * RET_DK:h * RET_DK + half]
        x2 = acc[:, h * RET_DK + half:(h + 1) * RET_DK]
        c0 = cs.start + h * RET_DK
        y1 = x1 * cos - x2 * sin
        y2 = x2 * cos + x1 * sin
        o[:, c0:c0 + half] = (y1 if scale == 1.0 else y1 * scale).astype(o.dtype)
        o[:, c0 + half:c0 + RET_DK] = (y2 if scale == 1.0 else y2 * scale).astype(o.dtype)


def _ep_rope_dil(acc, cs, extra, outs, *, scale):
    cos2 = extra[0][...]
    sin2 = extra[1][...]
    o = outs[0]
    for h in range(acc.shape[1] // DIL_HD):
        x = acc[:, h * DIL_HD:(h + 1) * DIL_HD]
        swapped = pltpu.roll(x, DIL_HD // 2, axis=1)
        c0 = cs.start + h * DIL_HD
        y = x * cos2 + swapped * sin2
        o[:, c0:c0 + DIL_HD] = (y if scale == 1.0 else y * scale).astype(o.dtype)


def _ep_ple(acc, cs, extra, outs):
    pe = jnp.dot(extra[1][...], extra[2][:, cs].astype(BF16), preferred_element_type=F32)
    _emit_stream(extra[0][:, cs] + pe * jax.nn.sigmoid(acc), cs, extra[3] if len(extra) > 3 else None, outs)


def _rope_tables(pos, half):
    inv = ROPE_THETA ** (-jnp.arange(half, dtype=F32) / half)
    ang = pos.astype(F32)[:, None] * inv[None, :]
    return jnp.cos(ang), jnp.sin(ang)


def _ret_tables(c, c_pad):
    lg = jnp.log1p(-jnp.exp2(-5.0 - jnp.arange(RET_HEADS, dtype=F32)))
    i = jnp.arange(c_pad, dtype=F32)
    diff = i[:, None] - i[None, :]
    dmat = jnp.where(diff[None] >= 0, jnp.exp(lg[:, None, None] * jnp.maximum(diff, 0.0)[None]), 0.0)
    xi = jnp.exp(lg[:, None] * (i[None, :] + 1.0))[:, :, None]
    zeta = jnp.exp(lg[:, None] * (c - 1.0 - i[None, :]))[:, :, None]
    gc = jnp.broadcast_to(jnp.exp(lg * c)[:, None, None], (RET_HEADS, 1, RET_DV))
    return dmat, xi, zeta, gc


def _ret_finish(o, g):
    on = o * lax.rsqrt(jnp.mean(o * o, axis=-1, keepdims=True) + EPS)
    return on * _silu(g)


_NT = (((1,), (1,)), ((), ()))
_TN = (((0,), (0,)), ((), ()))


def _ret_prompt_body(q_ref, k_ref, v_ref, g_ref, dmat_ref, xi_ref, zeta_ref, gc_ref, o_ref, s_ref,
                     sc_s, kv_s, sb_s, *, c, cb):
    @pl.when(pl.program_id(2) == 0)
    def _():
        s_ref[...] = jnp.zeros_like(s_ref)

    dmat = dmat_ref[...]
    xi = xi_ref[...]
    zeta = zeta_ref[...]
    gc = gc_ref[...]
    for i in range(cb):
        rows = slice(i * c, (i + 1) * c)
        k = k_ref[rows, :]
        sc = lax.dot_general(q_ref[rows, :], k, _NT, preferred_element_type=F32) * dmat
        sc_s[i] = sc.astype(BF16)
        kz = (k.astype(F32) * zeta).astype(BF16)
        kv_s[i] = lax.dot_general(kz, v_ref[rows, :], _TN, preferred_element_type=F32)
    for i in range(cb):
        s = s_ref[...]
        sb_s[i] = s.astype(BF16)
        s_ref[...] = s * gc + kv_s[i]
    for i in range(cb):
        rows = slice(i * c, (i + 1) * c)
        o = jnp.dot(sc_s[i], v_ref[rows, :], preferred_element_type=F32)
        o = o + jnp.dot(q_ref[rows, :], sb_s[i], preferred_element_type=F32) * xi
        o_ref[rows, :] = _ret_finish(o, g_ref[rows, :].astype(F32)).astype(o_ref.dtype)


def _retention_prompt(q, k, vg, batch, seq):
    c = RET_CHUNK
    cb = 8
    rb = c * cb
    ncb = seq // rb
    dmat, xi, zeta, gc = _ret_tables(c, c)
    row = lambda b, h, i: (b * ncb + i, h)
    tab = lambda b, h, i: (h, 0, 0)
    return pl.pallas_call(
        functools.partial(_ret_prompt_body, c=c, cb=cb),
        grid=(batch, RET_HEADS, ncb),
        in_specs=[pl.BlockSpec((rb, RET_DK), row),
                  pl.BlockSpec((rb, RET_DK), row),
                  pl.BlockSpec((rb, RET_DV), row),
                  pl.BlockSpec((rb, RET_DV), lambda b, h, i: (b * ncb + i, RET_HEADS + h)),
                  pl.BlockSpec((None, c, c), tab),
                  pl.BlockSpec((None, c, 1), tab),
                  pl.BlockSpec((None, c, 1), tab),
                  pl.BlockSpec((None, 1, RET_DV), tab)],
        out_specs=[pl.BlockSpec((rb, RET_DV), row),
                   pl.BlockSpec((None, None, RET_DK, RET_DV), lambda b, h, i: (b, h, 0, 0))],
        out_shape=[jax.ShapeDtypeStruct((batch * seq, RET_V), BF16),
                   jax.ShapeDtypeStruct((batch, RET_HEADS, RET_DK, RET_DV), F32)],
        scratch_shapes=[pltpu.VMEM((cb, c, c), BF16), pltpu.VMEM((cb, RET_DK, RET_DV), F32),
                        pltpu.VMEM((cb, RET_DK, RET_DV), BF16)],
        compiler_params=_params(3),
        name="retention_prompt",
    )(q, k, vg, vg, dmat, xi, zeta, gc)


def _ret_sample_body(q_ref, k_ref, v_ref, g_ref, s0_ref, dmat_ref, xi_ref, zeta_ref, gc_ref, o_ref, s_ref, *, nb):
    q = q_ref[...].astype(BF16)
    k = k_ref[...]
    v = v_ref[...].astype(BF16)
    xi = xi_ref[...]
    gc = gc_ref[...]
    sc = lax.dot_general(q, k.astype(BF16), _NT, preferred_element_type=F32) * dmat_ref[...]
    o = jnp.dot(sc.astype(BF16), v, preferred_element_type=F32)
    kz = (k * zeta_ref[...]).astype(BF16)
    row_batch = lax.broadcasted_iota(jnp.int32, kz.shape, 0) // SAMPLE_PAD
    cross = []
    for b in range(nb):
        s0 = s0_ref[b]
        qb = q[b * SAMPLE_PAD:(b + 1) * SAMPLE_PAD, :]
        cross.append(jnp.dot(qb, s0.astype(BF16), preferred_element_type=F32))
        kzb = jnp.where(row_batch == b, kz, jnp.zeros_like(kz))
        s_ref[b] = s0 * gc + lax.dot_general(kzb, v, _TN, preferred_element_type=F32)
    o = o + jnp.concatenate(cross, axis=0) * xi
    o_ref[...] = _ret_finish(o, g_ref[...]).astype(o_ref.dtype)


def _retention_sample(q, k, vg, state, layer, nb, t):
    rows = nb * SAMPLE_PAD
    dmat, xi, zeta, gc = _ret_tables(t, SAMPLE_PAD)
    eye = jnp.eye(nb, dtype=F32)
    dbig = jnp.einsum("ab,hij->haibj", eye, dmat).reshape(RET_HEADS, rows, rows)
    xib = jnp.tile(xi, (1, nb, 1))
    zetab = jnp.tile(zeta, (1, nb, 1))
    col = lambda h: (0, h)
    tab = lambda h: (h, 0, 0)
    return pl.pallas_call(
        functools.partial(_ret_sample_body, nb=nb),
        grid=(RET_HEADS,),
        in_specs=[pl.BlockSpec((rows, RET_DK), col),
                  pl.BlockSpec((rows, RET_DK), col),
                  pl.BlockSpec((rows, RET_DV), col),
                  pl.BlockSpec((rows, RET_DV), lambda h: (0, RET_HEADS + h)),
                  pl.BlockSpec((None, nb, None, RET_DK, RET_DV), lambda h: (layer, 0, h, 0, 0)),
                  pl.BlockSpec((None, rows, rows), tab),
                  pl.BlockSpec((None, rows, 1), tab),
                  pl.BlockSpec((None, rows, 1), tab),
                  pl.BlockSpec((None, 1, RET_DV), tab)],
        out_specs=[pl.BlockSpec((rows, RET_DV), col),
                   pl.BlockSpec((nb, None, RET_DK, RET_DV), lambda h: (0, h, 0, 0))],
        out_shape=[jax.ShapeDtypeStruct((rows, RET_V), BF16),
                   jax.ShapeDtypeStruct((nb, RET_HEADS, RET_DK, RET_DV), F32)],
        compiler_params=_params(1),
        name="retention_sample",
    )(q, k, vg, vg, state, dbig, xib, zetab, gc)


DIL_PROMPT_SPAN = DIL_SPAN * max(DIL_RATES)
DIL_PROMPT_HEADS = 2


def _dil_prompt_body(*refs, span, hpb):
    ng = DIL_GROUPS
    q_refs, kc_refs, kp_refs = refs[0:ng], refs[ng:2 * ng], refs[2 * ng:3 * ng]
    vc_refs, vp_refs = refs[3 * ng:4 * ng], refs[4 * ng:5 * ng]
    gate_ref, o_ref, acc_s, m_s, l_s, sc_s, sp_s, pc_s, pp_s = refs[5 * ng:]
    has_prev = pl.program_id(1) > 0
    ii = lax.broadcasted_iota(jnp.int32, (DIL_SPAN, DIL_SPAN), 0)
    jj = lax.broadcasted_iota(jnp.int32, (DIL_SPAN, DIL_SPAN), 1)
    mask_c = jj <= ii
    mask_p = jj >= ii
    mask_p0 = jnp.logical_and(mask_p, has_prev)

    def prev_refs(g, r, n, cur_ref, prv_ref, hs):
        if n > 0:
            return cur_ref[r, (n - 1) * DIL_SPAN:n * DIL_SPAN, hs]
        return prv_ref[r, :, hs]

    for hl in range(hpb):
        hs = slice(hl * DIL_HD, (hl + 1) * DIL_HD)
        for g in range(ng):
            d = DIL_RATES[g]
            blocks = [(r, n) for r in range(d) for n in range(span // (DIL_SPAN * d))]
            for i, (r, n) in enumerate(blocks):
                rows = slice(n * DIL_SPAN, (n + 1) * DIL_SPAN)
                q = q_refs[g][r, rows, hs]
                kp = prev_refs(g, r, n, kc_refs[g], kp_refs[g], hs)
                s_c = lax.dot_general(q, kc_refs[g][r, rows, hs], _NT, preferred_element_type=F32)
                s_p = lax.dot_general(q, kp, _NT, preferred_element_type=F32)
                sc_s[i] = jnp.where(mask_c, s_c, NEG)
                sp_s[i] = jnp.where(mask_p if n > 0 else mask_p0, s_p, NEG)
            for i, (r, n) in enumerate(blocks):
                s_c = sc_s[i]
                s_p = sp_s[i]
                m = jnp.max(jnp.maximum(s_c, s_p), axis=-1, keepdims=True)
                p_c = jnp.exp(s_c - m)
                p_p = jnp.exp(s_p - m)
                den = jnp.sum(p_c + p_p, axis=-1, keepdims=True)
                pc_s[i] = p_c.astype(BF16)
                pp_s[i] = p_p.astype(BF16)
                dst = slice(n * DIL_SPAN, (n + 1) * DIL_SPAN) if d == 1 else pl.ds(n * DIL_SPAN * d + r, DIL_SPAN, stride=d)
                m_s[g, dst, :] = jnp.broadcast_to(m, (DIL_SPAN, LANE))
                l_s[g, dst, :] = jnp.broadcast_to(den, (DIL_SPAN, LANE))
            for i, (r, n) in enumerate(blocks):
                rows = slice(n * DIL_SPAN, (n + 1) * DIL_SPAN)
                vp = prev_refs(g, r, n, vc_refs[g], vp_refs[g], hs)
                acc = jnp.dot(pc_s[i], vc_refs[g][r, rows, hs], preferred_element_type=F32)
                acc = acc + jnp.dot(pp_s[i], vp, preferred_element_type=F32)
                dst = rows if d == 1 else pl.ds(n * DIL_SPAN * d + r, DIL_SPAN, stride=d)
                acc_s[g, dst, :] = acc
        mr = 64
        for c in range(span // mr):
            rs = slice(c * mr, (c + 1) * mr)
            ms = [m_s[g, rs, :] for g in range(ng)]
            m_all = jnp.maximum(jnp.maximum(ms[0], ms[1]), ms[2])
            es = [jnp.exp(m - m_all) for m in ms]
            tot = es[0] * l_s[0, rs, :] + es[1] * l_s[1, rs, :] + es[2] * l_s[2, rs, :]
            num = es[0] * acc_s[0, rs, :] + es[1] * acc_s[1, rs, :] + es[2] * acc_s[2, rs, :]
            o_ref[rs, hs] = (num / tot * _silu(gate_ref[rs, hs].astype(F32))).astype(o_ref.dtype)


def _dilated_prompt(qkvs, gate, batch, seq):
    span, hpb = DIL_PROMPT_SPAN, DIL_PROMPT_HEADS
    wc = hpb * DIL_HD
    kcol, vcol = DIL_W // wc, 2 * DIL_W // wc
    nsp = seq // span
    views = [a.reshape(batch, d, seq // d, 3 * DIL_W) for a, d in zip(qkvs, DIL_RATES)]
    q_specs, kc_specs, kp_specs, vc_specs, vp_specs = [], [], [], [], []
    for d in DIL_RATES:
        cur = (None, d, span // d, wc)
        prv = (None, d, DIL_SPAN, wc)
        nblk = span // (d * DIL_SPAN)
        q_specs.append(pl.BlockSpec(cur, lambda b, s, h: (b, 0, s, h)))
        kc_specs.append(pl.BlockSpec(cur, lambda b, s, h: (b, 0, s, kcol + h)))
        vc_specs.append(pl.BlockSpec(cur, lambda b, s, h: (b, 0, s, vcol + h)))
        kp_specs.append(pl.BlockSpec(prv, lambda b, s, h, nblk=nblk: (b, 0, jnp.maximum(s * nblk - 1, 0), kcol + h)))
        vp_specs.append(pl.BlockSpec(prv, lambda b, s, h, nblk=nblk: (b, 0, jnp.maximum(s * nblk - 1, 0), vcol + h)))
    rows_spec = pl.BlockSpec((span, wc), lambda b, s, h: (b * nsp + s, h))
    return pl.pallas_call(
        functools.partial(_dil_prompt_body, span=span, hpb=hpb),
        grid=(batch, nsp, DIL_W // wc),
        in_specs=q_specs + kc_specs + kp_specs + vc_specs + vp_specs + [rows_spec],
        out_specs=rows_spec,
        out_shape=jax.ShapeDtypeStruct((batch * seq, DIL_W), BF16),
        scratch_shapes=([pltpu.VMEM((DIL_GROUPS, span, LANE), F32)] * 3
                        + [pltpu.VMEM((span // DIL_SPAN, DIL_SPAN, DIL_SPAN), F32)] * 2
                        + [pltpu.VMEM((span // DIL_SPAN, DIL_SPAN, DIL_SPAN), BF16)] * 2),
        compiler_params=_params(3),
        name="dilated_prompt",
    )(*(views * 5), gate)


def _dil_sample_body(qkv0_ref, qkv1_ref, qkv2_ref, gate_ref, c0_ref, c1_ref, c2_ref, o_ref, *, n_new):
    t = pl.program_id(1)

    @pl.when(t == 0)
    def _():
        o_ref[...] = jnp.zeros_like(o_ref)

    key_row = lax.broadcasted_iota(jnp.int32, (DIL_SPAN, 1, 1), 0)
    ms, dens, accs = [], [], []
    for g, (qkv_ref, c_ref) in enumerate(zip((qkv0_ref, qkv1_ref, qkv2_ref), (c0_ref, c1_ref, c2_ref))):
        k_heads = slice(DIL_HEADS, 2 * DIL_HEADS)
        v_heads = slice(2 * DIL_HEADS, 3 * DIL_HEADS)
        q = qkv_ref[t, 0:DIL_HEADS, :]
        s = jnp.sum(c_ref[:, 0] * q[None], axis=-1, keepdims=True)
        if DIL_RATES[g] == 1:
            s = jnp.where(key_row >= t, s, NEG)
            new_rows = [(tn, tn <= t) for tn in range(n_new)]
        else:
            new_rows = [(t, None)]
        s_new = []
        for tn, valid in new_rows:
            sn = jnp.sum(qkv_ref[tn, k_heads, :] * q, axis=-1, keepdims=True)
            s_new.append(sn if valid is None else jnp.where(valid, sn, NEG))
        m = jnp.max(s, axis=0)
        for sn in s_new:
            m = jnp.maximum(m, sn)
        p = jnp.exp(s - m[None])
        den = jnp.sum(p, axis=0)
        acc = jnp.sum(p * c_ref[:, 1], axis=0)
        for (tn, _), sn in zip(new_rows, s_new):
            pn = jnp.exp(sn - m)
            den = den + pn
            acc = acc + pn * qkv_ref[tn, v_heads, :]
        ms.append(m)
        dens.append(den)
        accs.append(acc)
    m_all = jnp.maximum(jnp.maximum(ms[0], ms[1]), ms[2])
    es = [jnp.exp(m - m_all) for m in ms]
    tot = es[0] * dens[0] + es[1] * dens[1] + es[2] * dens[2]
    merged = (es[0] * accs[0] + es[1] * accs[1] + es[2] * accs[2]) / tot
    o_ref[t] = merged * _silu(gate_ref[t])


def _dilated_sample(qkvs, gate, caches, layer, nb, t):
    rows = nb * SAMPLE_PAD
    views, specs = [], []
    for g, d in enumerate(DIL_RATES):
        c = caches[g]
        assert c.shape[2] == DIL_SPAN * d and t <= d * (1 if d > 1 else DIL_SPAN)
        views.append(c.reshape(c.shape[0], nb, DIL_SPAN, d, 2, DIL_HEADS, DIL_HD))
        blk = (None, None, DIL_SPAN, None, 2, DIL_HEADS, DIL_HD)
        if d == 1:
            specs.append(pl.BlockSpec(blk, lambda b, i: (layer, b, 0, 0, 0, 0, 0)))
        else:
            specs.append(pl.BlockSpec(blk, lambda b, i: (layer, b, 0, i, 0, 0, 0)))
    rows3 = pl.BlockSpec((SAMPLE_PAD, 3 * DIL_HEADS, DIL_HD), lambda b, i: (b, 0, 0))
    rows1 = pl.BlockSpec((SAMPLE_PAD, DIL_HEADS, DIL_HD), lambda b, i: (b, 0, 0))
    out = pl.pallas_call(
        functools.partial(_dil_sample_body, n_new=t),
        grid=(nb, t),
        in_specs=[rows3, rows3, rows3, rows1] + specs,
        out_specs=rows1,
        out_shape=jax.ShapeDtypeStruct((rows, DIL_HEADS, DIL_HD), F32),
        compiler_params=_params(2),
        name="dilated_sample",
    )(*(a.reshape(rows, 3 * DIL_HEADS, DIL_HD) for a in qkvs), gate.reshape(rows, DIL_HEADS, DIL_HD), *views)
    return out.reshape(rows, DIL_W)


def _cache_shift_body(a_ref, nxt_ref, new_ref, o_ref, *, t, tb):
    last = pl.program_id(1) == pl.num_programs(1) - 1
    o_ref[0:tb - t] = a_ref[t:tb]

    @pl.when(last)
    def _():
        o_ref[tb - t:tb] = new_ref[...]

    @pl.when(jnp.logical_not(last))
    def _():
        o_ref[tb - t:tb] = nxt_ref[...]


def _cache_shift(cache, new, layer, nb, t):
    wb = cache.shape[2]
    tail = cache.shape[3:]
    tb = min(wb, 256)
    assert wb % tb == 0 and tb % t == 0
    zeros = (0,) * len(tail)
    return pl.pallas_call(
        functools.partial(_cache_shift_body, t=t, tb=tb),
        grid=(nb, wb // tb),
        in_specs=[pl.BlockSpec((None, None, tb) + tail, lambda b, i: (layer, b, i) + zeros),
                  pl.BlockSpec((None, None, t) + tail,
                               lambda b, i: (layer, b, jnp.minimum((i + 1) * (tb // t), wb // t - 1)) + zeros),
                  pl.BlockSpec((None, t) + tail, lambda b, i: (b, 0) + zeros)],
        out_specs=pl.BlockSpec((None, tb) + tail, lambda b, i: (b, i) + zeros),
        out_shape=jax.ShapeDtypeStruct(cache.shape[1:], cache.dtype),
        compiler_params=_params(2),
        name="cache_shift",
    )(cache, cache, new)


def _gmlp_body(u_ref, v_ref, gate_ref, lng_ref, lnb_ref, wm_ref, bs_ref, o_ref, *vn_ref):
    v = v_ref[...].astype(F32)
    mu = jnp.mean(v, axis=-1, keepdims=True)
    xc = v - mu
    vn = xc * lax.rsqrt(jnp.mean(xc * xc, axis=-1, keepdims=True) + EPS) * lng_ref[...] + lnb_ref[...]
    if vn_ref:
        vn_ref[0][...] = vn
    vb = vn.astype(BF16)
    for g in range(GM_GROUPS):
        gs = slice(g * GM_GD, (g + 1) * GM_GD)
        mixed = jnp.dot(wm_ref[g], vb[:, gs], preferred_element_type=F32) + bs_ref[g]
        o_ref[:, gs] = (u_ref[:, gs].astype(F32) * mixed * _silu(gate_ref[:, gs].astype(F32))).astype(o_ref.dtype)


def _gmlp_core(uv, gate, ln_g3, ln_b3, layer, wm, bs, want_vn):
    rows = gate.shape[0]
    c = wm.shape[1]
    row = lambda i: (i, 0)
    out_specs = [pl.BlockSpec((c, GM_WIDTH), row)]
    out_shape = [jax.ShapeDtypeStruct((rows, GM_WIDTH), BF16)]
    if want_vn:
        out_specs.append(pl.BlockSpec((c, GM_WIDTH), row))
        out_shape.append(jax.ShapeDtypeStruct((rows, GM_WIDTH), F32))
    res = pl.pallas_call(
        _gmlp_body,
        grid=(rows // c,),
        in_specs=[pl.BlockSpec((c, GM_WIDTH), row),
                  pl.BlockSpec((c, GM_WIDTH), lambda i: (i, 1)),
                  pl.BlockSpec((c, GM_WIDTH), row),
                  pl.BlockSpec((None, 1, GM_WIDTH), lambda i: (layer, 0, 0)),
                  pl.BlockSpec((None, 1, GM_WIDTH), lambda i: (layer, 0, 0)),
                  pl.BlockSpec((GM_GROUPS, c, c), lambda i: (0, 0, 0)),
                  pl.BlockSpec((GM_GROUPS, c, 1), lambda i: (0, 0, 0))],
        out_specs=out_specs,
        out_shape=out_shape,
        compiler_params=_params(1),
        name="gmlp_core",
    )(uv, uv, gate, ln_g3, ln_b3, wm, bs)
    return res


class _Stream:
    def __init__(self, batch, t, t_pad, pos0, tm, act_dtype, tiled):
        self.batch, self.t, self.t_pad, self.tm, self.act, self.tiled = batch, t, t_pad, tm, act_dtype, tiled
        self.rows = batch * t_pad
        assert tiled or tm == self.rows
        self.row = (lambda i: i) if tiled else (lambda i: 0)
        pos = pos0 + jnp.arange(t_pad, dtype=jnp.int32)
        cos, sin = _rope_tables(pos, RET_DK // 2)
        self.ret_rope = (cos, sin)
        cos, sin = _rope_tables(pos, DIL_HD // 2)
        self.dil_rope = (jnp.concatenate([cos, cos], axis=-1), jnp.concatenate([-sin, sin], axis=-1))
        self.rope_tiles = max(t_pad // tm, 1)
        if t_pad < tm:
            rep = tm // t_pad
            self.ret_rope = tuple(jnp.tile(a, (rep, 1)) for a in self.ret_rope)
            self.dil_rope = tuple(jnp.tile(a, (rep, 1)) for a in self.dil_rope)

    def rope_specs(self):
        nt, row = self.rope_tiles, self.row
        return (pl.BlockSpec((self.tm, LANE), lambda j, i: (row(i) % nt, 0)),) * 2

    def tile_spec(self, tn):
        row = self.row
        return pl.BlockSpec((self.tm, tn), lambda j, i: (row(i), j))

    def dil_rope_residue_major(self, d):
        t = self.t_pad
        return tuple(a.reshape(t // d, d, LANE).transpose(1, 0, 2).reshape(t, LANE) for a in self.dil_rope)


class _Normed(NamedTuple):
    x: jax.Array
    ssq: object = None


def _proj(name, sts, hins, w3, layer, n0, n, tn, epilogue, extras=None, extra_specs=None):
    ops = []
    for k, (st, h) in enumerate(zip(sts, hins)):
        ops.append(_Operand(h.x, st.tm, st.tiled, epilogue, extras[k] if extras else (),
                            extra_specs[k] if extra_specs else (), (("tile", st.act),), h.ssq))
    return [r[0] for r in _matmul(w3, layer, n0, n, tn, ops, name)]


def _out_and_ple(sts, ys, w_out3, jl, xs, i, p3s, ple_w, ple_g3, ple_w_gate, next_g3, next_layer):
    tn = 512
    gain = lambda l: pl.BlockSpec((None, 1, tn), lambda j, m: (l, 0, j))
    stream_outs = (("tile", F32), ("tile", BF16), ("ssq",))
    ops = [_Operand(y, st.tm, st.tiled, _ep_residual, (x, ple_g3), (st.tile_spec(tn), gain(i)), stream_outs)
           for st, y, x in zip(sts, ys, xs)]
    mids = _matmul(w_out3, jl, 0, D_MODEL, tn, ops, "out_proj")
    ops = []
    for st, (x1, xg1, ssq1), p3 in zip(sts, mids, p3s):
        row = st.row
        extras = [x1, p3, ple_w]
        specs = [st.tile_spec(tn),
                 pl.BlockSpec((None, st.tm, PLE_DIM), lambda j, m, row=row: (i, row(m), 0)),
                 pl.BlockSpec((None, PLE_DIM, tn), lambda j, m: (i, 0, j))]
        if next_layer is not None:
            extras.append(next_g3)
            specs.append(gain(next_layer))
        ops.append(_Operand(xg1, st.tm, st.tiled, _ep_ple, tuple(extras), tuple(specs),
                            stream_outs if next_layer is not None else (("tile", F32),), ssq1))
    res = _matmul(ple_w_gate, i, 0, D_MODEL, tn, ops, "ple")
    x_new = [r[0] for r in res]
    h_next = [_Normed(r[1], r[2]) for r in res] if next_layer is not None else None
    return x_new, h_next


def _retention_proj(sts, hins, w_in, jl):
    tn = 1024
    ropes = [st.ret_rope for st in sts]
    rope_specs = [st.rope_specs() for st in sts]
    q = _proj("ret_q", sts, hins, w_in, jl, 0, RET_QK, tn, functools.partial(_ep_rope_ret, scale=1.0),
              ropes, rope_specs)
    k = _proj("ret_k", sts, hins, w_in, jl, RET_QK, RET_QK, tn,
              functools.partial(_ep_rope_ret, scale=RET_DK ** -0.5), ropes, rope_specs)
    vg = _proj("ret_vg", sts, hins, w_in, jl, 2 * RET_QK, 2 * RET_V, tn, _ep_plain)
    return q, k, vg


def _dilated_proj(sts, hs_by_rate, h_sample, w_in, jl):
    prm, smp = sts
    tn = 1024
    tiles = DIL_W // tn
    epilogues = ((0, functools.partial(_ep_rope_dil, scale=DIL_HD ** -0.5)),
                 (tiles, functools.partial(_ep_rope_dil, scale=1.0)),
                 (2 * tiles, _ep_plain))
    qkvs = []
    for g, d in enumerate(DIL_RATES):
        rope = prm.dil_rope if d == 1 else prm.dil_rope_residue_major(d)
        qkvs.append(_proj(f"dil_qkv_g{g}", sts, (hs_by_rate[g], h_sample), w_in, jl, g * 3 * DIL_W, 3 * DIL_W, tn,
                          epilogues, (rope, smp.dil_rope), (prm.rope_specs(), smp.rope_specs())))
    gates = _proj("dil_gate", sts, (hs_by_rate[0], h_sample), w_in, jl, DIL_GROUPS * 3 * DIL_W, DIL_W, tn, _ep_plain)
    return qkvs, gates


def _window_rows_prompt(qkv, st, d):
    kv = qkv.reshape(st.batch, d, st.t // d, 3, DIL_HEADS, DIL_HD)[:, :, st.t // d - DIL_SPAN:, 1:]
    kv = kv.transpose(0, 2, 1, 3, 4, 5).reshape(st.batch, DIL_SPAN * d, 2, DIL_HEADS, DIL_HD)
    return kv.astype(F32)


def _window_rows_sample(qkv, st):
    kv = qkv.reshape(st.batch, st.t_pad, 3, DIL_HEADS, DIL_HD)[:, :st.t, 1:]
    return kv.astype(F32)


def kernel(x_prompt, x_sample, state_ret, cache_win_g0, cache_win_g1, cache_win_g2, p_prompt, p_sample, norm_g,
           ret_w_in, ret_w_out, dil_w_in, dil_w_out, gm_w_in, gm_ln_g, gm_ln_b, gm_w_s, gm_b_s, gm_w_out, ple_w,
           ple_norm_g, ple_w_gate, final_norm_g):
    depth = norm_g.shape[0]
    bp, sp, _ = x_prompt.shape
    bs_, ts, _ = x_sample.shape
    assert ts <= SAMPLE_PAD and ts % GM_CHUNK != 0 and sp % (DIL_SPAN * max(DIL_RATES)) == 0
    caches = (cache_win_g0, cache_win_g1, cache_win_g2)

    prm = _Stream(bp, sp, sp, 0, 1024, BF16, True)
    smp = _Stream(bs_, ts, SAMPLE_PAD, PAST_LEN, bs_ * SAMPLE_PAD, F32, False)
    sts = (prm, smp)

    pad_t = ((0, 0), (0, SAMPLE_PAD - ts), (0, 0))
    xp = x_prompt.reshape(prm.rows, D_MODEL)
    xs = jnp.pad(x_sample, pad_t).reshape(smp.rows, D_MODEL)
    p3s = (p_prompt.astype(BF16).reshape(depth, prm.rows, PLE_DIM),
           jnp.pad(p_sample, ((0, 0),) + pad_t).astype(BF16).reshape(depth, smp.rows, PLE_DIM))

    norm_g3 = norm_g[:, None, :]
    ple_g3 = ple_norm_g[:, None, :]
    ln_g3 = gm_ln_g[:, None, :]
    ln_b3 = gm_ln_b[:, None, :]
    fin_g3 = final_norm_g[None, None, :]

    ret_p, ret_s, gm_s = [], [], []
    win_p = [[], [], []]
    win_s = [[], [], []]
    hins = [_Normed(_rmsnorm(xp, norm_g3, 0, BF16, 256)), _Normed(_rmsnorm(xs, norm_g3, 0, BF16, smp.rows))]
    for i in range(depth):
        kind, jl = i % 3, i // 3
        if kind == 0:
            q, k, vg = _retention_proj(sts, hins, ret_w_in, jl)
            yp, sp_new = _retention_prompt(q[0], k[0], vg[0], bp, sp)
            ys, ss_new = _retention_sample(q[1], k[1], vg[1], state_ret, jl, bs_, ts)
            ret_p.append(sp_new)
            ret_s.append(ss_new)
            w_out = ret_w_out
        elif kind == 1:
            hp_orders = [_Normed(a) for a in _rmsnorm_orders(xp, norm_g3, i, bp, sp, DIL_RATES[1:], 256)]
            h_sample = _Normed(_rmsnorm(xs, norm_g3, i, BF16, smp.rows))
            qkvs, gates = _dilated_proj(sts, hp_orders, h_sample, dil_w_in, jl)
            yp = _dilated_prompt([a[0] for a in qkvs], gates[0], bp, sp)
            ys = _dilated_sample([a[1] for a in qkvs], gates[1], caches, jl, bs_, ts).astype(BF16)
            for g in range(DIL_GROUPS):
                win_p[g].append(_window_rows_prompt(qkvs[g][0], prm, DIL_RATES[g]))
                win_s[g].append(_cache_shift(caches[g], _window_rows_sample(qkvs[g][1], smp), jl, bs_, ts))
            w_out = dil_w_out
        else:
            tn = 1024
            uv = _proj("gm_uv", sts, hins, gm_w_in, jl, 0, 2 * GM_WIDTH, tn, _ep_gelu)
            gate = _proj("gm_gate", sts, hins, gm_w_in, jl, 2 * GM_WIDTH, GM_WIDTH, tn, _ep_plain)
            wm_p = jnp.tril(gm_w_s[jl]).astype(BF16)
            bs_p = gm_b_s[jl][:, :, None]
            (yp,) = _gmlp_core(uv[0], gate[0], ln_g3, ln_b3, jl, wm_p, bs_p, False)
            wm_t = jnp.pad(jnp.tril(gm_w_s[jl][:, :ts, :ts]), ((0, 0), (0, SAMPLE_PAD - ts), (0, SAMPLE_PAD - ts)))
            wm_s = jnp.einsum("ab,gij->gaibj", jnp.eye(bs_, dtype=F32), wm_t).reshape(GM_GROUPS, smp.rows, smp.rows)
            bs_s = jnp.tile(jnp.pad(gm_b_s[jl][:, :ts], ((0, 0), (0, SAMPLE_PAD - ts))), (1, bs_))[:, :, None]
            ys, vn = _gmlp_core(uv[1], gate[1], ln_g3, ln_b3, jl, wm_s.astype(BF16), bs_s, True)
            gm_s.append(vn.reshape(bs_, SAMPLE_PAD, GM_WIDTH)[:, :ts])
            w_out = gm_w_out
        fold_next = i + 1 < depth and (i + 1) % 3 != 1
        (xp, xs), hins = _out_and_ple(sts, (yp, ys.astype(BF16)), w_out, jl, (xp, xs), i, p3s, ple_w, ple_g3,
                                      ple_w_gate, norm_g3, i + 1 if fold_next else None)

    y_prompt = _rmsnorm(xp, fin_g3, 0, F32, 256).reshape(bp, sp, D_MODEL)
    y_sample = _rmsnorm(xs, fin_g3, 0, F32, smp.rows).reshape(bs_, SAMPLE_PAD, D_MODEL)[:, :ts]
    return (y_prompt, y_sample,
            jnp.stack(ret_p), jnp.stack(ret_s),
            jnp.stack(win_p[0]), jnp.stack(win_s[0]),
            jnp.stack(win_p[1]), jnp.stack(win_s[1]),
            jnp.stack(win_p[2]), jnp.stack(win_s[2]),
            jnp.stack(gm_s))
```

```python
import functools
from typing import NamedTuple

import jax
import jax.numpy as jnp
import numpy as np
from jax import lax
from jax.experimental import pallas as pl
from jax.experimental.pallas import tpu as pltpu

F32 = jnp.float32
BF16 = jnp.bfloat16

D_MODEL = 2048
PAST_LEN = 16384
PLE_DIM = 256
ROPE_THETA = 10000.0
EPS = 1e-6
NEG = -1e30

RET_HEADS = 8
RET_DK = 256
RET_DV = 512
RET_QK = RET_HEADS * RET_DK
RET_V = RET_HEADS * RET_DV
RET_CHUNK = 128

DIL_RATES = (1, 4, 16)
DIL_GROUPS = 3
DIL_SPAN = 128
DIL_HEADS = 16
DIL_HD = 128
DIL_W = DIL_HEADS * DIL_HD

GM_WIDTH = 2 * D_MODEL
GM_GROUPS = 16
GM_GD = GM_WIDTH // GM_GROUPS
GM_CHUNK = 128

LANE = 128
MXU_COLS = 256
SAMPLE_PAD = 16
VMEM_LIMIT = 56 * 1024 * 1024


def _params(n_axes, vmem=VMEM_LIMIT):
    return pltpu.CompilerParams(dimension_semantics=("arbitrary",) * n_axes, vmem_limit_bytes=vmem)


def _silu(x):
    return x * jax.nn.sigmoid(x)


def _rms_body(x_ref, g_ref, o_ref):
    x = x_ref[...]
    y = x * lax.rsqrt(jnp.mean(x * x, axis=-1, keepdims=True) + EPS)
    o_ref[...] = (y * g_ref[...]).astype(o_ref.dtype)


def _rmsnorm(x, g3, layer, out_dtype, tr):
    r, d = x.shape
    return pl.pallas_call(
        _rms_body,
        grid=(r // tr,),
        in_specs=[pl.BlockSpec((tr, d), lambda i: (i, 0)),
                  pl.BlockSpec((None, 1, d), lambda i: (layer, 0, 0))],
        out_specs=pl.BlockSpec((tr, d), lambda i: (i, 0)),
        out_shape=jax.ShapeDtypeStruct((r, d), out_dtype),
        compiler_params=_params(1),
        name="rmsnorm",
    )(x, g3)


def _rms_orders_body(x_ref, g_ref, perm_ref, o_ref, *perm_out_refs, rates):
    x = x_ref[...]
    y = (x * lax.rsqrt(jnp.mean(x * x, axis=-1, keepdims=True) + EPS) * g_ref[...]).astype(o_ref.dtype)
    o_ref[...] = y
    tr = x.shape[0]
    for k, (p_ref, d) in enumerate(zip(perm_out_refs, rates)):
        yp = jnp.dot(perm_ref[k], y, preferred_element_type=F32).astype(o_ref.dtype)
        for r in range(d):
            p_ref[r] = yp[r * (tr // d):(r + 1) * (tr // d), :]


def _rmsnorm_orders(x, g3, layer, batch, seq, rates, tr):
    r, dm = x.shape
    nt = seq // tr
    rows = jnp.arange(tr, dtype=jnp.int32)
    perms = jnp.stack([(rows[None, :] == (rows % (tr // d) * d + rows // (tr // d))[:, None]) for d in rates])
    out_specs = [pl.BlockSpec((tr, dm), lambda i: (i, 0))]
    out_shape = [jax.ShapeDtypeStruct((r, dm), BF16)]
    for d in rates:
        out_specs.append(pl.BlockSpec((None, d, tr // d, dm), lambda i: (i // nt, 0, i % nt, 0)))
        out_shape.append(jax.ShapeDtypeStruct((batch, d, seq // d, dm), BF16))
    res = pl.pallas_call(
        functools.partial(_rms_orders_body, rates=rates),
        grid=(r // tr,),
        in_specs=[pl.BlockSpec((tr, dm), lambda i: (i, 0)),
                  pl.BlockSpec((None, 1, dm), lambda i: (layer, 0, 0)),
                  pl.BlockSpec((len(rates), tr, tr), lambda i: (0, 0, 0))],
        out_specs=out_specs,
        out_shape=out_shape,
        compiler_params=_params(1),
        name="rmsnorm_orders",
    )(x, g3, perms.astype(BF16))
    return [res[0]] + [a.reshape(r, dm) for a in res[1:]]


class _Operand(NamedTuple):
    x: jax.Array
    tm: int
    tiled: bool
    epilogue: object
    extras: tuple = ()
    extra_specs: tuple = ()
    outs: tuple = (("tile", F32),)
    ssq: object = None


def _mm_body(w_ref, *refs, layout):
    wb_ref = refs[-1]
    j = pl.program_id(0)
    i = pl.program_id(1)

    @pl.when(i == 0)
    def _():
        wb_ref[...] = w_ref[...].astype(BF16)

    def stream(x_ref, ssq_ref, extra, outs, epilogue):
        rinv = None
        if ssq_ref is not None:
            mean_sq = jnp.sum(jnp.sum(ssq_ref[...], axis=0), axis=-1, keepdims=True) / x_ref.shape[1]
            rinv = lax.rsqrt(mean_sq + EPS)

        def run(ep):
            for c in range(wb_ref.shape[1] // MXU_COLS):
                cs = slice(c * MXU_COLS, (c + 1) * MXU_COLS)
                acc = jnp.dot(x_ref[...], wb_ref[:, cs], preferred_element_type=F32)
                ep(acc if rinv is None else acc * rinv, cs, extra, outs)

        if callable(epilogue):
            run(epilogue)
        else:
            starts = [s for s, _ in epilogue] + [None]
            for (lo, ep), hi in zip(epilogue, starts[1:]):
                pred = j >= lo if hi is None else jnp.logical_and(j >= lo, j < hi)
                pl.when(pred)(functools.partial(run, ep))

    n_in = sum(n for n, _, _, _, _ in layout)
    pos, opos = 0, n_in
    for n, n_out, has_ssq, tiled, epilogue in layout:
        x_ref = refs[pos]
        ssq_ref = refs[pos + 1] if has_ssq else None
        extra = refs[pos + 1 + has_ssq:pos + n]
        outs = refs[opos:opos + n_out]
        pos, opos = pos + n, opos + n_out
        body = functools.partial(stream, x_ref, ssq_ref, extra, outs, epilogue)
        if tiled:
            body()
        else:
            pl.when(i == 0)(body)


def _matmul(w3, layer, n0, n, tn, operands, name):
    k = w3.shape[1]
    j0 = n0 // tn
    steps = [op.x.shape[0] // op.tm for op in operands if op.tiled]
    in_specs = [pl.BlockSpec((None, k, tn), lambda j, i: (layer, 0, j + j0))]
    args, out_specs, out_shape, layout = [w3], [], [], []
    for op in operands:
        rows = op.x.shape[0]
        row = (lambda i: i) if op.tiled else (lambda i: 0)
        in_specs.append(pl.BlockSpec((op.tm, k), lambda j, i, row=row: (row(i), 0)))
        args.append(op.x)
        if op.ssq is not None:
            in_specs.append(pl.BlockSpec((op.ssq.shape[0], op.tm, LANE), lambda j, i, row=row: (0, row(i), 0)))
            args.append(op.ssq)
        in_specs += list(op.extra_specs)
        args += list(op.extras)
        for out in op.outs:
            if out[0] == "tile":
                out_specs.append(pl.BlockSpec((op.tm, tn), lambda j, i, row=row: (row(i), j)))
                out_shape.append(jax.ShapeDtypeStruct((rows, n), out[1]))
            else:
                out_specs.append(pl.BlockSpec((None, op.tm, LANE), lambda j, i, row=row: (j, row(i), 0)))
                out_shape.append(jax.ShapeDtypeStruct((n // tn, rows, LANE), F32))
        layout.append((1 + (op.ssq is not None) + len(op.extras), len(op.outs), op.ssq is not None, op.tiled,
                       op.epilogue))
    res = pl.pallas_call(
        functools.partial(_mm_body, layout=tuple(layout)),
        grid=(n // tn, steps[0]),
        in_specs=in_specs,
        out_specs=out_specs,
        out_shape=out_shape,
        scratch_shapes=[pltpu.VMEM((k, tn), BF16)],
        compiler_params=_params(2),
        name=name,
    )(*args)
    groups, pos = [], 0
    for op in operands:
        groups.append(tuple(res[pos:pos + len(op.outs)]))
        pos += len(op.outs)
    return groups


def _ep_plain(acc, cs, extra, outs):
    outs[0][:, cs] = acc.astype(outs[0].dtype)


def _ep_gelu(acc, cs, extra, outs):
    outs[0][:, cs] = jax.nn.gelu(acc).astype(outs[0].dtype)


def _emit_stream(x_new, cs, g_ref, outs):
    outs[0][:, cs] = x_new
    if len(outs) == 1:
        return
    outs[1][:, cs] = (x_new * g_ref[:, cs]).astype(outs[1].dtype)
    sq = x_new * x_new
    part = sq[:, :LANE] + sq[:, LANE:]
    if cs.start == 0:
        outs[2][...] = part
    else:
        outs[2][...] += part


def _ep_residual(acc, cs, extra, outs):
    _emit_stream(extra[0][:, cs] + acc, cs, extra[1] if len(extra) > 1 else None, outs)


def _ep_rope_ret(acc, cs, extra, outs, *, scale):
    cos = extra[0][...]
    sin = extra[1][...]
    o = outs[0]
    half = RET_DK // 2
    for h in range(acc.shape[1] // RET_DK):
        x1 = acc[:, h * RET_DK:h * RET_DK + half]
        x2 = acc[:, h * RET_DK + half:(h + 1) * RET_DK]
        c0 = cs.start + h * RET_DK
        y1 = x1 * cos - x2 * sin
        y2 = x2 * cos + x1 * sin
        o[:, c0:c0 + half] = (y1 if scale == 1.0 else y1 * scale).astype(o.dtype)
        o[:, c0 + half:c0 + RET_DK] = (y2 if scale == 1.0 else y2 * scale).astype(o.dtype)


def _ep_rope_dil(acc, cs, extra, outs, *, scale):
    cos2 = extra[0][...]
    sin2 = extra[1][...]
    o = outs[0]
    for h in range(acc.shape[1] // DIL_HD):
        x = acc[:, h * DIL_HD:(h + 1) * DIL_HD]
        swapped = pltpu.roll(x, DIL_HD // 2, axis=1)
        c0 = cs.start + h * DIL_HD
        y = x * cos2 + swapped * sin2
        o[:, c0:c0 + DIL_HD] = (y if scale == 1.0 else y * scale).astype(o.dtype)


def _ep_ple(acc, cs, extra, outs):
    pe = jnp.dot(extra[1][...], extra[2][:, cs].astype(BF16), preferred_element_type=F32)
    _emit_stream(extra[0][:, cs] + pe * jax.nn.sigmoid(acc), cs, extra[3] if len(extra) > 3 else None, outs)


def _rope_tables(pos, half):
    inv = ROPE_THETA ** (-np.arange(half, dtype=np.float64) / half)
    ang = pos.astype(np.float64)[:, None] * inv[None, :]
    return np.cos(ang).astype(np.float32), np.sin(ang).astype(np.float32)


def _ret_tables(c, c_pad):
    lg = jnp.log1p(-jnp.exp2(-5.0 - jnp.arange(RET_HEADS, dtype=F32)))
    i = jnp.arange(c_pad, dtype=F32)
    diff = i[:, None] - i[None, :]
    dmat = jnp.where(diff[None] >= 0, jnp.exp(lg[:, None, None] * jnp.maximum(diff, 0.0)[None]), 0.0)
    xi = jnp.exp(lg[:, None] * (i[None, :] + 1.0))[:, :, None]
    zeta = jnp.exp(lg[:, None] * (c - 1.0 - i[None, :]))[:, :, None]
    gc = jnp.broadcast_to(jnp.exp(lg * c)[:, None, None], (RET_HEADS, 1, RET_DV))
    return dmat, xi, zeta, gc


def _ret_finish(o, g):
    on = o * lax.rsqrt(jnp.mean(o * o, axis=-1, keepdims=True) + EPS)
    return on * _silu(g)


_NT = (((1,), (1,)), ((), ()))
_TN = (((0,), (0,)), ((), ()))


def _ret_prompt_body(q_ref, k_ref, v_ref, g_ref, dmat_ref, xi_ref, zeta_ref, gc_ref, o_ref, s_ref,
                     sc_s, kv_s, sb_s, *, c, cb):
    @pl.when(pl.program_id(2) == 0)
    def _():
        s_ref[...] = jnp.zeros_like(s_ref)

    dmat = dmat_ref[...]
    xi = xi_ref[...]
    zeta = zeta_ref[...]
    gc = gc_ref[...]
    for i in range(cb):
        rows = slice(i * c, (i + 1) * c)
        k = k_ref[rows, :]
        sc = lax.dot_general(q_ref[rows, :], k, _NT, preferred_element_type=F32) * dmat
        sc_s[i] = sc.astype(BF16)
        kz = (k.astype(F32) * zeta).astype(BF16)
        kv_s[i] = lax.dot_general(kz, v_ref[rows, :], _TN, preferred_element_type=F32)
    for i in range(cb):
        s = s_ref[...]
        sb_s[i] = s.astype(BF16)
        s_ref[...] = s * gc + kv_s[i]
    for i in range(cb):
        rows = slice(i * c, (i + 1) * c)
        o = jnp.dot(sc_s[i], v_ref[rows, :], preferred_element_type=F32)
        o = o + jnp.dot(q_ref[rows, :], sb_s[i], preferred_element_type=F32) * xi
        o_ref[rows, :] = _ret_finish(o, g_ref[rows, :].astype(F32)).astype(o_ref.dtype)


def _retention_prompt(q, k, vg, batch, seq):
    c = RET_CHUNK
    cb = 8
    rb = c * cb
    ncb = seq // rb
    dmat, xi, zeta, gc = _ret_tables(c, c)
    row = lambda b, h, i: (b * ncb + i, h)
    tab = lambda b, h, i: (h, 0, 0)
    return pl.pallas_call(
        functools.partial(_ret_prompt_body, c=c, cb=cb),
        grid=(batch, RET_HEADS, ncb),
        in_specs=[pl.BlockSpec((rb, RET_DK), row),
                  pl.BlockSpec((rb, RET_DK), row),
                  pl.BlockSpec((rb, RET_DV), row),
                  pl.BlockSpec((rb, RET_DV), lambda b, h, i: (b * ncb + i, RET_HEADS + h)),
                  pl.BlockSpec((None, c, c), tab),
                  pl.BlockSpec((None, c, 1), tab),
                  pl.BlockSpec((None, c, 1), tab),
                  pl.BlockSpec((None, 1, RET_DV), tab)],
        out_specs=[pl.BlockSpec((rb, RET_DV), row),
                   pl.BlockSpec((None, None, RET_DK, RET_DV), lambda b, h, i: (b, h, 0, 0))],
        out_shape=[jax.ShapeDtypeStruct((batch * seq, RET_V), BF16),
                   jax.ShapeDtypeStruct((batch, RET_HEADS, RET_DK, RET_DV), F32)],
        scratch_shapes=[pltpu.VMEM((cb, c, c), BF16), pltpu.VMEM((cb, RET_DK, RET_DV), F32),
                        pltpu.VMEM((cb, RET_DK, RET_DV), BF16)],
        compiler_params=_params(3),
        name="retention_prompt",
    )(q, k, vg, vg, dmat, xi, zeta, gc)


def _ret_sample_body(q_ref, k_ref, v_ref, g_ref, s0_ref, dmat_ref, xi_ref, zeta_ref, gc_ref, o_ref, s_ref, *, nb):
    q = q_ref[...].astype(BF16)
    k = k_ref[...]
    v = v_ref[...].astype(BF16)
    xi = xi_ref[...]
    gc = gc_ref[...]
    sc = lax.dot_general(q, k.astype(BF16), _NT, preferred_element_type=F32) * dmat_ref[...]
    o = jnp.dot(sc.astype(BF16), v, preferred_element_type=F32)
    kz = (k * zeta_ref[...]).astype(BF16)
    row_batch = lax.broadcasted_iota(jnp.int32, kz.shape, 0) // SAMPLE_PAD
    cross = []
    for b in range(nb):
        s0 = s0_ref[b]
        qb = q[b * SAMPLE_PAD:(b + 1) * SAMPLE_PAD, :]
        cross.append(jnp.dot(qb, s0.astype(BF16), preferred_element_type=F32))
        kzb = jnp.where(row_batch == b, kz, jnp.zeros_like(kz))
        s_ref[b] = s0 * gc + lax.dot_general(kzb, v, _TN, preferred_element_type=F32)
    o = o + jnp.concatenate(cross, axis=0) * xi
    o_ref[...] = _ret_finish(o, g_ref[...]).astype(o_ref.dtype)


def _retention_sample(q, k, vg, state, layer, nb, t):
    rows = nb * SAMPLE_PAD
    dmat, xi, zeta, gc = _ret_tables(t, SAMPLE_PAD)
    eye = jnp.eye(nb, dtype=F32)
    dbig = jnp.einsum("ab,hij->haibj", eye, dmat).reshape(RET_HEADS, rows, rows)
    xib = jnp.tile(xi, (1, nb, 1))
    zetab = jnp.tile(zeta, (1, nb, 1))
    col = lambda h: (0, h)
    tab = lambda h: (h, 0, 0)
    return pl.pallas_call(
        functools.partial(_ret_sample_body, nb=nb),
        grid=(RET_HEADS,),
        in_specs=[pl.BlockSpec((rows, RET_DK), col),
                  pl.BlockSpec((rows, RET_DK), col),
                  pl.BlockSpec((rows, RET_DV), col),
                  pl.BlockSpec((rows, RET_DV), lambda h: (0, RET_HEADS + h)),
                  pl.BlockSpec((None, nb, None, RET_DK, RET_DV), lambda h: (layer, 0, h, 0, 0)),
                  pl.BlockSpec((None, rows, rows), tab),
                  pl.BlockSpec((None, rows, 1), tab),
                  pl.BlockSpec((None, rows, 1), tab),
                  pl.BlockSpec((None, 1, RET_DV), tab)],
        out_specs=[pl.BlockSpec((rows, RET_DV), col),
                   pl.BlockSpec((nb, None, RET_DK, RET_DV), lambda h: (0, h, 0, 0))],
        out_shape=[jax.ShapeDtypeStruct((rows, RET_V), BF16),
                   jax.ShapeDtypeStruct((nb, RET_HEADS, RET_DK, RET_DV), F32)],
        compiler_params=_params(1),
        name="retention_sample",
    )(q, k, vg, vg, state, dbig, xib, zetab, gc)


DIL_PROMPT_SPAN = DIL_SPAN * max(DIL_RATES)
DIL_PROMPT_HEADS = 2


def _dil_prompt_body(*refs, span, hpb):
    ng = DIL_GROUPS
    q_refs, kc_refs, kp_refs = refs[0:ng], refs[ng:2 * ng], refs[2 * ng:3 * ng]
    vc_refs, vp_refs = refs[3 * ng:4 * ng], refs[4 * ng:5 * ng]
    gate_ref, o_ref, acc_s, m_s, l_s, sc_s, sp_s, pc_s, pp_s = refs[5 * ng:]
    has_prev = pl.program_id(1) > 0
    ii = lax.broadcasted_iota(jnp.int32, (DIL_SPAN, DIL_SPAN), 0)
    jj = lax.broadcasted_iota(jnp.int32, (DIL_SPAN, DIL_SPAN), 1)
    mask_c = jj <= ii
    mask_p = jj >= ii
    mask_p0 = jnp.logical_and(mask_p, has_prev)

    def prev_refs(g, r, n, cur_ref, prv_ref, hs):
        if n > 0:
            return cur_ref[r, (n - 1) * DIL_SPAN:n * DIL_SPAN, hs]
        return prv_ref[r, :, hs]

    for hl in range(hpb):
        hs = slice(hl * DIL_HD, (hl + 1) * DIL_HD)
        for g in range(ng):
            d = DIL_RATES[g]
            blocks = [(r, n) for r in range(d) for n in range(span // (DIL_SPAN * d))]
            for i, (r, n) in enumerate(blocks):
                rows = slice(n * DIL_SPAN, (n + 1) * DIL_SPAN)
                q = q_refs[g][r, rows, hs]
                kp = prev_refs(g, r, n, kc_refs[g], kp_refs[g], hs)
                s_c = lax.dot_general(q, kc_refs[g][r, rows, hs], _NT, preferred_element_type=F32)
                s_p = lax.dot_general(q, kp, _NT, preferred_element_type=F32)
                sc_s[i] = jnp.where(mask_c, s_c, NEG)
                sp_s[i] = jnp.where(mask_p if n > 0 else mask_p0, s_p, NEG)
            for i, (r, n) in enumerate(blocks):
                s_c = sc_s[i]
                s_p = sp_s[i]
                m = jnp.max(jnp.maximum(s_c, s_p), axis=-1, keepdims=True)
                p_c = jnp.exp(s_c - m)
                p_p = jnp.exp(s_p - m)
                den = jnp.sum(p_c + p_p, axis=-1, keepdims=True)
                pc_s[i] = p_c.astype(BF16)
                pp_s[i] = p_p.astype(BF16)
                dst = slice(n * DIL_SPAN, (n + 1) * DIL_SPAN) if d == 1 else pl.ds(n * DIL_SPAN * d + r, DIL_SPAN, stride=d)
                m_s[g, dst, :] = jnp.broadcast_to(m, (DIL_SPAN, LANE))
                l_s[g, dst, :] = jnp.broadcast_to(den, (DIL_SPAN, LANE))
            for i, (r, n) in enumerate(blocks):
                rows = slice(n * DIL_SPAN, (n + 1) * DIL_SPAN)
                vp = prev_refs(g, r, n, vc_refs[g], vp_refs[g], hs)
                acc = jnp.dot(pc_s[i], vc_refs[g][r, rows, hs], preferred_element_type=F32)
                acc = acc + jnp.dot(pp_s[i], vp, preferred_element_type=F32)
                dst = rows if d == 1 else pl.ds(n * DIL_SPAN * d + r, DIL_SPAN, stride=d)
                acc_s[g, dst, :] = acc
        mr = 64
        for c in range(span // mr):
            rs = slice(c * mr, (c + 1) * mr)
            ms = [m_s[g, rs, :] for g in range(ng)]
            m_all = jnp.maximum(jnp.maximum(ms[0], ms[1]), ms[2])
            es = [jnp.exp(m - m_all) for m in ms]
            tot = es[0] * l_s[0, rs, :] + es[1] * l_s[1, rs, :] + es[2] * l_s[2, rs, :]
            num = es[0] * acc_s[0, rs, :] + es[1] * acc_s[1, rs, :] + es[2] * acc_s[2, rs, :]
            o_ref[rs, hs] = (num / tot * _silu(gate_ref[rs, hs].astype(F32))).astype(o_ref.dtype)


def _dilated_prompt(qkvs, gate, batch, seq):
    span, hpb = DIL_PROMPT_SPAN, DIL_PROMPT_HEADS
    wc = hpb * DIL_HD
    kcol, vcol = DIL_W // wc, 2 * DIL_W // wc
    nsp = seq // span
    views = [a.reshape(batch, d, seq // d, 3 * DIL_W) for a, d in zip(qkvs, DIL_RATES)]
    q_specs, kc_specs, kp_specs, vc_specs, vp_specs = [], [], [], [], []
    for d in DIL_RATES:
        cur = (None, d, span // d, wc)
        prv = (None, d, DIL_SPAN, wc)
        nblk = span // (d * DIL_SPAN)
        q_specs.append(pl.BlockSpec(cur, lambda b, s, h: (b, 0, s, h)))
        kc_specs.append(pl.BlockSpec(cur, lambda b, s, h: (b, 0, s, kcol + h)))
        vc_specs.append(pl.BlockSpec(cur, lambda b, s, h: (b, 0, s, vcol + h)))
        kp_specs.append(pl.BlockSpec(prv, lambda b, s, h, nblk=nblk: (b, 0, jnp.maximum(s * nblk - 1, 0), kcol + h)))
        vp_specs.append(pl.BlockSpec(prv, lambda b, s, h, nblk=nblk: (b, 0, jnp.maximum(s * nblk - 1, 0), vcol + h)))
    rows_spec = pl.BlockSpec((span, wc), lambda b, s, h: (b * nsp + s, h))
    return pl.pallas_call(
        functools.partial(_dil_prompt_body, span=span, hpb=hpb),
        grid=(batch, nsp, DIL_W // wc),
        in_specs=q_specs + kc_specs + kp_specs + vc_specs + vp_specs + [rows_spec],
        out_specs=rows_spec,
        out_shape=jax.ShapeDtypeStruct((batch * seq, DIL_W), BF16),
        scratch_shapes=([pltpu.VMEM((DIL_GROUPS, span, LANE), F32)] * 3
                        + [pltpu.VMEM((span // DIL_SPAN, DIL_SPAN, DIL_SPAN), F32)] * 2
                        + [pltpu.VMEM((span // DIL_SPAN, DIL_SPAN, DIL_SPAN), BF16)] * 2),
        compiler_params=_params(3),
        name="dilated_prompt",
    )(*(views * 5), gate)


def _dil_sample_body(qkv0_ref, qkv1_ref, qkv2_ref, gate_ref, c0_ref, c1_ref, c2_ref, o_ref, *, n_new):
    t = pl.program_id(1)

    @pl.when(t == 0)
    def _():
        o_ref[...] = jnp.zeros_like(o_ref)

    key_row = lax.broadcasted_iota(jnp.int32, (DIL_SPAN, 1, 1), 0)
    ms, dens, accs = [], [], []
    for g, (qkv_ref, c_ref) in enumerate(zip((qkv0_ref, qkv1_ref, qkv2_ref), (c0_ref, c1_ref, c2_ref))):
        k_heads = slice(DIL_HEADS, 2 * DIL_HEADS)
        v_heads = slice(2 * DIL_HEADS, 3 * DIL_HEADS)
        q = qkv_ref[t, 0:DIL_HEADS, :]
        s = jnp.sum(c_ref[:, 0] * q[None], axis=-1, keepdims=True)
        if DIL_RATES[g] == 1:
            s = jnp.where(key_row >= t, s, NEG)
            new_rows = [(tn, tn <= t) for tn in range(n_new)]
        else:
            new_rows = [(t, None)]
        s_new = []
        for tn, valid in new_rows:
            sn = jnp.sum(qkv_ref[tn, k_heads, :] * q, axis=-1, keepdims=True)
            s_new.append(sn if valid is None else jnp.where(valid, sn, NEG))
        m = jnp.max(s, axis=0)
        for sn in s_new:
            m = jnp.maximum(m, sn)
        p = jnp.exp(s - m[None])
        den = jnp.sum(p, axis=0)
        acc = jnp.sum(p * c_ref[:, 1], axis=0)
        for (tn, _), sn in zip(new_rows, s_new):
            pn = jnp.exp(sn - m)
            den = den + pn
            acc = acc + pn * qkv_ref[tn, v_heads, :]
        ms.append(m)
        dens.append(den)
        accs.append(acc)
    m_all = jnp.maximum(jnp.maximum(ms[0], ms[1]), ms[2])
    es = [jnp.exp(m - m_all) for m in ms]
    tot = es[0] * dens[0] + es[1] * dens[1] + es[2] * dens[2]
    merged = (es[0] * accs[0] + es[1] * accs[1] + es[2] * accs[2]) / tot
    o_ref[t] = merged * _silu(gate_ref[t])


def _dilated_sample(qkvs, gate, caches, layer, nb, t):
    rows = nb * SAMPLE_PAD
    views, specs = [], []
    for g, d in enumerate(DIL_RATES):
        c = caches[g]
        assert c.shape[2] == DIL_SPAN * d and t <= d * (1 if d > 1 else DIL_SPAN)
        views.append(c.reshape(c.shape[0], nb, DIL_SPAN, d, 2, DIL_HEADS, DIL_HD))
        blk = (None, None, DIL_SPAN, None, 2, DIL_HEADS, DIL_HD)
        if d == 1:
            specs.append(pl.BlockSpec(blk, lambda b, i: (layer, b, 0, 0, 0, 0, 0)))
        else:
            specs.append(pl.BlockSpec(blk, lambda b, i: (layer, b, 0, i, 0, 0, 0)))
    rows3 = pl.BlockSpec((SAMPLE_PAD, 3 * DIL_HEADS, DIL_HD), lambda b, i: (b, 0, 0))
    rows1 = pl.BlockSpec((SAMPLE_PAD, DIL_HEADS, DIL_HD), lambda b, i: (b, 0, 0))
    out = pl.pallas_call(
        functools.partial(_dil_sample_body, n_new=t),
        grid=(nb, t),
        in_specs=[rows3, rows3, rows3, rows1] + specs,
        out_specs=rows1,
        out_shape=jax.ShapeDtypeStruct((rows, DIL_HEADS, DIL_HD), F32),
        compiler_params=_params(2),
        name="dilated_sample",
    )(*(a.reshape(rows, 3 * DIL_HEADS, DIL_HD) for a in qkvs), gate.reshape(rows, DIL_HEADS, DIL_HD), *views)
    return out.reshape(rows, DIL_W)


def _cache_shift_body(a_ref, nxt_ref, new_ref, o_ref, *, t, tb):
    last = pl.program_id(1) == pl.num_programs(1) - 1
    o_ref[0:tb - t] = a_ref[t:tb]

    @pl.when(last)
    def _():
        o_ref[tb - t:tb] = new_ref[...]

    @pl.when(jnp.logical_not(last))
    def _():
        o_ref[tb - t:tb] = nxt_ref[...]


def _cache_shift(cache, new, layer, nb, t):
    wb = cache.shape[2]
    tail = cache.shape[3:]
    tb = min(wb, 256)
    assert wb % tb == 0 and tb % t == 0
    zeros = (0,) * len(tail)
    return pl.pallas_call(
        functools.partial(_cache_shift_body, t=t, tb=tb),
        grid=(nb, wb // tb),
        in_specs=[pl.BlockSpec((None, None, tb) + tail, lambda b, i: (layer, b, i) + zeros),
                  pl.BlockSpec((None, None, t) + tail,
                               lambda b, i: (layer, b, jnp.minimum((i + 1) * (tb // t), wb // t - 1)) + zeros),
                  pl.BlockSpec((None, t) + tail, lambda b, i: (b, 0) + zeros)],
        out_specs=pl.BlockSpec((None, tb) + tail, lambda b, i: (b, i) + zeros),
        out_shape=jax.ShapeDtypeStruct(cache.shape[1:], cache.dtype),
        compiler_params=_params(2),
        name="cache_shift",
    )(cache, cache, new)


def _gmlp_body(u_ref, v_ref, gate_ref, lng_ref, lnb_ref, wm_ref, bs_ref, o_ref, *vn_ref):
    v = v_ref[...].astype(F32)
    mu = jnp.mean(v, axis=-1, keepdims=True)
    xc = v - mu
    vn = xc * lax.rsqrt(jnp.mean(xc * xc, axis=-1, keepdims=True) + EPS) * lng_ref[...] + lnb_ref[...]
    if vn_ref:
        vn_ref[0][...] = vn
    vb = vn.astype(BF16)
    for g in range(GM_GROUPS):
        gs = slice(g * GM_GD, (g + 1) * GM_GD)
        mixed = jnp.dot(wm_ref[g], vb[:, gs], preferred_element_type=F32) + bs_ref[g]
        o_ref[:, gs] = (u_ref[:, gs].astype(F32) * mixed * _silu(gate_ref[:, gs].astype(F32))).astype(o_ref.dtype)


def _gmlp_core(uv, gate, ln_g3, ln_b3, layer, wm, bs, want_vn):
    rows = gate.shape[0]
    c = wm.shape[1]
    row = lambda i: (i, 0)
    out_specs = [pl.BlockSpec((c, GM_WIDTH), row)]
    out_shape = [jax.ShapeDtypeStruct((rows, GM_WIDTH), BF16)]
    if want_vn:
        out_specs.append(pl.BlockSpec((c, GM_WIDTH), row))
        out_shape.append(jax.ShapeDtypeStruct((rows, GM_WIDTH), F32))
    res = pl.pallas_call(
        _gmlp_body,
        grid=(rows // c,),
        in_specs=[pl.BlockSpec((c, GM_WIDTH), row),
                  pl.BlockSpec((c, GM_WIDTH), lambda i: (i, 1)),
                  pl.BlockSpec((c, GM_WIDTH), row),
                  pl.BlockSpec((None, 1, GM_WIDTH), lambda i: (layer, 0, 0)),
                  pl.BlockSpec((None, 1, GM_WIDTH), lambda i: (layer, 0, 0)),
                  pl.BlockSpec((GM_GROUPS, c, c), lambda i: (0, 0, 0)),
                  pl.BlockSpec((GM_GROUPS, c, 1), lambda i: (0, 0, 0))],
        out_specs=out_specs,
        out_shape=out_shape,
        compiler_params=_params(1),
        name="gmlp_core",
    )(uv, uv, gate, ln_g3, ln_b3, wm, bs)
    return res


class _Stream:
    def __init__(self, batch, t, t_pad, pos0, tm, tm_proj, act_dtype, tiled):
        self.batch, self.t, self.t_pad, self.act, self.tiled = batch, t, t_pad, act_dtype, tiled
        self.tm, self.tm_proj = tm, tm_proj
        self.rows = batch * t_pad
        assert tiled or tm == tm_proj == self.rows
        self.row = (lambda i: i) if tiled else (lambda i: 0)
        pos = pos0 + np.arange(t_pad)
        cos, sin = _rope_tables(pos, RET_DK // 2)
        self.ret_rope = (cos, sin)
        cos, sin = _rope_tables(pos, DIL_HD // 2)
        self.dil_rope = (np.concatenate([cos, cos], axis=-1), np.concatenate([-sin, sin], axis=-1))
        if t_pad < tm_proj:
            rep = tm_proj // t_pad
            self.ret_rope = tuple(np.tile(a, (rep, 1)) for a in self.ret_rope)
            self.dil_rope = tuple(np.tile(a, (rep, 1)) for a in self.dil_rope)

    def proj_tile(self, h):
        return self.tm_proj if h.ssq is None else self.tm

    def rope_specs(self, tm):
        nt, row = max(self.t_pad // tm, 1), self.row
        return (pl.BlockSpec((tm, LANE), lambda j, i: (row(i) % nt, 0)),) * 2

    def tile_spec(self, tn):
        row = self.row
        return pl.BlockSpec((self.tm, tn), lambda j, i: (row(i), j))

    def dil_rope_residue_major(self, d):
        t = self.t_pad
        return tuple(a.reshape(t // d, d, LANE).transpose(1, 0, 2).reshape(t, LANE) for a in self.dil_rope)


class _Normed(NamedTuple):
    x: jax.Array
    ssq: object = None


def _proj(name, sts, hins, w3, layer, n0, n, tn, epilogue, extras=None, extra_specs=None):
    ops = []
    for k, (st, h) in enumerate(zip(sts, hins)):
        ops.append(_Operand(h.x, st.proj_tile(h), st.tiled, epilogue, extras[k] if extras else (),
                            extra_specs[k] if extra_specs else (), (("tile", st.act),), h.ssq))
    return [r[0] for r in _matmul(w3, layer, n0, n, tn, ops, name)]


def _out_and_ple(sts, ys, w_out3, jl, xs, i, p3s, ple_w, ple_g3, ple_w_gate, next_g3, next_layer):
    tn = 512
    gain = lambda l: pl.BlockSpec((None, 1, tn), lambda j, m: (l, 0, j))
    stream_outs = (("tile", F32), ("tile", BF16), ("ssq",))
    ops = [_Operand(y, st.tm, st.tiled, _ep_residual, (x, ple_g3), (st.tile_spec(tn), gain(i)), stream_outs)
           for st, y, x in zip(sts, ys, xs)]
    mids = _matmul(w_out3, jl, 0, D_MODEL, tn, ops, "out_proj")
    ops = []
    for st, (x1, xg1, ssq1), p3 in zip(sts, mids, p3s):
        row = st.row
        extras = [x1, p3, ple_w]
        specs = [st.tile_spec(tn),
                 pl.BlockSpec((None, st.tm, PLE_DIM), lambda j, m, row=row: (i, row(m), 0)),
                 pl.BlockSpec((None, PLE_DIM, tn), lambda j, m: (i, 0, j))]
        if next_layer is not None:
            extras.append(next_g3)
            specs.append(gain(next_layer))
        ops.append(_Operand(xg1, st.tm, st.tiled, _ep_ple, tuple(extras), tuple(specs),
                            stream_outs if next_layer is not None else (("tile", F32),), ssq1))
    res = _matmul(ple_w_gate, i, 0, D_MODEL, tn, ops, "ple")
    x_new = [r[0] for r in res]
    h_next = [_Normed(r[1], r[2]) for r in res] if next_layer is not None else None
    return x_new, h_next


def _retention_proj(sts, hins, w_in, jl):
    tn = 1024
    ropes = [st.ret_rope for st in sts]
    rope_specs = [st.rope_specs(st.proj_tile(h)) for st, h in zip(sts, hins)]
    q = _proj("ret_q", sts, hins, w_in, jl, 0, RET_QK, tn, functools.partial(_ep_rope_ret, scale=1.0),
              ropes, rope_specs)
    k = _proj("ret_k", sts, hins, w_in, jl, RET_QK, RET_QK, tn,
              functools.partial(_ep_rope_ret, scale=RET_DK ** -0.5), ropes, rope_specs)
    vg = _proj("ret_vg", sts, hins, w_in, jl, 2 * RET_QK, 2 * RET_V, tn, _ep_plain)
    return q, k, vg


def _dilated_proj(sts, hs_by_rate, h_sample, w_in, jl):
    prm, smp = sts
    tn = 1024
    tiles = DIL_W // tn
    epilogues = ((0, functools.partial(_ep_rope_dil, scale=DIL_HD ** -0.5)),
                 (tiles, functools.partial(_ep_rope_dil, scale=1.0)),
                 (2 * tiles, _ep_plain))
    qkvs = []
    for g, d in enumerate(DIL_RATES):
        rope = prm.dil_rope if d == 1 else prm.dil_rope_residue_major(d)
        qkvs.append(_proj(f"dil_qkv_g{g}", sts, (hs_by_rate[g], h_sample), w_in, jl, g * 3 * DIL_W, 3 * DIL_W, tn,
                          epilogues, (rope, smp.dil_rope),
                          (prm.rope_specs(prm.proj_tile(hs_by_rate[g])), smp.rope_specs(smp.proj_tile(h_sample)))))
    gates = _proj("dil_gate", sts, (hs_by_rate[0], h_sample), w_in, jl, DIL_GROUPS * 3 * DIL_W, DIL_W, tn, _ep_plain)
    return qkvs, gates


def _window_prompt_body(k_ref, v_ref, o_ref, *, d):
    mb = k_ref.shape[1]
    for r in range(d):
        for kv, ref in enumerate((k_ref, v_ref)):
            for h in range(DIL_HEADS):
                val = ref[r, :, h * DIL_HD:(h + 1) * DIL_HD].astype(F32)
                o_ref[pl.ds((r * 2 + kv) * DIL_HEADS + h, mb, stride=d * 2 * DIL_HEADS), :] = val


def _window_rows_prompt(qkv, st, d):
    mb = min(DIL_SPAN, 256 // d)
    per_row = 2 * DIL_HEADS
    first = (st.t // d - DIL_SPAN) // mb
    view = qkv.reshape(st.batch, d, st.t // d, 3 * DIL_W)
    out = pl.pallas_call(
        functools.partial(_window_prompt_body, d=d),
        grid=(st.batch, DIL_SPAN // mb),
        in_specs=[pl.BlockSpec((None, d, mb, DIL_W), lambda b, i: (b, 0, first + i, 1)),
                  pl.BlockSpec((None, d, mb, DIL_W), lambda b, i: (b, 0, first + i, 2))],
        out_specs=pl.BlockSpec((None, mb * d * per_row, DIL_HD), lambda b, i: (b, i, 0)),
        out_shape=jax.ShapeDtypeStruct((st.batch, DIL_SPAN * d * per_row, DIL_HD), F32),
        compiler_params=_params(2),
        name="window_prompt",
    )(view, view)
    return out.reshape(st.batch, DIL_SPAN * d, 2, DIL_HEADS, DIL_HD)


def _window_rows_sample(qkv, st):
    kv = qkv.reshape(st.batch, st.t_pad, 3, DIL_HEADS, DIL_HD)[:, :st.t, 1:]
    return kv.astype(F32)


def kernel(x_prompt, x_sample, state_ret, cache_win_g0, cache_win_g1, cache_win_g2, p_prompt, p_sample, norm_g,
           ret_w_in, ret_w_out, dil_w_in, dil_w_out, gm_w_in, gm_ln_g, gm_ln_b, gm_w_s, gm_b_s, gm_w_out, ple_w,
           ple_norm_g, ple_w_gate, final_norm_g):
    depth = norm_g.shape[0]
    bp, sp, _ = x_prompt.shape
    bs_, ts, _ = x_sample.shape
    assert ts <= SAMPLE_PAD and ts % GM_CHUNK != 0 and sp % (DIL_SPAN * max(DIL_RATES)) == 0
    caches = (cache_win_g0, cache_win_g1, cache_win_g2)

    prm = _Stream(bp, sp, sp, 0, 1024, 2048, BF16, True)
    smp = _Stream(bs_, ts, SAMPLE_PAD, PAST_LEN, bs_ * SAMPLE_PAD, bs_ * SAMPLE_PAD, F32, False)
    sts = (prm, smp)

    pad_t = ((0, 0), (0, SAMPLE_PAD - ts), (0, 0))
    xp = x_prompt.reshape(prm.rows, D_MODEL)
    xs = jnp.pad(x_sample, pad_t).reshape(smp.rows, D_MODEL)
    p3s = (p_prompt.astype(BF16).reshape(depth, prm.rows, PLE_DIM),
           jnp.pad(p_sample, ((0, 0),) + pad_t).astype(BF16).reshape(depth, smp.rows, PLE_DIM))

    norm_g3 = norm_g[:, None, :]
    ple_g3 = ple_norm_g[:, None, :]
    ln_g3 = gm_ln_g[:, None, :]
    ln_b3 = gm_ln_b[:, None, :]
    fin_g3 = final_norm_g[None, None, :]

    ret_p, ret_s, gm_s = [], [], []
    win_p = [[], [], []]
    win_s = [[], [], []]
    hins = [_Normed(_rmsnorm(xp, norm_g3, 0, BF16, 256)), _Normed(_rmsnorm(xs, norm_g3, 0, BF16, smp.rows))]
    for i in range(depth):
        kind, jl = i % 3, i // 3
        if kind == 0:
            q, k, vg = _retention_proj(sts, hins, ret_w_in, jl)
            yp, sp_new = _retention_prompt(q[0], k[0], vg[0], bp, sp)
            ys, ss_new = _retention_sample(q[1], k[1], vg[1], state_ret, jl, bs_, ts)
            ret_p.append(sp_new)
            ret_s.append(ss_new)
            w_out = ret_w_out
        elif kind == 1:
            hp_orders = [_Normed(a) for a in _rmsnorm_orders(xp, norm_g3, i, bp, sp, DIL_RATES[1:], 256)]
            h_sample = _Normed(_rmsnorm(xs, norm_g3, i, BF16, smp.rows))
            qkvs, gates = _dilated_proj(sts, hp_orders, h_sample, dil_w_in, jl)
            yp = _dilated_prompt([a[0] for a in qkvs], gates[0], bp, sp)
            ys = _dilated_sample([a[1] for a in qkvs], gates[1], caches, jl, bs_, ts).astype(BF16)
            for g in range(DIL_GROUPS):
                win_p[g].append(_window_rows_prompt(qkvs[g][0], prm, DIL_RATES[g]))
                win_s[g].append(_cache_shift(caches[g], _window_rows_sample(qkvs[g][1], smp), jl, bs_, ts))
            w_out = dil_w_out
        else:
            tn = 1024
            uv = _proj("gm_uv", sts, hins, gm_w_in, jl, 0, 2 * GM_WIDTH, tn, _ep_gelu)
            gate = _proj("gm_gate", sts, hins, gm_w_in, jl, 2 * GM_WIDTH, GM_WIDTH, tn, _ep_plain)
            wm_p = jnp.tril(gm_w_s[jl]).astype(BF16)
            bs_p = gm_b_s[jl][:, :, None]
            (yp,) = _gmlp_core(uv[0], gate[0], ln_g3, ln_b3, jl, wm_p, bs_p, False)
            wm_t = jnp.pad(jnp.tril(gm_w_s[jl][:, :ts, :ts]), ((0, 0), (0, SAMPLE_PAD - ts), (0, SAMPLE_PAD - ts)))
            wm_s = jnp.einsum("ab,gij->gaibj", jnp.eye(bs_, dtype=F32), wm_t).reshape(GM_GROUPS, smp.rows, smp.rows)
            bs_s = jnp.tile(jnp.pad(gm_b_s[jl][:, :ts], ((0, 0), (0, SAMPLE_PAD - ts))), (1, bs_))[:, :, None]
            ys, vn = _gmlp_core(uv[1], gate[1], ln_g3, ln_b3, jl, wm_s.astype(BF16), bs_s, True)
            gm_s.append(vn.reshape(bs_, SAMPLE_PAD, GM_WIDTH)[:, :ts])
            w_out = gm_w_out
        fold_next = i + 1 < depth and (i + 1) % 3 != 1
        (xp, xs), hins = _out_and_ple(sts, (yp, ys.astype(BF16)), w_out, jl, (xp, xs), i, p3s, ple_w, ple_g3,
                                      ple_w_gate, norm_g3, i + 1 if fold_next else None)

    y_prompt = _rmsnorm(xp, fin_g3, 0, F32, 256).reshape(bp, sp, D_MODEL)
    y_sample = _rmsnorm(xs, fin_g3, 0, F32, smp.rows).reshape(bs_, SAMPLE_PAD, D_MODEL)[:, :ts]
    return (y_prompt, y_sample,
            jnp.stack(ret_p), jnp.stack(ret_s),
            jnp.stack(win_p[0]), jnp.stack(win_s[0]),
            jnp.stack(win_p[1]), jnp.stack(win_s[1]),
            jnp.stack(win_p[2]), jnp.stack(win_s[2]),
            jnp.stack(gm_s))
```

```python
import functools
from typing import NamedTuple

import jax
import jax.numpy as jnp
import numpy as np
from jax import lax
from jax.experimental import pallas as pl
from jax.experimental.pallas import tpu as pltpu

F32 = jnp.float32
BF16 = jnp.bfloat16

D_MODEL = 2048
PAST_LEN = 16384
PLE_DIM = 256
ROPE_THETA = 10000.0
EPS = 1e-6
NEG = -1e30

RET_HEADS = 8
RET_DK = 256
RET_DV = 512
RET_QK = RET_HEADS * RET_DK
RET_V = RET_HEADS * RET_DV
RET_CHUNK = 128

DIL_RATES = (1, 4, 16)
DIL_GROUPS = 3
DIL_SPAN = 128
DIL_HEADS = 16
DIL_HD = 128
DIL_W = DIL_HEADS * DIL_HD
DIL_Q_SCALE = DIL_HD ** -0.5 * 1.4426950408889634

GM_WIDTH = 2 * D_MODEL
GM_GROUPS = 16
GM_GD = GM_WIDTH // GM_GROUPS
GM_CHUNK = 128

LANE = 128
MXU_COLS = 256
SAMPLE_PAD = 16
VMEM_LIMIT = 56 * 1024 * 1024


def _params(n_axes, vmem=VMEM_LIMIT):
    return pltpu.CompilerParams(dimension_semantics=("arbitrary",) * n_axes, vmem_limit_bytes=vmem)


def _silu(x):
    return x * jax.nn.sigmoid(x)


def _rms_body(x_ref, g_ref, o_ref):
    x = x_ref[...]
    y = x * lax.rsqrt(jnp.mean(x * x, axis=-1, keepdims=True) + EPS)
    o_ref[...] = (y * g_ref[...]).astype(o_ref.dtype)


def _rmsnorm(x, g3, layer, out_dtype, tr):
    r, d = x.shape
    return pl.pallas_call(
        _rms_body,
        grid=(r // tr,),
        in_specs=[pl.BlockSpec((tr, d), lambda i: (i, 0)),
                  pl.BlockSpec((None, 1, d), lambda i: (layer, 0, 0))],
        out_specs=pl.BlockSpec((tr, d), lambda i: (i, 0)),
        out_shape=jax.ShapeDtypeStruct((r, d), out_dtype),
        compiler_params=_params(1),
        name="rmsnorm",
    )(x, g3)


def _rms_orders_body(x_ref, g_ref, perm_ref, o_ref, *perm_out_refs, rates):
    x = x_ref[...]
    y = (x * lax.rsqrt(jnp.mean(x * x, axis=-1, keepdims=True) + EPS) * g_ref[...]).astype(o_ref.dtype)
    o_ref[...] = y
    tr = x.shape[0]
    for k, (p_ref, d) in enumerate(zip(perm_out_refs, rates)):
        yp = jnp.dot(perm_ref[k], y, preferred_element_type=F32).astype(o_ref.dtype)
        for r in range(d):
            p_ref[r] = yp[r * (tr // d):(r + 1) * (tr // d), :]


def _rmsnorm_orders(x, g3, layer, batch, seq, rates, tr):
    r, dm = x.shape
    nt = seq // tr
    rows = jnp.arange(tr, dtype=jnp.int32)
    perms = jnp.stack([(rows[None, :] == (rows % (tr // d) * d + rows // (tr // d))[:, None]) for d in rates])
    out_specs = [pl.BlockSpec((tr, dm), lambda i: (i, 0))]
    out_shape = [jax.ShapeDtypeStruct((r, dm), BF16)]
    for d in rates:
        out_specs.append(pl.BlockSpec((None, d, tr // d, dm), lambda i: (i // nt, 0, i % nt, 0)))
        out_shape.append(jax.ShapeDtypeStruct((batch, d, seq // d, dm), BF16))
    res = pl.pallas_call(
        functools.partial(_rms_orders_body, rates=rates),
        grid=(r // tr,),
        in_specs=[pl.BlockSpec((tr, dm), lambda i: (i, 0)),
                  pl.BlockSpec((None, 1, dm), lambda i: (layer, 0, 0)),
                  pl.BlockSpec((len(rates), tr, tr), lambda i: (0, 0, 0))],
        out_specs=out_specs,
        out_shape=out_shape,
        compiler_params=_params(1),
        name="rmsnorm_orders",
    )(x, g3, perms.astype(BF16))
    return [res[0]] + [a.reshape(r, dm) for a in res[1:]]


class _Operand(NamedTuple):
    x: jax.Array
    tm: int
    tiled: bool
    epilogue: object
    extras: tuple = ()
    extra_specs: tuple = ()
    outs: tuple = (("tile", F32),)
    ssq: object = None


def _mm_body(w_ref, *refs, layout):
    n_scratch = 1 + sum(has_ssq for _, _, has_ssq, _, _ in layout)
    wb_ref = refs[-n_scratch]
    rinv_refs = list(refs[len(refs) - n_scratch + 1:])
    j = pl.program_id(0)
    i = pl.program_id(1)

    @pl.when(i == 0)
    def _():
        wb_ref[...] = w_ref[...].astype(BF16)

    def stream(x_ref, ssq_ref, rinv_ref, tile, extra, outs, epilogue):
        rinv = None
        if ssq_ref is not None:
            @pl.when(j == 0)
            def _():
                mean_sq = jnp.sum(jnp.sum(ssq_ref[...], axis=0), axis=-1, keepdims=True) / x_ref.shape[1]
                rinv_ref[tile] = lax.rsqrt(mean_sq + EPS)

            rinv = rinv_ref[tile]

        def run(ep):
            for c in range(wb_ref.shape[1] // MXU_COLS):
                cs = slice(c * MXU_COLS, (c + 1) * MXU_COLS)
                acc = jnp.dot(x_ref[...], wb_ref[:, cs], preferred_element_type=F32)
                ep(acc if rinv is None else acc * rinv, cs, extra, outs)

        if callable(epilogue):
            run(epilogue)
        else:
            starts = [s for s, _ in epilogue] + [None]
            for (lo, ep), hi in zip(epilogue, starts[1:]):
                pred = j >= lo if hi is None else jnp.logical_and(j >= lo, j < hi)
                pl.when(pred)(functools.partial(run, ep))

    n_in = sum(n for n, _, _, _, _ in layout)
    pos, opos = 0, n_in
    for n, n_out, has_ssq, tiled, epilogue in layout:
        x_ref = refs[pos]
        ssq_ref = refs[pos + 1] if has_ssq else None
        rinv_ref = rinv_refs.pop(0) if has_ssq else None
        extra = refs[pos + 1 + has_ssq:pos + n]
        outs = refs[opos:opos + n_out]
        pos, opos = pos + n, opos + n_out
        body = functools.partial(stream, x_ref, ssq_ref, rinv_ref, i if tiled else 0, extra, outs, epilogue)
        if tiled:
            body()
        else:
            pl.when(i == 0)(body)


def _matmul(w3, layer, n0, n, tn, operands, name):
    k = w3.shape[1]
    j0 = n0 // tn
    steps = [op.x.shape[0] // op.tm for op in operands if op.tiled]
    in_specs = [pl.BlockSpec((None, k, tn), lambda j, i: (layer, 0, j + j0))]
    args, out_specs, out_shape, layout = [w3], [], [], []
    scratch = [pltpu.VMEM((k, tn), BF16)]
    for op in operands:
        rows = op.x.shape[0]
        row = (lambda i: i) if op.tiled else (lambda i: 0)
        in_specs.append(pl.BlockSpec((op.tm, k), lambda j, i, row=row: (row(i), 0)))
        args.append(op.x)
        if op.ssq is not None:
            in_specs.append(pl.BlockSpec((op.ssq.shape[0], op.tm, LANE),
                                         lambda j, i, row=row: (0, jnp.where(j == 0, row(i), 0), 0)))
            args.append(op.ssq)
            scratch.append(pltpu.VMEM((rows // op.tm, op.tm, 1), F32))
        in_specs += list(op.extra_specs)
        args += list(op.extras)
        for out in op.outs:
            if out[0] == "tile":
                out_specs.append(pl.BlockSpec((op.tm, tn), lambda j, i, row=row: (row(i), j)))
                out_shape.append(jax.ShapeDtypeStruct((rows, n), out[1]))
            else:
                out_specs.append(pl.BlockSpec((None, op.tm, LANE), lambda j, i, row=row: (j, row(i), 0)))
                out_shape.append(jax.ShapeDtypeStruct((n // tn, rows, LANE), F32))
        layout.append((1 + (op.ssq is not None) + len(op.extras), len(op.outs), op.ssq is not None, op.tiled,
                       op.epilogue))
    res = pl.pallas_call(
        functools.partial(_mm_body, layout=tuple(layout)),
        grid=(n // tn, steps[0]),
        in_specs=in_specs,
        out_specs=out_specs,
        out_shape=out_shape,
        scratch_shapes=scratch,
        compiler_params=_params(2),
        name=name,
    )(*args)
    groups, pos = [], 0
    for op in operands:
        groups.append(tuple(res[pos:pos + len(op.outs)]))
        pos += len(op.outs)
    return groups


def _ep_plain(acc, cs, extra, outs):
    outs[0][:, cs] = acc.astype(outs[0].dtype)


def _ep_gelu(acc, cs, extra, outs):
    outs[0][:, cs] = jax.nn.gelu(acc).astype(outs[0].dtype)


def _emit_stream(x_new, cs, g_ref, outs):
    outs[0][:, cs] = x_new
    if len(outs) == 1:
        return
    outs[1][:, cs] = (x_new * g_ref[:, cs]).astype(outs[1].dtype)
    sq = x_new * x_new
    part = sq[:, :LANE] + sq[:, LANE:]
    if cs.start == 0:
        outs[2][...] = part
    else:
        outs[2][...] += part


def _ep_residual(acc, cs, extra, outs):
    _emit_stream(extra[0][:, cs] + acc, cs, extra[1] if len(extra) > 1 else None, outs)


def _ep_rope_ret(acc, cs, extra, outs, *, scale):
    cos = extra[0][...]
    sin = extra[1][...]
    o = outs[0]
    half = RET_DK // 2
    for h in range(acc.shape[1] // RET_DK):
        x1 = acc[:, h * RET_DK:h * RET_DK + half]
        x2 = acc[:, h * RET_DK + half:(h + 1) * RET_DK]
        c0 = cs.start + h * RET_DK
        y1 = x1 * cos - x2 * sin
        y2 = x2 * cos + x1 * sin
        o[:, c0:c0 + half] = (y1 if scale == 1.0 else y1 * scale).astype(o.dtype)
        o[:, c0 + half:c0 + RET_DK] = (y2 if scale == 1.0 else y2 * scale).astype(o.dtype)


def _ep_rope_dil(acc, cs, extra, outs, *, scale):
    cos2 = extra[0][...]
    sin2 = extra[1][...]
    o = outs[0]
    for h in range(acc.shape[1] // DIL_HD):
        x = acc[:, h * DIL_HD:(h + 1) * DIL_HD]
        swapped = pltpu.roll(x, DIL_HD // 2, axis=1)
        c0 = cs.start + h * DIL_HD
        y = x * cos2 + swapped * sin2
        o[:, c0:c0 + DIL_HD] = (y if scale == 1.0 else y * scale).astype(o.dtype)


def _ep_ple(acc, cs, extra, outs):
    pe = jnp.dot(extra[1][...], extra[2][:, cs].astype(BF16), preferred_element_type=F32)
    _emit_stream(extra[0][:, cs] + pe * jax.nn.sigmoid(acc), cs, extra[3] if len(extra) > 3 else None, outs)


def _rope_tables(pos, half):
    inv = ROPE_THETA ** (-np.arange(half, dtype=np.float64) / half)
    ang = pos.astype(np.float64)[:, None] * inv[None, :]
    return np.cos(ang).astype(np.float32), np.sin(ang).astype(np.float32)


def _ret_tables(c, c_pad):
    lg = jnp.log1p(-jnp.exp2(-5.0 - jnp.arange(RET_HEADS, dtype=F32)))
    i = jnp.arange(c_pad, dtype=F32)
    diff = i[:, None] - i[None, :]
    dmat = jnp.where(diff[None] >= 0, jnp.exp(lg[:, None, None] * jnp.maximum(diff, 0.0)[None]), 0.0)
    xi = jnp.exp(lg[:, None] * (i[None, :] + 1.0))[:, :, None]
    zeta = jnp.exp(lg[:, None] * (c - 1.0 - i[None, :]))[:, :, None]
    gc = jnp.broadcast_to(jnp.exp(lg * c)[:, None, None], (RET_HEADS, 1, RET_DV))
    return dmat, xi, zeta, gc


def _ret_finish(o, g):
    on = o * lax.rsqrt(jnp.mean(o * o, axis=-1, keepdims=True) + EPS)
    return on * _silu(g)


_NT = (((1,), (1,)), ((), ()))
_TN = (((0,), (0,)), ((), ()))


def _ret_prompt_body(q_ref, k_ref, v_ref, g_ref, dmat_ref, xi_ref, zeta_ref, gc_ref, o_ref, s_ref,
                     sc_s, kv_s, sb_s, *, c, cb):
    @pl.when(pl.program_id(2) == 0)
    def _():
        s_ref[...] = jnp.zeros_like(s_ref)

    dmat = dmat_ref[...]
    xi = xi_ref[...]
    zeta = zeta_ref[...]
    gc = gc_ref[...]
    for i in range(cb):
        rows = slice(i * c, (i + 1) * c)
        k = k_ref[rows, :]
        sc = lax.dot_general(q_ref[rows, :], k, _NT, preferred_element_type=F32) * dmat
        sc_s[i] = sc.astype(BF16)
        kz = (k.astype(F32) * zeta).astype(BF16)
        kv_s[i] = lax.dot_general(kz, v_ref[rows, :], _TN, preferred_element_type=F32)
    for i in range(cb):
        s = s_ref[...]
        sb_s[i] = s.astype(BF16)
        s_ref[...] = s * gc + kv_s[i]
    for i in range(cb):
        rows = slice(i * c, (i + 1) * c)
        o = jnp.dot(sc_s[i], v_ref[rows, :], preferred_element_type=F32)
        o = o + jnp.dot(q_ref[rows, :], sb_s[i], preferred_element_type=F32) * xi
        o_ref[rows, :] = _ret_finish(o, g_ref[rows, :].astype(F32)).astype(o_ref.dtype)


def _retention_prompt(q, k, vg, batch, seq):
    c = RET_CHUNK
    cb = 8
    rb = c * cb
    ncb = seq // rb
    dmat, xi, zeta, gc = _ret_tables(c, c)
    row = lambda b, h, i: (b * ncb + i, h)
    tab = lambda b, h, i: (h, 0, 0)
    return pl.pallas_call(
        functools.partial(_ret_prompt_body, c=c, cb=cb),
        grid=(batch, RET_HEADS, ncb),
        in_specs=[pl.BlockSpec((rb, RET_DK), row),
                  pl.BlockSpec((rb, RET_DK), row),
                  pl.BlockSpec((rb, RET_DV), row),
                  pl.BlockSpec((rb, RET_DV), lambda b, h, i: (b * ncb + i, RET_HEADS + h)),
                  pl.BlockSpec((None, c, c), tab),
                  pl.BlockSpec((None, c, 1), tab),
                  pl.BlockSpec((None, c, 1), tab),
                  pl.BlockSpec((None, 1, RET_DV), tab)],
        out_specs=[pl.BlockSpec((rb, RET_DV), row),
                   pl.BlockSpec((None, None, RET_DK, RET_DV), lambda b, h, i: (b, h, 0, 0))],
        out_shape=[jax.ShapeDtypeStruct((batch * seq, RET_V), BF16),
                   jax.ShapeDtypeStruct((batch, RET_HEADS, RET_DK, RET_DV), F32)],
        scratch_shapes=[pltpu.VMEM((cb, c, c), BF16), pltpu.VMEM((cb, RET_DK, RET_DV), F32),
                        pltpu.VMEM((cb, RET_DK, RET_DV), BF16)],
        compiler_params=_params(3),
        name="retention_prompt",
    )(q, k, vg, vg, dmat, xi, zeta, gc)


def _ret_sample_body(q_ref, k_ref, v_ref, g_ref, s0_ref, dmat_ref, xi_ref, zeta_ref, gc_ref, o_ref, s_ref, *, nb):
    q = q_ref[...].astype(BF16)
    k = k_ref[...]
    v = v_ref[...].astype(BF16)
    xi = xi_ref[...]
    gc = gc_ref[...]
    sc = lax.dot_general(q, k.astype(BF16), _NT, preferred_element_type=F32) * dmat_ref[...]
    o = jnp.dot(sc.astype(BF16), v, preferred_element_type=F32)
    kz = (k * zeta_ref[...]).astype(BF16)
    row_batch = lax.broadcasted_iota(jnp.int32, kz.shape, 0) // SAMPLE_PAD
    cross = []
    for b in range(nb):
        s0 = s0_ref[b]
        qb = q[b * SAMPLE_PAD:(b + 1) * SAMPLE_PAD, :]
        cross.append(jnp.dot(qb, s0.astype(BF16), preferred_element_type=F32))
        kzb = jnp.where(row_batch == b, kz, jnp.zeros_like(kz))
        s_ref[b] = s0 * gc + lax.dot_general(kzb, v, _TN, preferred_element_type=F32)
    o = o + jnp.concatenate(cross, axis=0) * xi
    o_ref[...] = _ret_finish(o, g_ref[...]).astype(o_ref.dtype)


def _retention_sample(q, k, vg, state, layer, nb, t):
    rows = nb * SAMPLE_PAD
    dmat, xi, zeta, gc = _ret_tables(t, SAMPLE_PAD)
    eye = jnp.eye(nb, dtype=F32)
    dbig = jnp.einsum("ab,hij->haibj", eye, dmat).reshape(RET_HEADS, rows, rows)
    xib = jnp.tile(xi, (1, nb, 1))
    zetab = jnp.tile(zeta, (1, nb, 1))
    col = lambda h: (0, h)
    tab = lambda h: (h, 0, 0)
    return pl.pallas_call(
        functools.partial(_ret_sample_body, nb=nb),
        grid=(RET_HEADS,),
        in_specs=[pl.BlockSpec((rows, RET_DK), col),
                  pl.BlockSpec((rows, RET_DK), col),
                  pl.BlockSpec((rows, RET_DV), col),
                  pl.BlockSpec((rows, RET_DV), lambda h: (0, RET_HEADS + h)),
                  pl.BlockSpec((None, nb, None, RET_DK, RET_DV), lambda h: (layer, 0, h, 0, 0)),
                  pl.BlockSpec((None, rows, rows), tab),
                  pl.BlockSpec((None, rows, 1), tab),
                  pl.BlockSpec((None, rows, 1), tab),
                  pl.BlockSpec((None, 1, RET_DV), tab)],
        out_specs=[pl.BlockSpec((rows, RET_DV), col),
                   pl.BlockSpec((nb, None, RET_DK, RET_DV), lambda h: (0, h, 0, 0))],
        out_shape=[jax.ShapeDtypeStruct((rows, RET_V), BF16),
                   jax.ShapeDtypeStruct((nb, RET_HEADS, RET_DK, RET_DV), F32)],
        compiler_params=_params(1),
        name="retention_sample",
    )(q, k, vg, vg, state, dbig, xib, zetab, gc)


DIL_PROMPT_SPAN = DIL_SPAN * max(DIL_RATES)
DIL_PROMPT_HEADS = 2


def _dil_prompt_body(*refs, span, hpb):
    ng = DIL_GROUPS
    q_refs, kc_refs, kp_refs = refs[0:ng], refs[ng:2 * ng], refs[2 * ng:3 * ng]
    vc_refs, vp_refs = refs[3 * ng:4 * ng], refs[4 * ng:5 * ng]
    gate_ref, o_ref, acc_s, m_s, l_s, sc_s, sp_s, pc_s, pp_s = refs[5 * ng:]
    has_prev = pl.program_id(1) > 0
    key = lax.broadcasted_iota(jnp.int32, (DIL_SPAN, DIL_SPAN), 0)
    qry = lax.broadcasted_iota(jnp.int32, (DIL_SPAN, DIL_SPAN), 1)
    eye = jnp.where(key == qry, 1.0, 0.0).astype(BF16)
    bias_c = jnp.where(key <= qry, 0.0, NEG).astype(BF16)
    bias_p = jnp.where(key >= qry, 0.0, NEG).astype(BF16)
    bias_p0 = jnp.where(jnp.logical_and(key >= qry, has_prev), 0.0, NEG).astype(BF16)

    def scores(q, k, bias):
        return lax.dot_general(jnp.concatenate([q, eye], axis=1), jnp.concatenate([k, bias], axis=1), _NT,
                               preferred_element_type=F32)

    def prev_refs(g, r, n, cur_ref, prv_ref, hs):
        if n > 0:
            return cur_ref[r, (n - 1) * DIL_SPAN:n * DIL_SPAN, hs]
        return prv_ref[r, :, hs]

    for hl in range(hpb):
        hs = slice(hl * DIL_HD, (hl + 1) * DIL_HD)
        for g in range(ng):
            d = DIL_RATES[g]
            blocks = [(r, n) for r in range(d) for n in range(span // (DIL_SPAN * d))]
            for i, (r, n) in enumerate(blocks):
                rows = slice(n * DIL_SPAN, (n + 1) * DIL_SPAN)
                q = q_refs[g][r, rows, hs]
                kp = prev_refs(g, r, n, kc_refs[g], kp_refs[g], hs)
                sc_s[i] = scores(q, kc_refs[g][r, rows, hs], bias_c)
                sp_s[i] = scores(q, kp, bias_p if n > 0 else bias_p0)
            for i, (r, n) in enumerate(blocks):
                s_c = sc_s[i]
                s_p = sp_s[i]
                m = jnp.max(jnp.maximum(s_c, s_p), axis=-1, keepdims=True)
                p_c = jnp.exp2(s_c - m)
                p_p = jnp.exp2(s_p - m)
                den = jnp.sum(p_c + p_p, axis=-1, keepdims=True)
                pc_s[i] = p_c.astype(BF16)
                pp_s[i] = p_p.astype(BF16)
                dst = slice(n * DIL_SPAN, (n + 1) * DIL_SPAN) if d == 1 else pl.ds(n * DIL_SPAN * d + r, DIL_SPAN, stride=d)
                m_s[g, dst, :] = jnp.broadcast_to(m, (DIL_SPAN, LANE))
                l_s[g, dst, :] = jnp.broadcast_to(den, (DIL_SPAN, LANE))
            for i, (r, n) in enumerate(blocks):
                rows = slice(n * DIL_SPAN, (n + 1) * DIL_SPAN)
                vp = prev_refs(g, r, n, vc_refs[g], vp_refs[g], hs)
                acc = jnp.dot(pc_s[i], vc_refs[g][r, rows, hs], preferred_element_type=F32)
                acc = acc + jnp.dot(pp_s[i], vp, preferred_element_type=F32)
                dst = rows if d == 1 else pl.ds(n * DIL_SPAN * d + r, DIL_SPAN, stride=d)
                acc_s[g, dst, :] = acc
        mr = 64
        for c in range(span // mr):
            rs = slice(c * mr, (c + 1) * mr)
            ms = [m_s[g, rs, :] for g in range(ng)]
            m_all = jnp.maximum(jnp.maximum(ms[0], ms[1]), ms[2])
            es = [jnp.exp2(m - m_all) for m in ms]
            tot = es[0] * l_s[0, rs, :] + es[1] * l_s[1, rs, :] + es[2] * l_s[2, rs, :]
            num = es[0] * acc_s[0, rs, :] + es[1] * acc_s[1, rs, :] + es[2] * acc_s[2, rs, :]
            o_ref[rs, hs] = (num / tot * _silu(gate_ref[rs, hs].astype(F32))).astype(o_ref.dtype)


def _dilated_prompt(qkvs, gate, batch, seq):
    span, hpb = DIL_PROMPT_SPAN, DIL_PROMPT_HEADS
    wc = hpb * DIL_HD
    kcol, vcol = DIL_W // wc, 2 * DIL_W // wc
    nsp = seq // span
    views = [a.reshape(batch, d, seq // d, 3 * DIL_W) for a, d in zip(qkvs, DIL_RATES)]
    q_specs, kc_specs, kp_specs, vc_specs, vp_specs = [], [], [], [], []
    for d in DIL_RATES:
        cur = (None, d, span // d, wc)
        prv = (None, d, DIL_SPAN, wc)
        nblk = span // (d * DIL_SPAN)
        q_specs.append(pl.BlockSpec(cur, lambda b, s, h: (b, 0, s, h)))
        kc_specs.append(pl.BlockSpec(cur, lambda b, s, h: (b, 0, s, kcol + h)))
        vc_specs.append(pl.BlockSpec(cur, lambda b, s, h: (b, 0, s, vcol + h)))
        kp_specs.append(pl.BlockSpec(prv, lambda b, s, h, nblk=nblk: (b, 0, jnp.maximum(s * nblk - 1, 0), kcol + h)))
        vp_specs.append(pl.BlockSpec(prv, lambda b, s, h, nblk=nblk: (b, 0, jnp.maximum(s * nblk - 1, 0), vcol + h)))
    rows_spec = pl.BlockSpec((span, wc), lambda b, s, h: (b * nsp + s, h))
    return pl.pallas_call(
        functools.partial(_dil_prompt_body, span=span, hpb=hpb),
        grid=(batch, nsp, DIL_W // wc),
        in_specs=q_specs + kc_specs + kp_specs + vc_specs + vp_specs + [rows_spec],
        out_specs=rows_spec,
        out_shape=jax.ShapeDtypeStruct((batch * seq, DIL_W), BF16),
        scratch_shapes=([pltpu.VMEM((DIL_GROUPS, span, LANE), F32)] * 3
                        + [pltpu.VMEM((span // DIL_SPAN, DIL_SPAN, DIL_SPAN), F32)] * 2
                        + [pltpu.VMEM((span // DIL_SPAN, DIL_SPAN, DIL_SPAN), BF16)] * 2),
        compiler_params=_params(3),
        name="dilated_prompt",
    )(*(views * 5), gate)


def _dil_sample_body(qkv0_ref, qkv1_ref, qkv2_ref, gate_ref, c0_ref, c1_ref, c2_ref, o_ref, *, n_new):
    t = pl.program_id(1)

    @pl.when(t == 0)
    def _():
        o_ref[...] = jnp.zeros_like(o_ref)

    key_row = lax.broadcasted_iota(jnp.int32, (DIL_SPAN, 1, 1), 0)
    ms, dens, accs = [], [], []
    for g, (qkv_ref, c_ref) in enumerate(zip((qkv0_ref, qkv1_ref, qkv2_ref), (c0_ref, c1_ref, c2_ref))):
        k_heads = slice(DIL_HEADS, 2 * DIL_HEADS)
        v_heads = slice(2 * DIL_HEADS, 3 * DIL_HEADS)
        q = qkv_ref[t, 0:DIL_HEADS, :]
        s = jnp.sum(c_ref[:, 0] * q[None], axis=-1, keepdims=True)
        if DIL_RATES[g] == 1:
            s = jnp.where(key_row >= t, s, NEG)
            new_rows = [(tn, tn <= t) for tn in range(n_new)]
        else:
            new_rows = [(t, None)]
        s_new = []
        for tn, valid in new_rows:
            sn = jnp.sum(qkv_ref[tn, k_heads, :] * q, axis=-1, keepdims=True)
            s_new.append(sn if valid is None else jnp.where(valid, sn, NEG))
        m = jnp.max(s, axis=0)
        for sn in s_new:
            m = jnp.maximum(m, sn)
        p = jnp.exp2(s - m[None])
        den = jnp.sum(p, axis=0)
        acc = jnp.sum(p * c_ref[:, 1], axis=0)
        for (tn, _), sn in zip(new_rows, s_new):
            pn = jnp.exp2(sn - m)
            den = den + pn
            acc = acc + pn * qkv_ref[tn, v_heads, :]
        ms.append(m)
        dens.append(den)
        accs.append(acc)
    m_all = jnp.maximum(jnp.maximum(ms[0], ms[1]), ms[2])
    es = [jnp.exp2(m - m_all) for m in ms]
    tot = es[0] * dens[0] + es[1] * dens[1] + es[2] * dens[2]
    merged = (es[0] * accs[0] + es[1] * accs[1] + es[2] * accs[2]) / tot
    o_ref[t] = merged * _silu(gate_ref[t])


def _dilated_sample(qkvs, gate, caches, layer, nb, t):
    rows = nb * SAMPLE_PAD
    views, specs = [], []
    for g, d in enumerate(DIL_RATES):
        c = caches[g]
        assert c.shape[2] == DIL_SPAN * d and t <= d * (1 if d > 1 else DIL_SPAN)
        views.append(c.reshape(c.shape[0], nb, DIL_SPAN, d, 2, DIL_HEADS, DIL_HD))
        blk = (None, None, DIL_SPAN, None, 2, DIL_HEADS, DIL_HD)
        if d == 1:
            specs.append(pl.BlockSpec(blk, lambda b, i: (layer, b, 0, 0, 0, 0, 0)))
        else:
            specs.append(pl.BlockSpec(blk, lambda b, i: (layer, b, 0, i, 0, 0, 0)))
    rows3 = pl.BlockSpec((SAMPLE_PAD, 3 * DIL_HEADS, DIL_HD), lambda b, i: (b, 0, 0))
    rows1 = pl.BlockSpec((SAMPLE_PAD, DIL_HEADS, DIL_HD), lambda b, i: (b, 0, 0))
    out = pl.pallas_call(
        functools.partial(_dil_sample_body, n_new=t),
        grid=(nb, t),
        in_specs=[rows3, rows3, rows3, rows1] + specs,
        out_specs=rows1,
        out_shape=jax.ShapeDtypeStruct((rows, DIL_HEADS, DIL_HD), F32),
        compiler_params=_params(2),
        name="dilated_sample",
    )(*(a.reshape(rows, 3 * DIL_HEADS, DIL_HD) for a in qkvs), gate.reshape(rows, DIL_HEADS, DIL_HD), *views)
    return out.reshape(rows, DIL_W)


def _cache_shift_body(a_ref, nxt_ref, new_ref, o_ref, *, t, tb):
    last = pl.program_id(1) == pl.num_programs(1) - 1
    o_ref[0:tb - t] = a_ref[t:tb]

    @pl.when(last)
    def _():
        o_ref[tb - t:tb] = new_ref[...]

    @pl.when(jnp.logical_not(last))
    def _():
        o_ref[tb - t:tb] = nxt_ref[...]


def _cache_shift(cache, new, layer, nb, t):
    wb = cache.shape[2]
    tail = cache.shape[3:]
    tb = min(wb, 256)
    assert wb % tb == 0 and tb % t == 0
    zeros = (0,) * len(tail)
    return pl.pallas_call(
        functools.partial(_cache_shift_body, t=t, tb=tb),
        grid=(nb, wb // tb),
        in_specs=[pl.BlockSpec((None, None, tb) + tail, lambda b, i: (layer, b, i) + zeros),
                  pl.BlockSpec((None, None, t) + tail,
                               lambda b, i: (layer, b, jnp.minimum((i + 1) * (tb // t), wb // t - 1)) + zeros),
                  pl.BlockSpec((None, t) + tail, lambda b, i: (b, 0) + zeros)],
        out_specs=pl.BlockSpec((None, tb) + tail, lambda b, i: (b, i) + zeros),
        out_shape=jax.ShapeDtypeStruct(cache.shape[1:], cache.dtype),
        compiler_params=_params(2),
        name="cache_shift",
    )(cache, cache, new)


def _gmlp_body(u_ref, v_ref, gate_ref, lng_ref, lnb_ref, wm_ref, bs_ref, o_ref, *vn_ref):
    v = v_ref[...].astype(F32)
    mu = jnp.mean(v, axis=-1, keepdims=True)
    xc = v - mu
    vn = xc * lax.rsqrt(jnp.mean(xc * xc, axis=-1, keepdims=True) + EPS) * lng_ref[...] + lnb_ref[...]
    if vn_ref:
        vn_ref[0][...] = vn
    vb = vn.astype(BF16)
    for g in range(GM_GROUPS):
        gs = slice(g * GM_GD, (g + 1) * GM_GD)
        mixed = jnp.dot(wm_ref[g], vb[:, gs], preferred_element_type=F32) + bs_ref[g]
        o_ref[:, gs] = (u_ref[:, gs].astype(F32) * mixed * _silu(gate_ref[:, gs].astype(F32))).astype(o_ref.dtype)


def _gmlp_core(uv, gate, ln_g3, ln_b3, layer, wm, bs, want_vn):
    rows = gate.shape[0]
    c = wm.shape[1]
    row = lambda i: (i, 0)
    out_specs = [pl.BlockSpec((c, GM_WIDTH), row)]
    out_shape = [jax.ShapeDtypeStruct((rows, GM_WIDTH), BF16)]
    if want_vn:
        out_specs.append(pl.BlockSpec((c, GM_WIDTH), row))
        out_shape.append(jax.ShapeDtypeStruct((rows, GM_WIDTH), F32))
    res = pl.pallas_call(
        _gmlp_body,
        grid=(rows // c,),
        in_specs=[pl.BlockSpec((c, GM_WIDTH), row),
                  pl.BlockSpec((c, GM_WIDTH), lambda i: (i, 1)),
                  pl.BlockSpec((c, GM_WIDTH), row),
                  pl.BlockSpec((None, 1, GM_WIDTH), lambda i: (layer, 0, 0)),
                  pl.BlockSpec((None, 1, GM_WIDTH), lambda i: (layer, 0, 0)),
                  pl.BlockSpec((GM_GROUPS, c, c), lambda i: (0, 0, 0)),
                  pl.BlockSpec((GM_GROUPS, c, 1), lambda i: (0, 0, 0))],
        out_specs=out_specs,
        out_shape=out_shape,
        compiler_params=_params(1),
        name="gmlp_core",
    )(uv, uv, gate, ln_g3, ln_b3, wm, bs)
    return res


class _Stream:
    def __init__(self, batch, t, t_pad, pos0, tm, tm_proj, act_dtype, tiled):
        self.batch, self.t, self.t_pad, self.act, self.tiled = batch, t, t_pad, act_dtype, tiled
        self.tm, self.tm_proj = tm, tm_proj
        self.rows = batch * t_pad
        assert tiled or tm == tm_proj == self.rows
        self.row = (lambda i: i) if tiled else (lambda i: 0)
        pos = pos0 + np.arange(t_pad)
        cos, sin = _rope_tables(pos, RET_DK // 2)
        self.ret_rope = (cos, sin)
        cos, sin = _rope_tables(pos, DIL_HD // 2)
        self.dil_rope = (np.concatenate([cos, cos], axis=-1), np.concatenate([-sin, sin], axis=-1))
        if t_pad < tm_proj:
            rep = tm_proj // t_pad
            self.ret_rope = tuple(np.tile(a, (rep, 1)) for a in self.ret_rope)
            self.dil_rope = tuple(np.tile(a, (rep, 1)) for a in self.dil_rope)

    def proj_tile(self, h):
        return self.tm_proj if h.ssq is None else self.tm

    def rope_specs(self, tm):
        nt, row = max(self.t_pad // tm, 1), self.row
        return (pl.BlockSpec((tm, LANE), lambda j, i: (row(i) % nt, 0)),) * 2

    def tile_spec(self, tm, tn):
        row = self.row
        return pl.BlockSpec((tm, tn), lambda j, i: (row(i), j))

    def dil_rope_residue_major(self, d):
        t = self.t_pad
        return tuple(a.reshape(t // d, d, LANE).transpose(1, 0, 2).reshape(t, LANE) for a in self.dil_rope)


class _Normed(NamedTuple):
    x: jax.Array
    ssq: object = None


def _proj(name, sts, hins, w3, layer, n0, n, tn, epilogue, extras=None, extra_specs=None):
    ops = []
    for k, (st, h) in enumerate(zip(sts, hins)):
        ops.append(_Operand(h.x, st.proj_tile(h), st.tiled, epilogue, extras[k] if extras else (),
                            extra_specs[k] if extra_specs else (), (("tile", st.act),), h.ssq))
    return [r[0] for r in _matmul(w3, layer, n0, n, tn, ops, name)]


def _out_and_ple(sts, ys, w_out3, jl, xs, i, p3s, ple_w, ple_g3, ple_w_gate, next_g3, next_layer):
    gain = lambda l, tn: pl.BlockSpec((None, 1, tn), lambda j, m: (l, 0, j))
    stream_outs = (("tile", F32), ("tile", BF16), ("ssq",))
    tn = 512
    ops = [_Operand(y, st.tm, st.tiled, _ep_residual, (x, ple_g3), (st.tile_spec(st.tm, tn), gain(i, tn)),
                    stream_outs) for st, y, x in zip(sts, ys, xs)]
    mids = _matmul(w_out3, jl, 0, D_MODEL, tn, ops, "out_proj")
    tn = 1024
    ops = []
    for st, (x1, xg1, ssq1), p3 in zip(sts, mids, p3s):
        row = st.row
        tm = st.tm // 2 if st.tiled else st.tm
        extras = [x1, p3, ple_w]
        specs = [st.tile_spec(tm, tn),
                 pl.BlockSpec((None, tm, PLE_DIM), lambda j, m, row=row: (i, row(m), 0)),
                 pl.BlockSpec((None, PLE_DIM, tn), lambda j, m: (i, 0, j))]
        if next_layer is not None:
            extras.append(next_g3)
            specs.append(gain(next_layer, tn))
        ops.append(_Operand(xg1, tm, st.tiled, _ep_ple, tuple(extras), tuple(specs),
                            stream_outs if next_layer is not None else (("tile", F32),), ssq1))
    res = _matmul(ple_w_gate, i, 0, D_MODEL, tn, ops, "ple")
    x_new = [r[0] for r in res]
    h_next = [_Normed(r[1], r[2]) for r in res] if next_layer is not None else None
    return x_new, h_next


def _retention_proj(sts, hins, w_in, jl):
    tn = 1024
    ropes = [st.ret_rope for st in sts]
    rope_specs = [st.rope_specs(st.proj_tile(h)) for st, h in zip(sts, hins)]
    q = _proj("ret_q", sts, hins, w_in, jl, 0, RET_QK, tn, functools.partial(_ep_rope_ret, scale=1.0),
              ropes, rope_specs)
    k = _proj("ret_k", sts, hins, w_in, jl, RET_QK, RET_QK, tn,
              functools.partial(_ep_rope_ret, scale=RET_DK ** -0.5), ropes, rope_specs)
    vg = _proj("ret_vg", sts, hins, w_in, jl, 2 * RET_QK, 2 * RET_V, tn, _ep_plain)
    return q, k, vg


def _dilated_proj(sts, hs_by_rate, h_sample, w_in, jl):
    prm, smp = sts
    tn = 1024
    tiles = DIL_W // tn
    epilogues = ((0, functools.partial(_ep_rope_dil, scale=DIL_Q_SCALE)),
                 (tiles, functools.partial(_ep_rope_dil, scale=1.0)),
                 (2 * tiles, _ep_plain))
    qkvs = []
    for g, d in enumerate(DIL_RATES):
        rope = prm.dil_rope if d == 1 else prm.dil_rope_residue_major(d)
        qkvs.append(_proj(f"dil_qkv_g{g}", sts, (hs_by_rate[g], h_sample), w_in, jl, g * 3 * DIL_W, 3 * DIL_W, tn,
                          epilogues, (rope, smp.dil_rope),
                          (prm.rope_specs(prm.proj_tile(hs_by_rate[g])), smp.rope_specs(smp.proj_tile(h_sample)))))
    gates = _proj("dil_gate", sts, (hs_by_rate[0], h_sample), w_in, jl, DIL_GROUPS * 3 * DIL_W, DIL_W, tn, _ep_plain)
    return qkvs, gates


def _window_prompt_body(k_ref, v_ref, o_ref, *, d):
    mb = k_ref.shape[1]
    for r in range(d):
        for kv, ref in enumerate((k_ref, v_ref)):
            for h in range(DIL_HEADS):
                val = ref[r, :, h * DIL_HD:(h + 1) * DIL_HD].astype(F32)
                o_ref[pl.ds((r * 2 + kv) * DIL_HEADS + h, mb, stride=d * 2 * DIL_HEADS), :] = val


def _window_rows_prompt(qkv, st, d):
    mb = min(DIL_SPAN, 256 // d)
    per_row = 2 * DIL_HEADS
    first = (st.t // d - DIL_SPAN) // mb
    view = qkv.reshape(st.batch, d, st.t // d, 3 * DIL_W)
    out = pl.pallas_call(
        functools.partial(_window_prompt_body, d=d),
        grid=(st.batch, DIL_SPAN // mb),
        in_specs=[pl.BlockSpec((None, d, mb, DIL_W), lambda b, i: (b, 0, first + i, 1)),
                  pl.BlockSpec((None, d, mb, DIL_W), lambda b, i: (b, 0, first + i, 2))],
        out_specs=pl.BlockSpec((None, mb * d * per_row, DIL_HD), lambda b, i: (b, i, 0)),
        out_shape=jax.ShapeDtypeStruct((st.batch, DIL_SPAN * d * per_row, DIL_HD), F32),
        compiler_params=_params(2),
        name="window_prompt",
    )(view, view)
    return out.reshape(st.batch, DIL_SPAN * d, 2, DIL_HEADS, DIL_HD)


def _window_rows_sample(qkv, st):
    kv = qkv.reshape(st.batch, st.t_pad, 3, DIL_HEADS, DIL_HD)[:, :st.t, 1:]
    return kv.astype(F32)


def kernel(x_prompt, x_sample, state_ret, cache_win_g0, cache_win_g1, cache_win_g2, p_prompt, p_sample, norm_g,
           ret_w_in, ret_w_out, dil_w_in, dil_w_out, gm_w_in, gm_ln_g, gm_ln_b, gm_w_s, gm_b_s, gm_w_out, ple_w,
           ple_norm_g, ple_w_gate, final_norm_g):
    depth = norm_g.shape[0]
    bp, sp, _ = x_prompt.shape
    bs_, ts, _ = x_sample.shape
    assert ts <= SAMPLE_PAD and ts % GM_CHUNK != 0 and sp % (DIL_SPAN * max(DIL_RATES)) == 0
    caches = (cache_win_g0, cache_win_g1, cache_win_g2)

    prm = _Stream(bp, sp, sp, 0, 1024, 2048, BF16, True)
    smp = _Stream(bs_, ts, SAMPLE_PAD, PAST_LEN, bs_ * SAMPLE_PAD, bs_ * SAMPLE_PAD, F32, False)
    sts = (prm, smp)

    pad_t = ((0, 0), (0, SAMPLE_PAD - ts), (0, 0))
    xp = x_prompt.reshape(prm.rows, D_MODEL)
    xs = jnp.pad(x_sample, pad_t).reshape(smp.rows, D_MODEL)
    p3s = (p_prompt.astype(BF16).reshape(depth, prm.rows, PLE_DIM),
           jnp.pad(p_sample, ((0, 0),) + pad_t).astype(BF16).reshape(depth, smp.rows, PLE_DIM))

    norm_g3 = norm_g[:, None, :]
    ple_g3 = ple_norm_g[:, None, :]
    ln_g3 = gm_ln_g[:, None, :]
    ln_b3 = gm_ln_b[:, None, :]
    fin_g3 = final_norm_g[None, None, :]

    ret_p, ret_s, gm_s = [], [], []
    win_p = [[], [], []]
    win_s = [[], [], []]
    hins = [_Normed(_rmsnorm(xp, norm_g3, 0, BF16, 256)), _Normed(_rmsnorm(xs, norm_g3, 0, BF16, smp.rows))]
    for i in range(depth):
        kind, jl = i % 3, i // 3
        if kind == 0:
            q, k, vg = _retention_proj(sts, hins, ret_w_in, jl)
            yp, sp_new = _retention_prompt(q[0], k[0], vg[0], bp, sp)
            ys, ss_new = _retention_sample(q[1], k[1], vg[1], state_ret, jl, bs_, ts)
            ret_p.append(sp_new)
            ret_s.append(ss_new)
            w_out = ret_w_out
        elif kind == 1:
            hp_orders = [_Normed(a) for a in _rmsnorm_orders(xp, norm_g3, i, bp, sp, DIL_RATES[1:], 256)]
            h_sample = _Normed(_rmsnorm(xs, norm_g3, i, BF16, smp.rows))
            qkvs, gates = _dilated_proj(sts, hp_orders, h_sample, dil_w_in, jl)
            yp = _dilated_prompt([a[0] for a in qkvs], gates[0], bp, sp)
            ys = _dilated_sample([a[1] for a in qkvs], gates[1], caches, jl, bs_, ts).astype(BF16)
            for g in range(DIL_GROUPS):
                win_p[g].append(_window_rows_prompt(qkvs[g][0], prm, DIL_RATES[g]))
                win_s[g].append(_cache_shift(caches[g], _window_rows_sample(qkvs[g][1], smp), jl, bs_, ts))
            w_out = dil_w_out
        else:
            tn = 1024
            uv = _proj("gm_uv", sts, hins, gm_w_in, jl, 0, 2 * GM_WIDTH, tn, _ep_gelu)
            gate = _proj("gm_gate", sts, hins, gm_w_in, jl, 2 * GM_WIDTH, GM_WIDTH, tn, _ep_plain)
            wm_p = jnp.tril(gm_w_s[jl]).astype(BF16)
            bs_p = gm_b_s[jl][:, :, None]
            (yp,) = _gmlp_core(uv[0], gate[0], ln_g3, ln_b3, jl, wm_p, bs_p, False)
            wm_t = jnp.pad(jnp.tril(gm_w_s[jl][:, :ts, :ts]), ((0, 0), (0, SAMPLE_PAD - ts), (0, SAMPLE_PAD - ts)))
            wm_s = jnp.einsum("ab,gij->gaibj", jnp.eye(bs_, dtype=F32), wm_t).reshape(GM_GROUPS, smp.rows, smp.rows)
            bs_s = jnp.tile(jnp.pad(gm_b_s[jl][:, :ts], ((0, 0), (0, SAMPLE_PAD - ts))), (1, bs_))[:, :, None]
            ys, vn = _gmlp_core(uv[1], gate[1], ln_g3, ln_b3, jl, wm_s.astype(BF16), bs_s, True)
            gm_s.append(vn.reshape(bs_, SAMPLE_PAD, GM_WIDTH)[:, :ts])
            w_out = gm_w_out
        fold_next = i + 1 < depth and (i + 1) % 3 != 1
        (xp, xs), hins = _out_and_ple(sts, (yp, ys.astype(BF16)), w_out, jl, (xp, xs), i, p3s, ple_w, ple_g3,
                                      ple_w_gate, norm_g3, i + 1 if fold_next else None)

    y_prompt = _rmsnorm(xp, fin_g3, 0, F32, 256).reshape(bp, sp, D_MODEL)
    y_sample = _rmsnorm(xs, fin_g3, 0, F32, smp.rows).reshape(bs_, SAMPLE_PAD, D_MODEL)[:, :ts]
    return (y_prompt, y_sample,
            jnp.stack(ret_p), jnp.stack(ret_s),
            jnp.stack(win_p[0]), jnp.stack(win_s[0]),
            jnp.stack(win_p[1]), jnp.stack(win_s[1]),
            jnp.stack(win_p[2]), jnp.stack(win_s[2]),
            jnp.stack(gm_s))
```

```python
import functools
from typing import NamedTuple

import jax
import jax.numpy as jnp
import numpy as np
from jax import lax
from jax.experimental import pallas as pl
from jax.experimental.pallas import tpu as pltpu

F32 = jnp.float32
BF16 = jnp.bfloat16

D_MODEL = 2048
PAST_LEN = 16384
PLE_DIM = 256
ROPE_THETA = 10000.0
EPS = 1e-6
NEG = -1e30

RET_HEADS = 8
RET_DK = 256
RET_DV = 512
RET_QK = RET_HEADS * RET_DK
RET_V = RET_HEADS * RET_DV
RET_CHUNK = 128

DIL_RATES = (1, 4, 16)
DIL_GROUPS = 3
DIL_SPAN = 128
DIL_HEADS = 16
DIL_HD = 128
DIL_W = DIL_HEADS * DIL_HD
DIL_Q_SCALE = DIL_HD ** -0.5 * 1.4426950408889634

GM_WIDTH = 2 * D_MODEL
GM_GROUPS = 16
GM_GD = GM_WIDTH // GM_GROUPS
GM_CHUNK = 128

LANE = 128
MXU_COLS = 256
SAMPLE_PAD = 16
VMEM_LIMIT = 56 * 1024 * 1024


def _params(n_axes, vmem=VMEM_LIMIT):
    return pltpu.CompilerParams(dimension_semantics=("arbitrary",) * n_axes, vmem_limit_bytes=vmem)


def _silu(x):
    return x * jax.nn.sigmoid(x)


def _rms_body(x_ref, g_ref, o_ref):
    x = x_ref[...]
    y = x * lax.rsqrt(jnp.mean(x * x, axis=-1, keepdims=True) + EPS)
    o_ref[...] = (y * g_ref[...]).astype(o_ref.dtype)


def _rmsnorm(x, g3, layer, out_dtype, tr):
    r, d = x.shape
    return pl.pallas_call(
        _rms_body,
        grid=(r // tr,),
        in_specs=[pl.BlockSpec((tr, d), lambda i: (i, 0)),
                  pl.BlockSpec((None, 1, d), lambda i: (layer, 0, 0))],
        out_specs=pl.BlockSpec((tr, d), lambda i: (i, 0)),
        out_shape=jax.ShapeDtypeStruct((r, d), out_dtype),
        compiler_params=_params(1),
        name="rmsnorm",
    )(x, g3)


def _rms_orders_body(x_ref, g_ref, perm_ref, o_ref, *perm_out_refs, rates):
    x = x_ref[...]
    y = (x * lax.rsqrt(jnp.mean(x * x, axis=-1, keepdims=True) + EPS) * g_ref[...]).astype(o_ref.dtype)
    o_ref[...] = y
    tr = x.shape[0]
    for k, (p_ref, d) in enumerate(zip(perm_out_refs, rates)):
        yp = jnp.dot(perm_ref[k], y, preferred_element_type=F32).astype(o_ref.dtype)
        for r in range(d):
            p_ref[r] = yp[r * (tr // d):(r + 1) * (tr // d), :]


def _rmsnorm_orders(x, g3, layer, batch, seq, rates, tr):
    r, dm = x.shape
    nt = seq // tr
    rows = jnp.arange(tr, dtype=jnp.int32)
    perms = jnp.stack([(rows[None, :] == (rows % (tr // d) * d + rows // (tr // d))[:, None]) for d in rates])
    out_specs = [pl.BlockSpec((tr, dm), lambda i: (i, 0))]
    out_shape = [jax.ShapeDtypeStruct((r, dm), BF16)]
    for d in rates:
        out_specs.append(pl.BlockSpec((None, d, tr // d, dm), lambda i: (i // nt, 0, i % nt, 0)))
        out_shape.append(jax.ShapeDtypeStruct((batch, d, seq // d, dm), BF16))
    res = pl.pallas_call(
        functools.partial(_rms_orders_body, rates=rates),
        grid=(r // tr,),
        in_specs=[pl.BlockSpec((tr, dm), lambda i: (i, 0)),
                  pl.BlockSpec((None, 1, dm), lambda i: (layer, 0, 0)),
                  pl.BlockSpec((len(rates), tr, tr), lambda i: (0, 0, 0))],
        out_specs=out_specs,
        out_shape=out_shape,
        compiler_params=_params(1),
        name="rmsnorm_orders",
    )(x, g3, perms.astype(BF16))
    return [res[0]] + [a.reshape(r, dm) for a in res[1:]]


class _Operand(NamedTuple):
    x: jax.Array
    tm: int
    tiled: bool
    epilogue: object
    extras: tuple = ()
    extra_specs: tuple = ()
    outs: tuple = (("tile", F32),)
    ssq: object = None


def _mm_body(w_ref, *refs, layout):
    n_scratch = 1 + sum(has_ssq for _, _, has_ssq, _, _ in layout)
    wb_ref = refs[-n_scratch]
    rinv_refs = list(refs[len(refs) - n_scratch + 1:])
    j = pl.program_id(0)
    i = pl.program_id(1)

    @pl.when(i == 0)
    def _():
        wb_ref[...] = w_ref[...].astype(BF16)

    def stream(x_ref, ssq_ref, rinv_ref, tile, extra, outs, epilogue):
        rinv = None
        if ssq_ref is not None:
            @pl.when(j == 0)
            def _():
                mean_sq = jnp.sum(jnp.sum(ssq_ref[...], axis=0), axis=-1, keepdims=True) / x_ref.shape[1]
                rinv_ref[tile] = lax.rsqrt(mean_sq + EPS)

            rinv = rinv_ref[tile]

        def run(ep):
            for c in range(wb_ref.shape[1] // MXU_COLS):
                cs = slice(c * MXU_COLS, (c + 1) * MXU_COLS)
                acc = jnp.dot(x_ref[...], wb_ref[:, cs], preferred_element_type=F32)
                ep(acc if rinv is None else acc * rinv, cs, extra, outs)

        if callable(epilogue):
            run(epilogue)
        else:
            starts = [s for s, _ in epilogue] + [None]
            for (lo, ep), hi in zip(epilogue, starts[1:]):
                pred = j >= lo if hi is None else jnp.logical_and(j >= lo, j < hi)
                pl.when(pred)(functools.partial(run, ep))

    n_in = sum(n for n, _, _, _, _ in layout)
    pos, opos = 0, n_in
    for n, n_out, has_ssq, tiled, epilogue in layout:
        x_ref = refs[pos]
        ssq_ref = refs[pos + 1] if has_ssq else None
        rinv_ref = rinv_refs.pop(0) if has_ssq else None
        extra = refs[pos + 1 + has_ssq:pos + n]
        outs = refs[opos:opos + n_out]
        pos, opos = pos + n, opos + n_out
        body = functools.partial(stream, x_ref, ssq_ref, rinv_ref, i if tiled else 0, extra, outs, epilogue)
        if tiled:
            body()
        else:
            pl.when(i == 0)(body)


def _matmul(w3, layer, n0, n, tn, operands, name, single_buffer_w=False):
    k = w3.shape[1]
    j0 = n0 // tn
    steps = [op.x.shape[0] // op.tm for op in operands if op.tiled]
    w_mode = dict(pipeline_mode=pl.Buffered(1)) if single_buffer_w else {}
    in_specs = [pl.BlockSpec((None, k, tn), lambda j, i: (layer, 0, j + j0), **w_mode)]
    args, out_specs, out_shape, layout = [w3], [], [], []
    scratch = [pltpu.VMEM((k, tn), BF16)]
    for op in operands:
        rows = op.x.shape[0]
        row = (lambda i: i) if op.tiled else (lambda i: 0)
        in_specs.append(pl.BlockSpec((op.tm, k), lambda j, i, row=row: (row(i), 0)))
        args.append(op.x)
        if op.ssq is not None:
            in_specs.append(pl.BlockSpec((op.ssq.shape[0], op.tm, LANE),
                                         lambda j, i, row=row: (0, jnp.where(j == 0, row(i), 0), 0)))
            args.append(op.ssq)
            scratch.append(pltpu.VMEM((rows // op.tm, op.tm, 1), F32))
        in_specs += list(op.extra_specs)
        args += list(op.extras)
        for out in op.outs:
            if out[0] == "tile":
                out_specs.append(pl.BlockSpec((op.tm, tn), lambda j, i, row=row: (row(i), j)))
                out_shape.append(jax.ShapeDtypeStruct((rows, n), out[1]))
            else:
                out_specs.append(pl.BlockSpec((None, op.tm, LANE), lambda j, i, row=row: (j, row(i), 0)))
                out_shape.append(jax.ShapeDtypeStruct((n // tn, rows, LANE), F32))
        layout.append((1 + (op.ssq is not None) + len(op.extras), len(op.outs), op.ssq is not None, op.tiled,
                       op.epilogue))
    res = pl.pallas_call(
        functools.partial(_mm_body, layout=tuple(layout)),
        grid=(n // tn, steps[0]),
        in_specs=in_specs,
        out_specs=out_specs,
        out_shape=out_shape,
        scratch_shapes=scratch,
        compiler_params=_params(2),
        name=name,
    )(*args)
    groups, pos = [], 0
    for op in operands:
        groups.append(tuple(res[pos:pos + len(op.outs)]))
        pos += len(op.outs)
    return groups


def _ep_plain(acc, cs, extra, outs):
    outs[0][:, cs] = acc.astype(outs[0].dtype)


def _ep_gelu(acc, cs, extra, outs):
    a = -2.0 * 1.4426950408889634 * 0.7978845608028654
    z = acc * (a + (a * 0.044715) * (acc * acc))
    outs[0][:, cs] = (acc / (1.0 + jnp.exp2(z))).astype(outs[0].dtype)


def _emit_stream(x_new, cs, g_ref, outs):
    outs[0][:, cs] = x_new
    if len(outs) == 1:
        return
    outs[1][:, cs] = (x_new * g_ref[:, cs]).astype(outs[1].dtype)
    sq = x_new * x_new
    part = sq[:, :LANE] + sq[:, LANE:]
    if cs.start == 0:
        outs[2][...] = part
    else:
        outs[2][...] += part


def _ep_residual(acc, cs, extra, outs):
    _emit_stream(extra[0][:, cs] + acc, cs, extra[1] if len(extra) > 1 else None, outs)


def _ep_rope_ret(acc, cs, extra, outs, *, scale):
    cos = extra[0][...]
    sin = extra[1][...]
    o = outs[0]
    half = RET_DK // 2
    for h in range(acc.shape[1] // RET_DK):
        x1 = acc[:, h * RET_DK:h * RET_DK + half]
        x2 = acc[:, h * RET_DK + half:(h + 1) * RET_DK]
        c0 = cs.start + h * RET_DK
        y1 = x1 * cos - x2 * sin
        y2 = x2 * cos + x1 * sin
        o[:, c0:c0 + half] = (y1 if scale == 1.0 else y1 * scale).astype(o.dtype)
        o[:, c0 + half:c0 + RET_DK] = (y2 if scale == 1.0 else y2 * scale).astype(o.dtype)


def _ep_rope_dil(acc, cs, extra, outs, *, scale):
    cos2 = extra[0][...]
    sin2 = extra[1][...]
    o = outs[0]
    for h in range(acc.shape[1] // DIL_HD):
        x = acc[:, h * DIL_HD:(h + 1) * DIL_HD]
        swapped = pltpu.roll(x, DIL_HD // 2, axis=1)
        c0 = cs.start + h * DIL_HD
        y = x * cos2 + swapped * sin2
        o[:, c0:c0 + DIL_HD] = (y if scale == 1.0 else y * scale).astype(o.dtype)


def _ep_ple(acc, cs, extra, outs):
    pe = jnp.dot(extra[1][...], extra[2][:, cs].astype(BF16), preferred_element_type=F32)
    _emit_stream(extra[0][:, cs] + pe * jax.nn.sigmoid(acc), cs, extra[3] if len(extra) > 3 else None, outs)


def _rope_tables(pos, half):
    inv = ROPE_THETA ** (-np.arange(half, dtype=np.float64) / half)
    ang = pos.astype(np.float64)[:, None] * inv[None, :]
    return np.cos(ang).astype(np.float32), np.sin(ang).astype(np.float32)


def _ret_tables(c, c_pad):
    lg = jnp.log1p(-jnp.exp2(-5.0 - jnp.arange(RET_HEADS, dtype=F32)))
    i = jnp.arange(c_pad, dtype=F32)
    diff = i[:, None] - i[None, :]
    dmat = jnp.where(diff[None] >= 0, jnp.exp(lg[:, None, None] * jnp.maximum(diff, 0.0)[None]), 0.0)
    xi = jnp.exp(lg[:, None] * (i[None, :] + 1.0))[:, :, None]
    zeta = jnp.exp(lg[:, None] * (c - 1.0 - i[None, :]))[:, :, None]
    gc = jnp.broadcast_to(jnp.exp(lg * c)[:, None, None], (RET_HEADS, 1, RET_DV))
    return dmat, xi, zeta, gc


def _ret_finish(o, g):
    on = o * lax.rsqrt(jnp.mean(o * o, axis=-1, keepdims=True) + EPS)
    return on * _silu(g)


_NT = (((1,), (1,)), ((), ()))
_TN = (((0,), (0,)), ((), ()))


def _ret_prompt_body(q_ref, k_ref, v_ref, g_ref, dmat_ref, xi_ref, zeta_ref, gc_ref, o_ref, s_ref,
                     sc_s, kv_s, sb_s, *, c, cb):
    @pl.when(pl.program_id(2) == 0)
    def _():
        s_ref[...] = jnp.zeros_like(s_ref)

    dmat = dmat_ref[...]
    xi = xi_ref[...]
    zeta = zeta_ref[...]
    gc = gc_ref[...]
    for i in range(cb):
        rows = slice(i * c, (i + 1) * c)
        k = k_ref[rows, :]
        sc = lax.dot_general(q_ref[rows, :], k, _NT, preferred_element_type=F32) * dmat
        sc_s[i] = sc.astype(BF16)
        kz = (k.astype(F32) * zeta).astype(BF16)
        kv_s[i] = lax.dot_general(kz, v_ref[rows, :], _TN, preferred_element_type=F32)
    for i in range(cb):
        s = s_ref[...]
        sb_s[i] = s.astype(BF16)
        s_ref[...] = s * gc + kv_s[i]
    for i in range(cb):
        rows = slice(i * c, (i + 1) * c)
        o = jnp.dot(sc_s[i], v_ref[rows, :], preferred_element_type=F32)
        o = o + jnp.dot(q_ref[rows, :], sb_s[i], preferred_element_type=F32) * xi
        o_ref[rows, :] = _ret_finish(o, g_ref[rows, :].astype(F32)).astype(o_ref.dtype)


def _retention_prompt(qkvg, batch, seq):
    c = RET_CHUNK
    cb = 8
    rb = c * cb
    ncb = seq // rb
    dmat, xi, zeta, gc = _ret_tables(c, c)
    row = lambda b, h, i: (b * ncb + i, h)
    tab = lambda b, h, i: (h, 0, 0)
    k0, v0 = RET_QK // RET_DK, 2 * RET_QK // RET_DV
    return pl.pallas_call(
        functools.partial(_ret_prompt_body, c=c, cb=cb),
        grid=(batch, RET_HEADS, ncb),
        in_specs=[pl.BlockSpec((rb, RET_DK), row),
                  pl.BlockSpec((rb, RET_DK), lambda b, h, i: (b * ncb + i, k0 + h)),
                  pl.BlockSpec((rb, RET_DV), lambda b, h, i: (b * ncb + i, v0 + h)),
                  pl.BlockSpec((rb, RET_DV), lambda b, h, i: (b * ncb + i, v0 + RET_HEADS + h)),
                  pl.BlockSpec((None, c, c), tab),
                  pl.BlockSpec((None, c, 1), tab),
                  pl.BlockSpec((None, c, 1), tab),
                  pl.BlockSpec((None, 1, RET_DV), tab)],
        out_specs=[pl.BlockSpec((rb, RET_DV), row),
                   pl.BlockSpec((None, None, RET_DK, RET_DV), lambda b, h, i: (b, h, 0, 0))],
        out_shape=[jax.ShapeDtypeStruct((batch * seq, RET_V), BF16),
                   jax.ShapeDtypeStruct((batch, RET_HEADS, RET_DK, RET_DV), F32)],
        scratch_shapes=[pltpu.VMEM((cb, c, c), BF16), pltpu.VMEM((cb, RET_DK, RET_DV), F32),
                        pltpu.VMEM((cb, RET_DK, RET_DV), BF16)],
        compiler_params=_params(3),
        name="retention_prompt",
    )(qkvg, qkvg, qkvg, qkvg, dmat, xi, zeta, gc)


def _ret_sample_body(q_ref, k_ref, v_ref, g_ref, s0_ref, dmat_ref, xi_ref, zeta_ref, gc_ref, o_ref, s_ref, *, nb):
    q = q_ref[...].astype(BF16)
    k = k_ref[...]
    v = v_ref[...].astype(BF16)
    xi = xi_ref[...]
    gc = gc_ref[...]
    sc = lax.dot_general(q, k.astype(BF16), _NT, preferred_element_type=F32) * dmat_ref[...]
    o = jnp.dot(sc.astype(BF16), v, preferred_element_type=F32)
    kz = (k * zeta_ref[...]).astype(BF16)
    row_batch = lax.broadcasted_iota(jnp.int32, kz.shape, 0) // SAMPLE_PAD
    cross = []
    for b in range(nb):
        s0 = s0_ref[b]
        qb = q[b * SAMPLE_PAD:(b + 1) * SAMPLE_PAD, :]
        cross.append(jnp.dot(qb, s0.astype(BF16), preferred_element_type=F32))
        kzb = jnp.where(row_batch == b, kz, jnp.zeros_like(kz))
        s_ref[b] = s0 * gc + lax.dot_general(kzb, v, _TN, preferred_element_type=F32)
    o = o + jnp.concatenate(cross, axis=0) * xi
    o_ref[...] = _ret_finish(o, g_ref[...]).astype(o_ref.dtype)


def _retention_sample(qkvg, state, layer, nb, t):
    rows = nb * SAMPLE_PAD
    k0, v0 = RET_QK // RET_DK, 2 * RET_QK // RET_DV
    dmat, xi, zeta, gc = _ret_tables(t, SAMPLE_PAD)
    eye = jnp.eye(nb, dtype=F32)
    dbig = jnp.einsum("ab,hij->haibj", eye, dmat).reshape(RET_HEADS, rows, rows)
    xib = jnp.tile(xi, (1, nb, 1))
    zetab = jnp.tile(zeta, (1, nb, 1))
    col = lambda h: (0, h)
    tab = lambda h: (h, 0, 0)
    return pl.pallas_call(
        functools.partial(_ret_sample_body, nb=nb),
        grid=(RET_HEADS,),
        in_specs=[pl.BlockSpec((rows, RET_DK), col),
                  pl.BlockSpec((rows, RET_DK), lambda h: (0, k0 + h)),
                  pl.BlockSpec((rows, RET_DV), lambda h: (0, v0 + h)),
                  pl.BlockSpec((rows, RET_DV), lambda h: (0, v0 + RET_HEADS + h)),
                  pl.BlockSpec((None, nb, None, RET_DK, RET_DV), lambda h: (layer, 0, h, 0, 0)),
                  pl.BlockSpec((None, rows, rows), tab),
                  pl.BlockSpec((None, rows, 1), tab),
                  pl.BlockSpec((None, rows, 1), tab),
                  pl.BlockSpec((None, 1, RET_DV), tab)],
        out_specs=[pl.BlockSpec((rows, RET_DV), col),
                   pl.BlockSpec((nb, None, RET_DK, RET_DV), lambda h: (0, h, 0, 0))],
        out_shape=[jax.ShapeDtypeStruct((rows, RET_V), BF16),
                   jax.ShapeDtypeStruct((nb, RET_HEADS, RET_DK, RET_DV), F32)],
        compiler_params=_params(1),
        name="retention_sample",
    )(qkvg, qkvg, qkvg, qkvg, state, dbig, xib, zetab, gc)


DIL_PROMPT_SPAN = DIL_SPAN * max(DIL_RATES)
DIL_PROMPT_HEADS = 2


def _dil_prompt_body(*refs, span, hpb):
    ng = DIL_GROUPS
    q_refs, kc_refs, kp_refs = refs[0:ng], refs[ng:2 * ng], refs[2 * ng:3 * ng]
    vc_refs, vp_refs = refs[3 * ng:4 * ng], refs[4 * ng:5 * ng]
    gate_ref, o_ref, acc_s, m_s, l_s, sc_s, sp_s, pc_s, pp_s = refs[5 * ng:]
    has_prev = pl.program_id(1) > 0
    key = lax.broadcasted_iota(jnp.int32, (DIL_SPAN, DIL_SPAN), 0)
    qry = lax.broadcasted_iota(jnp.int32, (DIL_SPAN, DIL_SPAN), 1)
    eye = jnp.where(key == qry, 1.0, 0.0).astype(BF16)
    bias_c = jnp.where(key <= qry, 0.0, NEG).astype(BF16)
    bias_p = jnp.where(key >= qry, 0.0, NEG).astype(BF16)
    bias_p0 = jnp.where(jnp.logical_and(key >= qry, has_prev), 0.0, NEG).astype(BF16)

    def scores(q, k, bias):
        return lax.dot_general(jnp.concatenate([q, eye], axis=1), jnp.concatenate([k, bias], axis=1), _NT,
                               preferred_element_type=F32)

    def prev_refs(g, r, n, cur_ref, prv_ref, hs):
        if n > 0:
            return cur_ref[r, (n - 1) * DIL_SPAN:n * DIL_SPAN, hs]
        return prv_ref[r, :, hs]

    for hl in range(hpb):
        hs = slice(hl * DIL_HD, (hl + 1) * DIL_HD)
        for g in range(ng):
            d = DIL_RATES[g]
            blocks = [(r, n) for r in range(d) for n in range(span // (DIL_SPAN * d))]
            for i, (r, n) in enumerate(blocks):
                rows = slice(n * DIL_SPAN, (n + 1) * DIL_SPAN)
                q = q_refs[g][r, rows, hs]
                kp = prev_refs(g, r, n, kc_refs[g], kp_refs[g], hs)
                sc_s[i] = scores(q, kc_refs[g][r, rows, hs], bias_c)
                sp_s[i] = scores(q, kp, bias_p if n > 0 else bias_p0)
            for i, (r, n) in enumerate(blocks):
                s_c = sc_s[i]
                s_p = sp_s[i]
                m = jnp.max(jnp.maximum(s_c, s_p), axis=-1, keepdims=True)
                p_c = jnp.exp2(s_c - m)
                p_p = jnp.exp2(s_p - m)
                den = jnp.sum(p_c + p_p, axis=-1, keepdims=True)
                pc_s[i] = p_c.astype(BF16)
                pp_s[i] = p_p.astype(BF16)
                dst = slice(n * DIL_SPAN, (n + 1) * DIL_SPAN) if d == 1 else pl.ds(n * DIL_SPAN * d + r, DIL_SPAN, stride=d)
                m_s[g, dst, :] = jnp.broadcast_to(m, (DIL_SPAN, LANE))
                l_s[g, dst, :] = jnp.broadcast_to(den, (DIL_SPAN, LANE))
            for i, (r, n) in enumerate(blocks):
                rows = slice(n * DIL_SPAN, (n + 1) * DIL_SPAN)
                vp = prev_refs(g, r, n, vc_refs[g], vp_refs[g], hs)
                acc = jnp.dot(pc_s[i], vc_refs[g][r, rows, hs], preferred_element_type=F32)
                acc = acc + jnp.dot(pp_s[i], vp, preferred_element_type=F32)
                dst = rows if d == 1 else pl.ds(n * DIL_SPAN * d + r, DIL_SPAN, stride=d)
                acc_s[g, dst, :] = acc
        mr = 64
        for c in range(span // mr):
            rs = slice(c * mr, (c + 1) * mr)
            ms = [m_s[g, rs, :] for g in range(ng)]
            m_all = jnp.maximum(jnp.maximum(ms[0], ms[1]), ms[2])
            es = [jnp.exp2(m - m_all) for m in ms]
            tot = es[0] * l_s[0, rs, :] + es[1] * l_s[1, rs, :] + es[2] * l_s[2, rs, :]
            num = es[0] * acc_s[0, rs, :] + es[1] * acc_s[1, rs, :] + es[2] * acc_s[2, rs, :]
            o_ref[rs, hs] = (num / tot * _silu(gate_ref[rs, hs].astype(F32))).astype(o_ref.dtype)


def _dilated_prompt(qkvs, gate, batch, seq):
    span, hpb = DIL_PROMPT_SPAN, DIL_PROMPT_HEADS
    wc = hpb * DIL_HD
    kcol, vcol = DIL_W // wc, 2 * DIL_W // wc
    nsp = seq // span
    views = [a.reshape(batch, d, seq // d, 3 * DIL_W) for a, d in zip(qkvs, DIL_RATES)]
    q_specs, kc_specs, kp_specs, vc_specs, vp_specs = [], [], [], [], []
    for d in DIL_RATES:
        cur = (None, d, span // d, wc)
        prv = (None, d, DIL_SPAN, wc)
        nblk = span // (d * DIL_SPAN)
        q_specs.append(pl.BlockSpec(cur, lambda b, s, h: (b, 0, s, h)))
        kc_specs.append(pl.BlockSpec(cur, lambda b, s, h: (b, 0, s, kcol + h)))
        vc_specs.append(pl.BlockSpec(cur, lambda b, s, h: (b, 0, s, vcol + h)))
        kp_specs.append(pl.BlockSpec(prv, lambda b, s, h, nblk=nblk: (b, 0, jnp.maximum(s * nblk - 1, 0), kcol + h)))
        vp_specs.append(pl.BlockSpec(prv, lambda b, s, h, nblk=nblk: (b, 0, jnp.maximum(s * nblk - 1, 0), vcol + h)))
    rows_spec = pl.BlockSpec((span, wc), lambda b, s, h: (b * nsp + s, h))
    return pl.pallas_call(
        functools.partial(_dil_prompt_body, span=span, hpb=hpb),
        grid=(batch, nsp, DIL_W // wc),
        in_specs=q_specs + kc_specs + kp_specs + vc_specs + vp_specs + [rows_spec],
        out_specs=rows_spec,
        out_shape=jax.ShapeDtypeStruct((batch * seq, DIL_W), BF16),
        scratch_shapes=([pltpu.VMEM((DIL_GROUPS, span, LANE), F32)] * 3
                        + [pltpu.VMEM((span // DIL_SPAN, DIL_SPAN, DIL_SPAN), F32)] * 2
                        + [pltpu.VMEM((span // DIL_SPAN, DIL_SPAN, DIL_SPAN), BF16)] * 2),
        compiler_params=_params(3),
        name="dilated_prompt",
    )(*(views * 5), gate)


def _dil_sample_body(qkv0_ref, qkv1_ref, qkv2_ref, gate_ref, c0_ref, c1_ref, c2_ref, o_ref, *, n_new):
    t = pl.program_id(1)

    @pl.when(t == 0)
    def _():
        o_ref[...] = jnp.zeros_like(o_ref)

    key_row = lax.broadcasted_iota(jnp.int32, (DIL_SPAN, 1, 1), 0)
    ms, dens, accs = [], [], []
    for g, (qkv_ref, c_ref) in enumerate(zip((qkv0_ref, qkv1_ref, qkv2_ref), (c0_ref, c1_ref, c2_ref))):
        k_heads = slice(DIL_HEADS, 2 * DIL_HEADS)
        v_heads = slice(2 * DIL_HEADS, 3 * DIL_HEADS)
        q = qkv_ref[t, 0:DIL_HEADS, :]
        s = jnp.sum(c_ref[:, 0] * q[None], axis=-1, keepdims=True)
        if DIL_RATES[g] == 1:
            s = jnp.where(key_row >= t, s, NEG)
            new_rows = [(tn, tn <= t) for tn in range(n_new)]
        else:
            new_rows = [(t, None)]
        s_new = []
        for tn, valid in new_rows:
            sn = jnp.sum(qkv_ref[tn, k_heads, :] * q, axis=-1, keepdims=True)
            s_new.append(sn if valid is None else jnp.where(valid, sn, NEG))
        m = jnp.max(s, axis=0)
        for sn in s_new:
            m = jnp.maximum(m, sn)
        p = jnp.exp2(s - m[None])
        den = jnp.sum(p, axis=0)
        acc = jnp.sum(p * c_ref[:, 1], axis=0)
        for (tn, _), sn in zip(new_rows, s_new):
            pn = jnp.exp2(sn - m)
            den = den + pn
            acc = acc + pn * qkv_ref[tn, v_heads, :]
        ms.append(m)
        dens.append(den)
        accs.append(acc)
    m_all = jnp.maximum(jnp.maximum(ms[0], ms[1]), ms[2])
    es = [jnp.exp2(m - m_all) for m in ms]
    tot = es[0] * dens[0] + es[1] * dens[1] + es[2] * dens[2]
    merged = (es[0] * accs[0] + es[1] * accs[1] + es[2] * accs[2]) / tot
    o_ref[t] = merged * _silu(gate_ref[t])


def _dilated_sample(qkvs, gate, caches, layer, nb, t):
    rows = nb * SAMPLE_PAD
    views, specs = [], []
    for g, d in enumerate(DIL_RATES):
        c = caches[g]
        assert c.shape[2] == DIL_SPAN * d and t <= d * (1 if d > 1 else DIL_SPAN)
        views.append(c.reshape(c.shape[0], nb, DIL_SPAN, d, 2, DIL_HEADS, DIL_HD))
        blk = (None, None, DIL_SPAN, None, 2, DIL_HEADS, DIL_HD)
        if d == 1:
            specs.append(pl.BlockSpec(blk, lambda b, i: (layer, b, 0, 0, 0, 0, 0)))
        else:
            specs.append(pl.BlockSpec(blk, lambda b, i: (layer, b, 0, i, 0, 0, 0)))
    rows3 = pl.BlockSpec((SAMPLE_PAD, 3 * DIL_HEADS, DIL_HD), lambda b, i: (b, 0, 0))
    rows1 = pl.BlockSpec((SAMPLE_PAD, DIL_HEADS, DIL_HD), lambda b, i: (b, 0, 0))
    out = pl.pallas_call(
        functools.partial(_dil_sample_body, n_new=t),
        grid=(nb, t),
        in_specs=[rows3, rows3, rows3, rows1] + specs,
        out_specs=rows1,
        out_shape=jax.ShapeDtypeStruct((rows, DIL_HEADS, DIL_HD), F32),
        compiler_params=_params(2),
        name="dilated_sample",
    )(*(a.reshape(rows, 3 * DIL_HEADS, DIL_HD) for a in qkvs), gate.reshape(rows, DIL_HEADS, DIL_HD), *views)
    return out.reshape(rows, DIL_W)


def _cache_shift_body(a_ref, nxt_ref, new_ref, o_ref, *, t, tb):
    last = pl.program_id(1) == pl.num_programs(1) - 1
    o_ref[0:tb - t] = a_ref[t:tb]

    @pl.when(last)
    def _():
        o_ref[tb - t:tb] = new_ref[...]

    @pl.when(jnp.logical_not(last))
    def _():
        o_ref[tb - t:tb] = nxt_ref[...]


def _cache_shift(cache, new, layer, nb, t):
    wb = cache.shape[2]
    tail = cache.shape[3:]
    tb = min(wb, 256)
    assert wb % tb == 0 and tb % t == 0
    zeros = (0,) * len(tail)
    return pl.pallas_call(
        functools.partial(_cache_shift_body, t=t, tb=tb),
        grid=(nb, wb // tb),
        in_specs=[pl.BlockSpec((None, None, tb) + tail, lambda b, i: (layer, b, i) + zeros),
                  pl.BlockSpec((None, None, t) + tail,
                               lambda b, i: (layer, b, jnp.minimum((i + 1) * (tb // t), wb // t - 1)) + zeros),
                  pl.BlockSpec((None, t) + tail, lambda b, i: (b, 0) + zeros)],
        out_specs=pl.BlockSpec((None, tb) + tail, lambda b, i: (b, i) + zeros),
        out_shape=jax.ShapeDtypeStruct(cache.shape[1:], cache.dtype),
        compiler_params=_params(2),
        name="cache_shift",
    )(cache, cache, new)


def _gmlp_body(u_ref, v_ref, gate_ref, lng_ref, lnb_ref, wm_ref, bs_ref, o_ref, *vn_ref):
    v = v_ref[...].astype(F32)
    mu = jnp.mean(v, axis=-1, keepdims=True)
    xc = v - mu
    vn = xc * lax.rsqrt(jnp.mean(xc * xc, axis=-1, keepdims=True) + EPS) * lng_ref[...] + lnb_ref[...]
    if vn_ref:
        vn_ref[0][...] = vn
    vb = vn.astype(BF16)
    for g in range(GM_GROUPS):
        gs = slice(g * GM_GD, (g + 1) * GM_GD)
        mixed = jnp.dot(wm_ref[g], vb[:, gs], preferred_element_type=F32) + bs_ref[g]
        o_ref[:, gs] = (u_ref[:, gs].astype(F32) * mixed * _silu(gate_ref[:, gs].astype(F32))).astype(o_ref.dtype)


def _gmlp_core(uvg, ln_g3, ln_b3, layer, wm, bs, want_vn):
    rows = uvg.shape[0]
    c = wm.shape[1]
    row = lambda i: (i, 0)
    out_specs = [pl.BlockSpec((c, GM_WIDTH), row)]
    out_shape = [jax.ShapeDtypeStruct((rows, GM_WIDTH), BF16)]
    if want_vn:
        out_specs.append(pl.BlockSpec((c, GM_WIDTH), row))
        out_shape.append(jax.ShapeDtypeStruct((rows, GM_WIDTH), F32))
    res = pl.pallas_call(
        _gmlp_body,
        grid=(rows // c,),
        in_specs=[pl.BlockSpec((c, GM_WIDTH), row),
                  pl.BlockSpec((c, GM_WIDTH), lambda i: (i, 1)),
                  pl.BlockSpec((c, GM_WIDTH), lambda i: (i, 2)),
                  pl.BlockSpec((None, 1, GM_WIDTH), lambda i: (layer, 0, 0)),
                  pl.BlockSpec((None, 1, GM_WIDTH), lambda i: (layer, 0, 0)),
                  pl.BlockSpec((GM_GROUPS, c, c), lambda i: (0, 0, 0)),
                  pl.BlockSpec((GM_GROUPS, c, 1), lambda i: (0, 0, 0))],
        out_specs=out_specs,
        out_shape=out_shape,
        compiler_params=_params(1),
        name="gmlp_core",
    )(uvg, uvg, uvg, ln_g3, ln_b3, wm, bs)
    return res


class _Stream:
    def __init__(self, batch, t, t_pad, pos0, tm, tm_proj, act_dtype, tiled):
        self.batch, self.t, self.t_pad, self.act, self.tiled = batch, t, t_pad, act_dtype, tiled
        self.tm, self.tm_proj = tm, tm_proj
        self.rows = batch * t_pad
        assert tiled or tm == tm_proj == self.rows
        self.row = (lambda i: i) if tiled else (lambda i: 0)
        pos = pos0 + np.arange(t_pad)
        cos, sin = _rope_tables(pos, RET_DK // 2)
        self.ret_rope = (cos, sin)
        cos, sin = _rope_tables(pos, DIL_HD // 2)
        self.dil_rope = (np.concatenate([cos, cos], axis=-1), np.concatenate([-sin, sin], axis=-1))
        if t_pad < tm_proj:
            rep = tm_proj // t_pad
            self.ret_rope = tuple(np.tile(a, (rep, 1)) for a in self.ret_rope)
            self.dil_rope = tuple(np.tile(a, (rep, 1)) for a in self.dil_rope)

    def proj_tile(self, h):
        return self.tm_proj if h.ssq is None else self.tm

    def rope_specs(self, tm):
        nt, row = max(self.t_pad // tm, 1), self.row
        return (pl.BlockSpec((tm, LANE), lambda j, i: (row(i) % nt, 0)),) * 2

    def tile_spec(self, tm, tn):
        row = self.row
        return pl.BlockSpec((tm, tn), lambda j, i: (row(i), j))

    def dil_rope_residue_major(self, d):
        t = self.t_pad
        return tuple(a.reshape(t // d, d, LANE).transpose(1, 0, 2).reshape(t, LANE) for a in self.dil_rope)


class _Normed(NamedTuple):
    x: jax.Array
    ssq: object = None


def _proj(name, sts, hins, w3, layer, n0, n, tn, epilogue, extras=None, extra_specs=None):
    ops = []
    for k, (st, h) in enumerate(zip(sts, hins)):
        ops.append(_Operand(h.x, st.proj_tile(h), st.tiled, epilogue, extras[k] if extras else (),
                            extra_specs[k] if extra_specs else (), (("tile", st.act),), h.ssq))
    return [r[0] for r in _matmul(w3, layer, n0, n, tn, ops, name)]


def _out_and_ple(sts, ys, w_out3, jl, xs, i, p3s, ple_w, ple_g3, ple_w_gate, next_g3, next_layer):
    gain = lambda l, tn: pl.BlockSpec((None, 1, tn), lambda j, m: (l, 0, j))
    stream_outs = (("tile", F32), ("tile", BF16), ("ssq",))
    tn = 1024
    half = lambda st: st.tm // 2 if st.tiled else st.tm
    ops = [_Operand(y, half(st), st.tiled, _ep_residual, (x, ple_g3), (st.tile_spec(half(st), tn), gain(i, tn)),
                    stream_outs) for st, y, x in zip(sts, ys, xs)]
    mids = _matmul(w_out3, jl, 0, D_MODEL, tn, ops, "out_proj", single_buffer_w=w_out3.shape[1] > D_MODEL)
    ops = []
    for st, (x1, xg1, ssq1), p3 in zip(sts, mids, p3s):
        row = st.row
        tm = half(st)
        extras = [x1, p3, ple_w]
        specs = [st.tile_spec(tm, tn),
                 pl.BlockSpec((None, tm, PLE_DIM), lambda j, m, row=row: (i, row(m), 0)),
                 pl.BlockSpec((None, PLE_DIM, tn), lambda j, m: (i, 0, j))]
        if next_layer is not None:
            extras.append(next_g3)
            specs.append(gain(next_layer, tn))
        ops.append(_Operand(xg1, tm, st.tiled, _ep_ple, tuple(extras), tuple(specs),
                            stream_outs if next_layer is not None else (("tile", F32),), ssq1))
    res = _matmul(ple_w_gate, i, 0, D_MODEL, tn, ops, "ple")
    x_new = [r[0] for r in res]
    h_next = [_Normed(r[1], r[2]) for r in res] if next_layer is not None else None
    return x_new, h_next


def _retention_proj(sts, hins, w_in, jl):
    tn = 1024
    tiles = RET_QK // tn
    epilogues = ((0, functools.partial(_ep_rope_ret, scale=1.0)),
                 (tiles, functools.partial(_ep_rope_ret, scale=RET_DK ** -0.5)),
                 (2 * tiles, _ep_plain))
    ropes = [st.ret_rope for st in sts]
    rope_specs = [st.rope_specs(st.proj_tile(h)) for st, h in zip(sts, hins)]
    return _proj("ret_in", sts, hins, w_in, jl, 0, 2 * RET_QK + 2 * RET_V, tn, epilogues, ropes, rope_specs)


def _dilated_proj(sts, hs_by_rate, h_sample, w_in, jl):
    prm, smp = sts
    tn = 1024
    tiles = DIL_W // tn
    epilogues = ((0, functools.partial(_ep_rope_dil, scale=DIL_Q_SCALE)),
                 (tiles, functools.partial(_ep_rope_dil, scale=1.0)),
                 (2 * tiles, _ep_plain))
    qkvs = []
    for g, d in enumerate(DIL_RATES):
        rope = prm.dil_rope if d == 1 else prm.dil_rope_residue_major(d)
        qkvs.append(_proj(f"dil_qkv_g{g}", sts, (hs_by_rate[g], h_sample), w_in, jl, g * 3 * DIL_W, 3 * DIL_W, tn,
                          epilogues, (rope, smp.dil_rope),
                          (prm.rope_specs(prm.proj_tile(hs_by_rate[g])), smp.rope_specs(smp.proj_tile(h_sample)))))
    gates = _proj("dil_gate", sts, (hs_by_rate[0], h_sample), w_in, jl, DIL_GROUPS * 3 * DIL_W, DIL_W, tn, _ep_plain)
    return qkvs, gates


def _window_prompt_body(k_ref, v_ref, o_ref, *, d):
    mb = k_ref.shape[1]
    for r in range(d):
        for kv, ref in enumerate((k_ref, v_ref)):
            for h in range(DIL_HEADS):
                val = ref[r, :, h * DIL_HD:(h + 1) * DIL_HD].astype(F32)
                o_ref[pl.ds((r * 2 + kv) * DIL_HEADS + h, mb, stride=d * 2 * DIL_HEADS), :] = val


def _window_rows_prompt(qkv, st, d):
    mb = min(DIL_SPAN, 256 // d)
    per_row = 2 * DIL_HEADS
    first = (st.t // d - DIL_SPAN) // mb
    view = qkv.reshape(st.batch, d, st.t // d, 3 * DIL_W)
    out = pl.pallas_call(
        functools.partial(_window_prompt_body, d=d),
        grid=(st.batch, DIL_SPAN // mb),
        in_specs=[pl.BlockSpec((None, d, mb, DIL_W), lambda b, i: (b, 0, first + i, 1)),
                  pl.BlockSpec((None, d, mb, DIL_W), lambda b, i: (b, 0, first + i, 2))],
        out_specs=pl.BlockSpec((None, mb * d * per_row, DIL_HD), lambda b, i: (b, i, 0)),
        out_shape=jax.ShapeDtypeStruct((st.batch, DIL_SPAN * d * per_row, DIL_HD), F32),
        compiler_params=_params(2),
        name="window_prompt",
    )(view, view)
    return out.reshape(st.batch, DIL_SPAN * d, 2, DIL_HEADS, DIL_HD)


def _window_rows_sample(qkv, st):
    kv = qkv.reshape(st.batch, st.t_pad, 3, DIL_HEADS, DIL_HD)[:, :st.t, 1:]
    return kv.astype(F32)


def kernel(x_prompt, x_sample, state_ret, cache_win_g0, cache_win_g1, cache_win_g2, p_prompt, p_sample, norm_g,
           ret_w_in, ret_w_out, dil_w_in, dil_w_out, gm_w_in, gm_ln_g, gm_ln_b, gm_w_s, gm_b_s, gm_w_out, ple_w,
           ple_norm_g, ple_w_gate, final_norm_g):
    depth = norm_g.shape[0]
    bp, sp, _ = x_prompt.shape
    bs_, ts, _ = x_sample.shape
    assert ts <= SAMPLE_PAD and ts % GM_CHUNK != 0 and sp % (DIL_SPAN * max(DIL_RATES)) == 0
    caches = (cache_win_g0, cache_win_g1, cache_win_g2)

    prm = _Stream(bp, sp, sp, 0, 1024, 2048, BF16, True)
    smp = _Stream(bs_, ts, SAMPLE_PAD, PAST_LEN, bs_ * SAMPLE_PAD, bs_ * SAMPLE_PAD, F32, False)
    sts = (prm, smp)

    pad_t = ((0, 0), (0, SAMPLE_PAD - ts), (0, 0))
    xp = x_prompt.reshape(prm.rows, D_MODEL)
    xs = jnp.pad(x_sample, pad_t).reshape(smp.rows, D_MODEL)
    p3s = (p_prompt.astype(BF16).reshape(depth, prm.rows, PLE_DIM),
           jnp.pad(p_sample, ((0, 0),) + pad_t).astype(BF16).reshape(depth, smp.rows, PLE_DIM))

    norm_g3 = norm_g[:, None, :]
    ple_g3 = ple_norm_g[:, None, :]
    ln_g3 = gm_ln_g[:, None, :]
    ln_b3 = gm_ln_b[:, None, :]
    fin_g3 = final_norm_g[None, None, :]

    ret_p, ret_s, gm_s = [], [], []
    win_p = [[], [], []]
    win_s = [[], [], []]
    hins = [_Normed(_rmsnorm(xp, norm_g3, 0, BF16, 256)), _Normed(_rmsnorm(xs, norm_g3, 0, BF16, smp.rows))]
    for i in range(depth):
        kind, jl = i % 3, i // 3
        if kind == 0:
            qkvg = _retention_proj(sts, hins, ret_w_in, jl)
            yp, sp_new = _retention_prompt(qkvg[0], bp, sp)
            ys, ss_new = _retention_sample(qkvg[1], state_ret, jl, bs_, ts)
            ret_p.append(sp_new)
            ret_s.append(ss_new)
            w_out = ret_w_out
        elif kind == 1:
            hp_orders = [_Normed(a) for a in _rmsnorm_orders(xp, norm_g3, i, bp, sp, DIL_RATES[1:], 256)]
            h_sample = _Normed(_rmsnorm(xs, norm_g3, i, BF16, smp.rows))
            qkvs, gates = _dilated_proj(sts, hp_orders, h_sample, dil_w_in, jl)
            yp = _dilated_prompt([a[0] for a in qkvs], gates[0], bp, sp)
            ys = _dilated_sample([a[1] for a in qkvs], gates[1], caches, jl, bs_, ts).astype(BF16)
            for g in range(DIL_GROUPS):
                win_p[g].append(_window_rows_prompt(qkvs[g][0], prm, DIL_RATES[g]))
                win_s[g].append(_cache_shift(caches[g], _window_rows_sample(qkvs[g][1], smp), jl, bs_, ts))
            w_out = dil_w_out
        else:
            tn = 1024
            uvg = _proj("gm_in", sts, hins, gm_w_in, jl, 0, 3 * GM_WIDTH, tn,
                        ((0, _ep_gelu), (2 * GM_WIDTH // tn, _ep_plain)))
            wm_p = jnp.tril(gm_w_s[jl]).astype(BF16)
            bs_p = gm_b_s[jl][:, :, None]
            (yp,) = _gmlp_core(uvg[0], ln_g3, ln_b3, jl, wm_p, bs_p, False)
            wm_t = jnp.pad(jnp.tril(gm_w_s[jl][:, :ts, :ts]), ((0, 0), (0, SAMPLE_PAD - ts), (0, SAMPLE_PAD - ts)))
            wm_s = jnp.einsum("ab,gij->gaibj", jnp.eye(bs_, dtype=F32), wm_t).reshape(GM_GROUPS, smp.rows, smp.rows)
            bs_s = jnp.tile(jnp.pad(gm_b_s[jl][:, :ts], ((0, 0), (0, SAMPLE_PAD - ts))), (1, bs_))[:, :, None]
            ys, vn = _gmlp_core(uvg[1], ln_g3, ln_b3, jl, wm_s.astype(BF16), bs_s, True)
            gm_s.append(vn.reshape(bs_, SAMPLE_PAD, GM_WIDTH)[:, :ts])
            w_out = gm_w_out
        fold_next = i + 1 < depth and (i + 1) % 3 != 1
        (xp, xs), hins = _out_and_ple(sts, (yp, ys.astype(BF16)), w_out, jl, (xp, xs), i, p3s, ple_w, ple_g3,
                                      ple_w_gate, norm_g3, i + 1 if fold_next else None)

    y_prompt = _rmsnorm(xp, fin_g3, 0, F32, 256).reshape(bp, sp, D_MODEL)
    y_sample = _rmsnorm(xs, fin_g3, 0, F32, smp.rows).reshape(bs_, SAMPLE_PAD, D_MODEL)[:, :ts]
    return (y_prompt, y_sample,
            jnp.stack(ret_p), jnp.stack(ret_s),
            jnp.stack(win_p[0]), jnp.stack(win_s[0]),
            jnp.stack(win_p[1]), jnp.stack(win_s[1]),
            jnp.stack(win_p[2]), jnp.stack(win_s[2]),
            jnp.stack(gm_s))
```

```python
import functools
from typing import NamedTuple

import jax
import jax.numpy as jnp
import numpy as np
from jax import lax
from jax.experimental import pallas as pl
from jax.experimental.pallas import tpu as pltpu

F32 = jnp.float32
BF16 = jnp.bfloat16

D_MODEL = 2048
PAST_LEN = 16384
PLE_DIM = 256
ROPE_THETA = 10000.0
EPS = 1e-6
NEG = -1e30

RET_HEADS = 8
RET_DK = 256
RET_DV = 512
RET_QK = RET_HEADS * RET_DK
RET_V = RET_HEADS * RET_DV
RET_CHUNK = 128

DIL_RATES = (1, 4, 16)
DIL_GROUPS = 3
DIL_SPAN = 128
DIL_HEADS = 16
DIL_HD = 128
DIL_W = DIL_HEADS * DIL_HD
DIL_Q_SCALE = DIL_HD ** -0.5 * 1.4426950408889634

GM_WIDTH = 2 * D_MODEL
GM_GROUPS = 16
GM_GD = GM_WIDTH // GM_GROUPS
GM_CHUNK = 128

LANE = 128
MXU_COLS = 256
SAMPLE_PAD = 16
VMEM_LIMIT = 56 * 1024 * 1024


def _params(n_axes, vmem=VMEM_LIMIT):
    return pltpu.CompilerParams(dimension_semantics=("arbitrary",) * n_axes, vmem_limit_bytes=vmem)


def _silu(x):
    return x * jax.nn.sigmoid(x)


def _rms_body(x_ref, g_ref, o_ref):
    x = x_ref[...]
    y = x * lax.rsqrt(jnp.mean(x * x, axis=-1, keepdims=True) + EPS)
    o_ref[...] = (y * g_ref[...]).astype(o_ref.dtype)


def _rmsnorm(x, g3, layer, out_dtype, tr):
    r, d = x.shape
    return pl.pallas_call(
        _rms_body,
        grid=(r // tr,),
        in_specs=[pl.BlockSpec((tr, d), lambda i: (i, 0)),
                  pl.BlockSpec((None, 1, d), lambda i: (layer, 0, 0))],
        out_specs=pl.BlockSpec((tr, d), lambda i: (i, 0)),
        out_shape=jax.ShapeDtypeStruct((r, d), out_dtype),
        compiler_params=_params(1),
        name="rmsnorm",
    )(x, g3)


def _rms_orders_body(x_ref, g_ref, perm_ref, o_ref, *perm_out_refs, rates):
    x = x_ref[...]
    y = (x * lax.rsqrt(jnp.mean(x * x, axis=-1, keepdims=True) + EPS) * g_ref[...]).astype(o_ref.dtype)
    o_ref[...] = y
    tr = x.shape[0]
    for k, (p_ref, d) in enumerate(zip(perm_out_refs, rates)):
        yp = jnp.dot(perm_ref[k], y, preferred_element_type=F32).astype(o_ref.dtype)
        for r in range(d):
            p_ref[r] = yp[r * (tr // d):(r + 1) * (tr // d), :]


def _rmsnorm_orders(x, g3, layer, batch, seq, rates, tr):
    r, dm = x.shape
    nt = seq // tr
    rows = jnp.arange(tr, dtype=jnp.int32)
    perms = jnp.stack([(rows[None, :] == (rows % (tr // d) * d + rows // (tr // d))[:, None]) for d in rates])
    out_specs = [pl.BlockSpec((tr, dm), lambda i: (i, 0))]
    out_shape = [jax.ShapeDtypeStruct((r, dm), BF16)]
    for d in rates:
        out_specs.append(pl.BlockSpec((None, d, tr // d, dm), lambda i: (i // nt, 0, i % nt, 0)))
        out_shape.append(jax.ShapeDtypeStruct((batch, d, seq // d, dm), BF16))
    res = pl.pallas_call(
        functools.partial(_rms_orders_body, rates=rates),
        grid=(r // tr,),
        in_specs=[pl.BlockSpec((tr, dm), lambda i: (i, 0)),
                  pl.BlockSpec((None, 1, dm), lambda i: (layer, 0, 0)),
                  pl.BlockSpec((len(rates), tr, tr), lambda i: (0, 0, 0))],
        out_specs=out_specs,
        out_shape=out_shape,
        compiler_params=_params(1),
        name="rmsnorm_orders",
    )(x, g3, perms.astype(BF16))
    return [res[0]] + [a.reshape(r, dm) for a in res[1:]]


class _Operand(NamedTuple):
    x: jax.Array
    tm: int
    tiled: bool
    epilogue: object
    extras: tuple = ()
    extra_specs: tuple = ()
    outs: tuple = (("tile", F32),)
    ssq: object = None


def _mm_body(w_ref, *refs, layout):
    n_scratch = 1 + sum(has_ssq for _, _, has_ssq, _, _ in layout)
    wb_ref = refs[-n_scratch]
    rinv_refs = list(refs[len(refs) - n_scratch + 1:])
    j = pl.program_id(0)
    i = pl.program_id(1)

    @pl.when(i == 0)
    def _():
        wb_ref[...] = w_ref[...].astype(BF16)

    def stream(x_ref, ssq_ref, rinv_ref, tile, extra, outs, epilogue):
        rinv = None
        if ssq_ref is not None:
            @pl.when(j == 0)
            def _():
                mean_sq = jnp.sum(jnp.sum(ssq_ref[...], axis=0), axis=-1, keepdims=True) / x_ref.shape[1]
                rinv_ref[tile] = lax.rsqrt(mean_sq + EPS)

            rinv = rinv_ref[tile]

        def run(ep):
            for c in range(wb_ref.shape[1] // MXU_COLS):
                cs = slice(c * MXU_COLS, (c + 1) * MXU_COLS)
                acc = jnp.dot(x_ref[...], wb_ref[:, cs], preferred_element_type=F32)
                ep(acc if rinv is None else acc * rinv, cs, extra, outs)

        if callable(epilogue):
            run(epilogue)
        else:
            starts = [s for s, _ in epilogue] + [None]
            for (lo, ep), hi in zip(epilogue, starts[1:]):
                pred = j >= lo if hi is None else jnp.logical_and(j >= lo, j < hi)
                pl.when(pred)(functools.partial(run, ep))

    n_in = sum(n for n, _, _, _, _ in layout)
    pos, opos = 0, n_in
    for n, n_out, has_ssq, tiled, epilogue in layout:
        x_ref = refs[pos]
        ssq_ref = refs[pos + 1] if has_ssq else None
        rinv_ref = rinv_refs.pop(0) if has_ssq else None
        extra = refs[pos + 1 + has_ssq:pos + n]
        outs = refs[opos:opos + n_out]
        pos, opos = pos + n, opos + n_out
        body = functools.partial(stream, x_ref, ssq_ref, rinv_ref, i if tiled else 0, extra, outs, epilogue)
        if tiled:
            body()
        else:
            pl.when(i == 0)(body)


def _matmul(w3, layer, n0, n, tn, operands, name, single_buffer_w=False):
    k = w3.shape[1]
    j0 = n0 // tn
    steps = [op.x.shape[0] // op.tm for op in operands if op.tiled]
    w_mode = dict(pipeline_mode=pl.Buffered(1)) if single_buffer_w else {}
    in_specs = [pl.BlockSpec((None, k, tn), lambda j, i: (layer, 0, j + j0), **w_mode)]
    args, out_specs, out_shape, layout = [w3], [], [], []
    scratch = [pltpu.VMEM((k, tn), BF16)]
    for op in operands:
        rows = op.x.shape[0]
        row = (lambda i: i) if op.tiled else (lambda i: 0)
        in_specs.append(pl.BlockSpec((op.tm, k), lambda j, i, row=row: (row(i), 0)))
        args.append(op.x)
        if op.ssq is not None:
            in_specs.append(pl.BlockSpec((op.ssq.shape[0], op.tm, LANE),
                                         lambda j, i, row=row: (0, jnp.where(j == 0, row(i), 0), 0)))
            args.append(op.ssq)
            scratch.append(pltpu.VMEM((rows // op.tm, op.tm, 1), F32))
        in_specs += list(op.extra_specs)
        args += list(op.extras)
        for out in op.outs:
            if out[0] == "tile":
                out_specs.append(pl.BlockSpec((op.tm, tn), lambda j, i, row=row: (row(i), j)))
                out_shape.append(jax.ShapeDtypeStruct((rows, n), out[1]))
            else:
                out_specs.append(pl.BlockSpec((None, op.tm, LANE), lambda j, i, row=row: (j, row(i), 0)))
                out_shape.append(jax.ShapeDtypeStruct((n // tn, rows, LANE), F32))
        layout.append((1 + (op.ssq is not None) + len(op.extras), len(op.outs), op.ssq is not None, op.tiled,
                       op.epilogue))
    res = pl.pallas_call(
        functools.partial(_mm_body, layout=tuple(layout)),
        grid=(n // tn, steps[0]),
        in_specs=in_specs,
        out_specs=out_specs,
        out_shape=out_shape,
        scratch_shapes=scratch,
        compiler_params=_params(2),
        name=name,
    )(*args)
    groups, pos = [], 0
    for op in operands:
        groups.append(tuple(res[pos:pos + len(op.outs)]))
        pos += len(op.outs)
    return groups


def _ep_plain(acc, cs, extra, outs):
    outs[0][:, cs] = acc.astype(outs[0].dtype)


def _ep_gelu(acc, cs, extra, outs):
    a = -2.0 * 1.4426950408889634 * 0.7978845608028654
    z = acc * (a + (a * 0.044715) * (acc * acc))
    outs[0][:, cs] = (acc / (1.0 + jnp.exp2(z))).astype(outs[0].dtype)


def _emit_stream(x_new, cs, g_ref, outs):
    outs[0][:, cs] = x_new
    if len(outs) == 1:
        return
    outs[1][:, cs] = (x_new * g_ref[:, cs]).astype(outs[1].dtype)
    sq = x_new * x_new
    part = sq[:, :LANE] + sq[:, LANE:]
    if cs.start == 0:
        outs[2][...] = part
    else:
        outs[2][...] += part


def _ep_residual(acc, cs, extra, outs):
    _emit_stream(extra[0][:, cs] + acc, cs, extra[1] if len(extra) > 1 else None, outs)


def _ep_rope_ret(acc, cs, extra, outs, *, scale):
    cos = extra[0][...]
    sin = extra[1][...]
    o = outs[0]
    half = RET_DK // 2
    for h in range(acc.shape[1] // RET_DK):
        x1 = acc[:, h * RET_DK:h * RET_DK + half]
        x2 = acc[:, h * RET_DK + half:(h + 1) * RET_DK]
        c0 = cs.start + h * RET_DK
        y1 = x1 * cos - x2 * sin
        y2 = x2 * cos + x1 * sin
        o[:, c0:c0 + half] = (y1 if scale == 1.0 else y1 * scale).astype(o.dtype)
        o[:, c0 + half:c0 + RET_DK] = (y2 if scale == 1.0 else y2 * scale).astype(o.dtype)


def _ep_rope_dil(acc, cs, extra, outs, *, scale):
    cos2 = extra[0][...]
    sin2 = extra[1][...]
    o = outs[0]
    for h in range(acc.shape[1] // DIL_HD):
        x = acc[:, h * DIL_HD:(h + 1) * DIL_HD]
        swapped = pltpu.roll(x, DIL_HD // 2, axis=1)
        c0 = cs.start + h * DIL_HD
        y = x * cos2 + swapped * sin2
        o[:, c0:c0 + DIL_HD] = (y if scale == 1.0 else y * scale).astype(o.dtype)


def _ep_ple(acc, cs, extra, outs):
    pe = jnp.dot(extra[1][...], extra[2][:, cs].astype(BF16), preferred_element_type=F32)
    _emit_stream(extra[0][:, cs] + pe * jax.nn.sigmoid(acc), cs, extra[3] if len(extra) > 3 else None, outs)


def _rope_tables(pos, half):
    inv = ROPE_THETA ** (-np.arange(half, dtype=np.float64) / half)
    ang = pos.astype(np.float64)[:, None] * inv[None, :]
    return np.cos(ang).astype(np.float32), np.sin(ang).astype(np.float32)


def _ret_tables(c, c_pad):
    lg = jnp.log1p(-jnp.exp2(-5.0 - jnp.arange(RET_HEADS, dtype=F32)))
    i = jnp.arange(c_pad, dtype=F32)
    diff = i[:, None] - i[None, :]
    dmat = jnp.where(diff[None] >= 0, jnp.exp(lg[:, None, None] * jnp.maximum(diff, 0.0)[None]), 0.0)
    xi = jnp.exp(lg[:, None] * (i[None, :] + 1.0))[:, :, None]
    zeta = jnp.exp(lg[:, None] * (c - 1.0 - i[None, :]))[:, :, None]
    gc = jnp.broadcast_to(jnp.exp(lg * c)[:, None, None], (RET_HEADS, 1, RET_DV))
    return dmat, xi, zeta, gc


def _ret_finish(o, g):
    on = o * lax.rsqrt(jnp.mean(o * o, axis=-1, keepdims=True) + EPS)
    return on * _silu(g)


_NT = (((1,), (1,)), ((), ()))
_TN = (((0,), (0,)), ((), ()))


def _ret_prompt_body(q_ref, k_ref, v_ref, g_ref, dmat_ref, xi_ref, zeta_ref, gc_ref, o_ref, s_ref,
                     sc_s, kv_s, sb_s, *, c, cb):
    @pl.when(pl.program_id(2) == 0)
    def _():
        s_ref[...] = jnp.zeros_like(s_ref)

    dmat = dmat_ref[...]
    xi = xi_ref[...]
    zeta = zeta_ref[...]
    gc = gc_ref[...]
    for i in range(cb):
        rows = slice(i * c, (i + 1) * c)
        k = k_ref[rows, :]
        sc = lax.dot_general(q_ref[rows, :], k, _NT, preferred_element_type=F32) * dmat
        sc_s[i] = sc.astype(BF16)
        kz = (k.astype(F32) * zeta).astype(BF16)
        kv_s[i] = lax.dot_general(kz, v_ref[rows, :], _TN, preferred_element_type=F32)
    for i in range(cb):
        s = s_ref[...]
        sb_s[i] = s.astype(BF16)
        s_ref[...] = s * gc + kv_s[i]
    for i in range(cb):
        rows = slice(i * c, (i + 1) * c)
        o = jnp.dot(sc_s[i], v_ref[rows, :], preferred_element_type=F32)
        o = o + jnp.dot(q_ref[rows, :], sb_s[i], preferred_element_type=F32) * xi
        o_ref[rows, :] = _ret_finish(o, g_ref[rows, :].astype(F32)).astype(o_ref.dtype)


def _retention_prompt(qkvg, batch, seq):
    c = RET_CHUNK
    cb = 16
    rb = c * cb
    ncb = seq // rb
    dmat, xi, zeta, gc = _ret_tables(c, c)
    row = lambda b, h, i: (b * ncb + i, h)
    tab = lambda b, h, i: (h, 0, 0)
    k0, v0 = RET_QK // RET_DK, 2 * RET_QK // RET_DV
    return pl.pallas_call(
        functools.partial(_ret_prompt_body, c=c, cb=cb),
        grid=(batch, RET_HEADS, ncb),
        in_specs=[pl.BlockSpec((rb, RET_DK), row),
                  pl.BlockSpec((rb, RET_DK), lambda b, h, i: (b * ncb + i, k0 + h)),
                  pl.BlockSpec((rb, RET_DV), lambda b, h, i: (b * ncb + i, v0 + h)),
                  pl.BlockSpec((rb, RET_DV), lambda b, h, i: (b * ncb + i, v0 + RET_HEADS + h)),
                  pl.BlockSpec((None, c, c), tab),
                  pl.BlockSpec((None, c, 1), tab),
                  pl.BlockSpec((None, c, 1), tab),
                  pl.BlockSpec((None, 1, RET_DV), tab)],
        out_specs=[pl.BlockSpec((rb, RET_DV), row),
                   pl.BlockSpec((None, None, RET_DK, RET_DV), lambda b, h, i: (b, h, 0, 0))],
        out_shape=[jax.ShapeDtypeStruct((batch * seq, RET_V), BF16),
                   jax.ShapeDtypeStruct((batch, RET_HEADS, RET_DK, RET_DV), F32)],
        scratch_shapes=[pltpu.VMEM((cb, c, c), BF16), pltpu.VMEM((cb, RET_DK, RET_DV), F32),
                        pltpu.VMEM((cb, RET_DK, RET_DV), BF16)],
        compiler_params=_params(3),
        name="retention_prompt",
    )(qkvg, qkvg, qkvg, qkvg, dmat, xi, zeta, gc)


def _ret_sample_body(q_ref, k_ref, v_ref, g_ref, s0_ref, dmat_ref, xi_ref, zeta_ref, gc_ref, o_ref, s_ref, *, nb):
    q = q_ref[...].astype(BF16)
    k = k_ref[...]
    v = v_ref[...].astype(BF16)
    xi = xi_ref[...]
    gc = gc_ref[...]
    sc = lax.dot_general(q, k.astype(BF16), _NT, preferred_element_type=F32) * dmat_ref[...]
    o = jnp.dot(sc.astype(BF16), v, preferred_element_type=F32)
    kz = (k * zeta_ref[...]).astype(BF16)
    row_batch = lax.broadcasted_iota(jnp.int32, kz.shape, 0) // SAMPLE_PAD
    cross = []
    for b in range(nb):
        s0 = s0_ref[b]
        qb = q[b * SAMPLE_PAD:(b + 1) * SAMPLE_PAD, :]
        cross.append(jnp.dot(qb, s0.astype(BF16), preferred_element_type=F32))
        kzb = jnp.where(row_batch == b, kz, jnp.zeros_like(kz))
        s_ref[b] = s0 * gc + lax.dot_general(kzb, v, _TN, preferred_element_type=F32)
    o = o + jnp.concatenate(cross, axis=0) * xi
    o_ref[...] = _ret_finish(o, g_ref[...]).astype(o_ref.dtype)


def _retention_sample(qkvg, state, layer, nb, t):
    rows = nb * SAMPLE_PAD
    k0, v0 = RET_QK // RET_DK, 2 * RET_QK // RET_DV
    dmat, xi, zeta, gc = _ret_tables(t, SAMPLE_PAD)
    eye = jnp.eye(nb, dtype=F32)
    dbig = jnp.einsum("ab,hij->haibj", eye, dmat).reshape(RET_HEADS, rows, rows)
    xib = jnp.tile(xi, (1, nb, 1))
    zetab = jnp.tile(zeta, (1, nb, 1))
    col = lambda h: (0, h)
    tab = lambda h: (h, 0, 0)
    return pl.pallas_call(
        functools.partial(_ret_sample_body, nb=nb),
        grid=(RET_HEADS,),
        in_specs=[pl.BlockSpec((rows, RET_DK), col),
                  pl.BlockSpec((rows, RET_DK), lambda h: (0, k0 + h)),
                  pl.BlockSpec((rows, RET_DV), lambda h: (0, v0 + h)),
                  pl.BlockSpec((rows, RET_DV), lambda h: (0, v0 + RET_HEADS + h)),
                  pl.BlockSpec((None, nb, None, RET_DK, RET_DV), lambda h: (layer, 0, h, 0, 0)),
                  pl.BlockSpec((None, rows, rows), tab),
                  pl.BlockSpec((None, rows, 1), tab),
                  pl.BlockSpec((None, rows, 1), tab),
                  pl.BlockSpec((None, 1, RET_DV), tab)],
        out_specs=[pl.BlockSpec((rows, RET_DV), col),
                   pl.BlockSpec((nb, None, RET_DK, RET_DV), lambda h: (0, h, 0, 0))],
        out_shape=[jax.ShapeDtypeStruct((rows, RET_V), BF16),
                   jax.ShapeDtypeStruct((nb, RET_HEADS, RET_DK, RET_DV), F32)],
        compiler_params=_params(1),
        name="retention_sample",
    )(qkvg, qkvg, qkvg, qkvg, state, dbig, xib, zetab, gc)


DIL_PROMPT_SPAN = DIL_SPAN * max(DIL_RATES)
DIL_PROMPT_HEADS = 2


def _dil_prompt_body(*refs, span, hpb):
    ng = DIL_GROUPS
    q_refs, kc_refs, kp_refs = refs[0:ng], refs[ng:2 * ng], refs[2 * ng:3 * ng]
    vc_refs, vp_refs = refs[3 * ng:4 * ng], refs[4 * ng:5 * ng]
    gate_ref, o_ref, acc_s, m_s, l_s, sc_s, sp_s, pc_s, pp_s = refs[5 * ng:]
    has_prev = pl.program_id(1) > 0
    key = lax.broadcasted_iota(jnp.int32, (DIL_SPAN, DIL_SPAN), 0)
    qry = lax.broadcasted_iota(jnp.int32, (DIL_SPAN, DIL_SPAN), 1)
    eye = jnp.where(key == qry, 1.0, 0.0).astype(BF16)
    bias_c = jnp.where(key <= qry, 0.0, NEG).astype(BF16)
    bias_p = jnp.where(key >= qry, 0.0, NEG).astype(BF16)
    bias_p0 = jnp.where(jnp.logical_and(key >= qry, has_prev), 0.0, NEG).astype(BF16)

    def scores(q, k, bias):
        return lax.dot_general(jnp.concatenate([q, eye], axis=1), jnp.concatenate([k, bias], axis=1), _NT,
                               preferred_element_type=F32)

    def prev_refs(g, r, n, cur_ref, prv_ref, hs):
        if n > 0:
            return cur_ref[r, (n - 1) * DIL_SPAN:n * DIL_SPAN, hs]
        return prv_ref[r, :, hs]

    for hl in range(hpb):
        hs = slice(hl * DIL_HD, (hl + 1) * DIL_HD)
        for g in range(ng):
            d = DIL_RATES[g]
            blocks = [(r, n) for r in range(d) for n in range(span // (DIL_SPAN * d))]
            for i, (r, n) in enumerate(blocks):
                rows = slice(n * DIL_SPAN, (n + 1) * DIL_SPAN)
                q = q_refs[g][r, rows, hs]
                kp = prev_refs(g, r, n, kc_refs[g], kp_refs[g], hs)
                sc_s[i] = scores(q, kc_refs[g][r, rows, hs], bias_c)
                sp_s[i] = scores(q, kp, bias_p if n > 0 else bias_p0)
            for i, (r, n) in enumerate(blocks):
                s_c = sc_s[i]
                s_p = sp_s[i]
                m = jnp.max(jnp.maximum(s_c, s_p), axis=-1, keepdims=True)
                p_c = jnp.exp2(s_c - m)
                p_p = jnp.exp2(s_p - m)
                den = jnp.sum(p_c + p_p, axis=-1, keepdims=True)
                pc_s[i] = p_c.astype(BF16)
                pp_s[i] = p_p.astype(BF16)
                dst = slice(n * DIL_SPAN, (n + 1) * DIL_SPAN) if d == 1 else pl.ds(n * DIL_SPAN * d + r, DIL_SPAN, stride=d)
                m_s[g, dst, :] = jnp.broadcast_to(m, (DIL_SPAN, LANE))
                l_s[g, dst, :] = jnp.broadcast_to(den, (DIL_SPAN, LANE))
            for i, (r, n) in enumerate(blocks):
                rows = slice(n * DIL_SPAN, (n + 1) * DIL_SPAN)
                vp = prev_refs(g, r, n, vc_refs[g], vp_refs[g], hs)
                acc = jnp.dot(pc_s[i], vc_refs[g][r, rows, hs], preferred_element_type=F32)
                acc = acc + jnp.dot(pp_s[i], vp, preferred_element_type=F32)
                dst = rows if d == 1 else pl.ds(n * DIL_SPAN * d + r, DIL_SPAN, stride=d)
                acc_s[g, dst, :] = acc
        mr = 64
        for c in range(span // mr):
            rs = slice(c * mr, (c + 1) * mr)
            ms = [m_s[g, rs, :] for g in range(ng)]
            m_all = jnp.maximum(jnp.maximum(ms[0], ms[1]), ms[2])
            es = [jnp.exp2(m - m_all) for m in ms]
            tot = es[0] * l_s[0, rs, :] + es[1] * l_s[1, rs, :] + es[2] * l_s[2, rs, :]
            num = es[0] * acc_s[0, rs, :] + es[1] * acc_s[1, rs, :] + es[2] * acc_s[2, rs, :]
            o_ref[rs, hs] = (num / tot * _silu(gate_ref[rs, hs].astype(F32))).astype(o_ref.dtype)


def _dilated_prompt(qkvs, gate, batch, seq):
    span, hpb = DIL_PROMPT_SPAN, DIL_PROMPT_HEADS
    wc = hpb * DIL_HD
    kcol, vcol = DIL_W // wc, 2 * DIL_W // wc
    nsp = seq // span
    views = [a.reshape(batch, d, seq // d, 3 * DIL_W) for a, d in zip(qkvs, DIL_RATES)]
    q_specs, kc_specs, kp_specs, vc_specs, vp_specs = [], [], [], [], []
    for d in DIL_RATES:
        cur = (None, d, span // d, wc)
        prv = (None, d, DIL_SPAN, wc)
        nblk = span // (d * DIL_SPAN)
        q_specs.append(pl.BlockSpec(cur, lambda b, s, h: (b, 0, s, h)))
        kc_specs.append(pl.BlockSpec(cur, lambda b, s, h: (b, 0, s, kcol + h)))
        vc_specs.append(pl.BlockSpec(cur, lambda b, s, h: (b, 0, s, vcol + h)))
        kp_specs.append(pl.BlockSpec(prv, lambda b, s, h, nblk=nblk: (b, 0, jnp.maximum(s * nblk - 1, 0), kcol + h)))
        vp_specs.append(pl.BlockSpec(prv, lambda b, s, h, nblk=nblk: (b, 0, jnp.maximum(s * nblk - 1, 0), vcol + h)))
    rows_spec = pl.BlockSpec((span, wc), lambda b, s, h: (b * nsp + s, h))
    return pl.pallas_call(
        functools.partial(_dil_prompt_body, span=span, hpb=hpb),
        grid=(batch, nsp, DIL_W // wc),
        in_specs=q_specs + kc_specs + kp_specs + vc_specs + vp_specs + [rows_spec],
        out_specs=rows_spec,
        out_shape=jax.ShapeDtypeStruct((batch * seq, DIL_W), BF16),
        scratch_shapes=([pltpu.VMEM((DIL_GROUPS, span, LANE), F32)] * 3
                        + [pltpu.VMEM((span // DIL_SPAN, DIL_SPAN, DIL_SPAN), F32)] * 2
                        + [pltpu.VMEM((span // DIL_SPAN, DIL_SPAN, DIL_SPAN), BF16)] * 2),
        compiler_params=_params(3),
        name="dilated_prompt",
    )(*(views * 5), gate)


def _dil_sample_body(qkv0_ref, qkv1_ref, qkv2_ref, gate_ref, c0_ref, c1_ref, c2_ref, o_ref, *, n_new):
    t = pl.program_id(1)

    @pl.when(t == 0)
    def _():
        o_ref[...] = jnp.zeros_like(o_ref)

    key_row = lax.broadcasted_iota(jnp.int32, (DIL_SPAN, 1, 1), 0)
    ms, dens, accs = [], [], []
    for g, (qkv_ref, c_ref) in enumerate(zip((qkv0_ref, qkv1_ref, qkv2_ref), (c0_ref, c1_ref, c2_ref))):
        k_heads = slice(DIL_HEADS, 2 * DIL_HEADS)
        v_heads = slice(2 * DIL_HEADS, 3 * DIL_HEADS)
        q = qkv_ref[t, 0:DIL_HEADS, :]
        s = jnp.sum(c_ref[:, 0] * q[None], axis=-1, keepdims=True)
        if DIL_RATES[g] == 1:
            s = jnp.where(key_row >= t, s, NEG)
            new_rows = [(tn, tn <= t) for tn in range(n_new)]
        else:
            new_rows = [(t, None)]
        s_new = []
        for tn, valid in new_rows:
            sn = jnp.sum(qkv_ref[tn, k_heads, :] * q, axis=-1, keepdims=True)
            s_new.append(sn if valid is None else jnp.where(valid, sn, NEG))
        m = jnp.max(s, axis=0)
        for sn in s_new:
            m = jnp.maximum(m, sn)
        p = jnp.exp2(s - m[None])
        den = jnp.sum(p, axis=0)
        acc = jnp.sum(p * c_ref[:, 1], axis=0)
        for (tn, _), sn in zip(new_rows, s_new):
            pn = jnp.exp2(sn - m)
            den = den + pn
            acc = acc + pn * qkv_ref[tn, v_heads, :]
        ms.append(m)
        dens.append(den)
        accs.append(acc)
    m_all = jnp.maximum(jnp.maximum(ms[0], ms[1]), ms[2])
    es = [jnp.exp2(m - m_all) for m in ms]
    tot = es[0] * dens[0] + es[1] * dens[1] + es[2] * dens[2]
    merged = (es[0] * accs[0] + es[1] * accs[1] + es[2] * accs[2]) / tot
    o_ref[t] = merged * _silu(gate_ref[t])


def _dilated_sample(qkvs, gate, caches, layer, nb, t):
    rows = nb * SAMPLE_PAD
    views, specs = [], []
    for g, d in enumerate(DIL_RATES):
        c = caches[g]
        assert c.shape[2] == DIL_SPAN * d and t <= d * (1 if d > 1 else DIL_SPAN)
        views.append(c.reshape(c.shape[0], nb, DIL_SPAN, d, 2, DIL_HEADS, DIL_HD))
        blk = (None, None, DIL_SPAN, None, 2, DIL_HEADS, DIL_HD)
        if d == 1:
            specs.append(pl.BlockSpec(blk, lambda b, i: (layer, b, 0, 0, 0, 0, 0)))
        else:
            specs.append(pl.BlockSpec(blk, lambda b, i: (layer, b, 0, i, 0, 0, 0)))
    rows3 = pl.BlockSpec((SAMPLE_PAD, 3 * DIL_HEADS, DIL_HD), lambda b, i: (b, 0, 0))
    rows1 = pl.BlockSpec((SAMPLE_PAD, DIL_HEADS, DIL_HD), lambda b, i: (b, 0, 0))
    out = pl.pallas_call(
        functools.partial(_dil_sample_body, n_new=t),
        grid=(nb, t),
        in_specs=[rows3, rows3, rows3, rows1] + specs,
        out_specs=rows1,
        out_shape=jax.ShapeDtypeStruct((rows, DIL_HEADS, DIL_HD), F32),
        compiler_params=_params(2),
        name="dilated_sample",
    )(*(a.reshape(rows, 3 * DIL_HEADS, DIL_HD) for a in qkvs), gate.reshape(rows, DIL_HEADS, DIL_HD), *views)
    return out.reshape(rows, DIL_W)


def _cache_shift_body(a_ref, nxt_ref, new_ref, o_ref, *, t, tb):
    last = pl.program_id(1) == pl.num_programs(1) - 1
    o_ref[0:tb - t] = a_ref[t:tb]

    @pl.when(last)
    def _():
        o_ref[tb - t:tb] = new_ref[...]

    @pl.when(jnp.logical_not(last))
    def _():
        o_ref[tb - t:tb] = nxt_ref[...]


def _cache_shift(cache, new, layer, nb, t):
    wb = cache.shape[2]
    tail = cache.shape[3:]
    tb = min(wb, 256)
    assert wb % tb == 0 and tb % t == 0
    zeros = (0,) * len(tail)
    return pl.pallas_call(
        functools.partial(_cache_shift_body, t=t, tb=tb),
        grid=(nb, wb // tb),
        in_specs=[pl.BlockSpec((None, None, tb) + tail, lambda b, i: (layer, b, i) + zeros),
                  pl.BlockSpec((None, None, t) + tail,
                               lambda b, i: (layer, b, jnp.minimum((i + 1) * (tb // t), wb // t - 1)) + zeros),
                  pl.BlockSpec((None, t) + tail, lambda b, i: (b, 0) + zeros)],
        out_specs=pl.BlockSpec((None, tb) + tail, lambda b, i: (b, i) + zeros),
        out_shape=jax.ShapeDtypeStruct(cache.shape[1:], cache.dtype),
        compiler_params=_params(2),
        name="cache_shift",
    )(cache, cache, new)


def _gmlp_body(u_ref, v_ref, gate_ref, lng_ref, lnb_ref, wm_ref, bs_ref, o_ref, *vn_ref):
    c = wm_ref.shape[1]
    for k in range(u_ref.shape[0] // c):
        rows = slice(k * c, (k + 1) * c)
        v = v_ref[rows, :].astype(F32)
        mu = jnp.mean(v, axis=-1, keepdims=True)
        xc = v - mu
        vn = xc * lax.rsqrt(jnp.mean(xc * xc, axis=-1, keepdims=True) + EPS) * lng_ref[...] + lnb_ref[...]
        if vn_ref:
            vn_ref[0][rows, :] = vn
        vb = vn.astype(BF16)
        for g in range(GM_GROUPS):
            gs = slice(g * GM_GD, (g + 1) * GM_GD)
            mixed = jnp.dot(wm_ref[g], vb[:, gs], preferred_element_type=F32) + bs_ref[g]
            o_ref[rows, gs] = (u_ref[rows, gs].astype(F32) * mixed
                               * _silu(gate_ref[rows, gs].astype(F32))).astype(o_ref.dtype)


def _gmlp_core(uvg, ln_g3, ln_b3, layer, wm, bs, want_vn):
    rows = uvg.shape[0]
    c = wm.shape[1]
    rb = min(rows, 2 * c)
    row = lambda i: (i, 0)
    out_specs = [pl.BlockSpec((rb, GM_WIDTH), row)]
    out_shape = [jax.ShapeDtypeStruct((rows, GM_WIDTH), BF16)]
    if want_vn:
        out_specs.append(pl.BlockSpec((rb, GM_WIDTH), row))
        out_shape.append(jax.ShapeDtypeStruct((rows, GM_WIDTH), F32))
    res = pl.pallas_call(
        _gmlp_body,
        grid=(rows // rb,),
        in_specs=[pl.BlockSpec((rb, GM_WIDTH), row),
                  pl.BlockSpec((rb, GM_WIDTH), lambda i: (i, 1)),
                  pl.BlockSpec((rb, GM_WIDTH), lambda i: (i, 2)),
                  pl.BlockSpec((None, 1, GM_WIDTH), lambda i: (layer, 0, 0)),
                  pl.BlockSpec((None, 1, GM_WIDTH), lambda i: (layer, 0, 0)),
                  pl.BlockSpec((GM_GROUPS, c, c), lambda i: (0, 0, 0)),
                  pl.BlockSpec((GM_GROUPS, c, 1), lambda i: (0, 0, 0))],
        out_specs=out_specs,
        out_shape=out_shape,
        compiler_params=_params(1),
        name="gmlp_core",
    )(uvg, uvg, uvg, ln_g3, ln_b3, wm, bs)
    return res


class _Stream:
    def __init__(self, batch, t, t_pad, pos0, tm, tm_proj, act_dtype, tiled):
        self.batch, self.t, self.t_pad, self.act, self.tiled = batch, t, t_pad, act_dtype, tiled
        self.tm, self.tm_proj = tm, tm_proj
        self.rows = batch * t_pad
        assert tiled or tm == tm_proj == self.rows
        self.row = (lambda i: i) if tiled else (lambda i: 0)
        pos = pos0 + np.arange(t_pad)
        cos, sin = _rope_tables(pos, RET_DK // 2)
        self.ret_rope = (cos, sin)
        cos, sin = _rope_tables(pos, DIL_HD // 2)
        self.dil_rope = (np.concatenate([cos, cos], axis=-1), np.concatenate([-sin, sin], axis=-1))
        if t_pad < tm_proj:
            rep = tm_proj // t_pad
            self.ret_rope = tuple(np.tile(a, (rep, 1)) for a in self.ret_rope)
            self.dil_rope = tuple(np.tile(a, (rep, 1)) for a in self.dil_rope)

    def proj_tile(self, h):
        return self.tm_proj if h.ssq is None else self.tm

    def rope_specs(self, tm):
        nt, row = max(self.t_pad // tm, 1), self.row
        return (pl.BlockSpec((tm, LANE), lambda j, i: (row(i) % nt, 0)),) * 2

    def tile_spec(self, tm, tn):
        row = self.row
        return pl.BlockSpec((tm, tn), lambda j, i: (row(i), j))

    def dil_rope_residue_major(self, d):
        t = self.t_pad
        return tuple(a.reshape(t // d, d, LANE).transpose(1, 0, 2).reshape(t, LANE) for a in self.dil_rope)


class _Normed(NamedTuple):
    x: jax.Array
    ssq: object = None


def _proj(name, sts, hins, w3, layer, n0, n, tn, epilogue, extras=None, extra_specs=None):
    ops = []
    for k, (st, h) in enumerate(zip(sts, hins)):
        ops.append(_Operand(h.x, st.proj_tile(h), st.tiled, epilogue, extras[k] if extras else (),
                            extra_specs[k] if extra_specs else (), (("tile", st.act),), h.ssq))
    return [r[0] for r in _matmul(w3, layer, n0, n, tn, ops, name)]


def _out_and_ple(sts, ys, w_out3, jl, xs, i, p3s, ple_w, ple_g3, ple_w_gate, next_g3, next_layer):
    gain = lambda l, tn: pl.BlockSpec((None, 1, tn), lambda j, m: (l, 0, j))
    stream_outs = (("tile", F32), ("tile", BF16), ("ssq",))
    tn = 1024
    half = lambda st: st.tm // 2 if st.tiled else st.tm
    ops = [_Operand(y, half(st), st.tiled, _ep_residual, (x, ple_g3), (st.tile_spec(half(st), tn), gain(i, tn)),
                    stream_outs) for st, y, x in zip(sts, ys, xs)]
    mids = _matmul(w_out3, jl, 0, D_MODEL, tn, ops, "out_proj", single_buffer_w=w_out3.shape[1] > D_MODEL)
    ops = []
    for st, (x1, xg1, ssq1), p3 in zip(sts, mids, p3s):
        row = st.row
        tm = half(st)
        extras = [x1, p3, ple_w]
        specs = [st.tile_spec(tm, tn),
                 pl.BlockSpec((None, tm, PLE_DIM), lambda j, m, row=row: (i, row(m), 0)),
                 pl.BlockSpec((None, PLE_DIM, tn), lambda j, m: (i, 0, j))]
        if next_layer is not None:
            extras.append(next_g3)
            specs.append(gain(next_layer, tn))
        ops.append(_Operand(xg1, tm, st.tiled, _ep_ple, tuple(extras), tuple(specs),
                            stream_outs if next_layer is not None else (("tile", F32),), ssq1))
    res = _matmul(ple_w_gate, i, 0, D_MODEL, tn, ops, "ple")
    x_new = [r[0] for r in res]
    h_next = [_Normed(r[1], r[2]) for r in res] if next_layer is not None else None
    return x_new, h_next


def _retention_proj(sts, hins, w_in, jl):
    tn = 1024
    tiles = RET_QK // tn
    epilogues = ((0, functools.partial(_ep_rope_ret, scale=1.0)),
                 (tiles, functools.partial(_ep_rope_ret, scale=RET_DK ** -0.5)),
                 (2 * tiles, _ep_plain))
    ropes = [st.ret_rope for st in sts]
    rope_specs = [st.rope_specs(st.proj_tile(h)) for st, h in zip(sts, hins)]
    return _proj("ret_in", sts, hins, w_in, jl, 0, 2 * RET_QK + 2 * RET_V, tn, epilogues, ropes, rope_specs)


def _dilated_proj(sts, hs_by_rate, h_sample, w_in, jl):
    prm, smp = sts
    tn = 1024
    tiles = DIL_W // tn
    epilogues = ((0, functools.partial(_ep_rope_dil, scale=DIL_Q_SCALE)),
                 (tiles, functools.partial(_ep_rope_dil, scale=1.0)),
                 (2 * tiles, _ep_plain))
    qkvs = []
    for g, d in enumerate(DIL_RATES):
        rope = prm.dil_rope if d == 1 else prm.dil_rope_residue_major(d)
        qkvs.append(_proj(f"dil_qkv_g{g}", sts, (hs_by_rate[g], h_sample), w_in, jl, g * 3 * DIL_W, 3 * DIL_W, tn,
                          epilogues, (rope, smp.dil_rope),
                          (prm.rope_specs(prm.proj_tile(hs_by_rate[g])), smp.rope_specs(smp.proj_tile(h_sample)))))
    gates = _proj("dil_gate", sts, (hs_by_rate[0], h_sample), w_in, jl, DIL_GROUPS * 3 * DIL_W, DIL_W, tn, _ep_plain)
    return qkvs, gates


def _window_prompt_body(k_ref, v_ref, o_ref, *, d):
    mb = k_ref.shape[1]
    for r in range(d):
        for kv, ref in enumerate((k_ref, v_ref)):
            for h in range(DIL_HEADS):
                val = ref[r, :, h * DIL_HD:(h + 1) * DIL_HD].astype(F32)
                o_ref[pl.ds((r * 2 + kv) * DIL_HEADS + h, mb, stride=d * 2 * DIL_HEADS), :] = val


def _window_rows_prompt(qkv, st, d):
    mb = min(DIL_SPAN, 256 // d)
    per_row = 2 * DIL_HEADS
    first = (st.t // d - DIL_SPAN) // mb
    view = qkv.reshape(st.batch, d, st.t // d, 3 * DIL_W)
    out = pl.pallas_call(
        functools.partial(_window_prompt_body, d=d),
        grid=(st.batch, DIL_SPAN // mb),
        in_specs=[pl.BlockSpec((None, d, mb, DIL_W), lambda b, i: (b, 0, first + i, 1)),
                  pl.BlockSpec((None, d, mb, DIL_W), lambda b, i: (b, 0, first + i, 2))],
        out_specs=pl.BlockSpec((None, mb * d * per_row, DIL_HD), lambda b, i: (b, i, 0)),
        out_shape=jax.ShapeDtypeStruct((st.batch, DIL_SPAN * d * per_row, DIL_HD), F32),
        compiler_params=_params(2),
        name="window_prompt",
    )(view, view)
    return out.reshape(st.batch, DIL_SPAN * d, 2, DIL_HEADS, DIL_HD)


def _window_rows_sample(qkv, st):
    kv = qkv.reshape(st.batch, st.t_pad, 3, DIL_HEADS, DIL_HD)[:, :st.t, 1:]
    return kv.astype(F32)


def kernel(x_prompt, x_sample, state_ret, cache_win_g0, cache_win_g1, cache_win_g2, p_prompt, p_sample, norm_g,
           ret_w_in, ret_w_out, dil_w_in, dil_w_out, gm_w_in, gm_ln_g, gm_ln_b, gm_w_s, gm_b_s, gm_w_out, ple_w,
           ple_norm_g, ple_w_gate, final_norm_g):
    depth = norm_g.shape[0]
    bp, sp, _ = x_prompt.shape
    bs_, ts, _ = x_sample.shape
    assert ts <= SAMPLE_PAD and ts % GM_CHUNK != 0 and sp % (DIL_SPAN * max(DIL_RATES)) == 0
    caches = (cache_win_g0, cache_win_g1, cache_win_g2)

    prm = _Stream(bp, sp, sp, 0, 1024, 2048, BF16, True)
    smp = _Stream(bs_, ts, SAMPLE_PAD, PAST_LEN, bs_ * SAMPLE_PAD, bs_ * SAMPLE_PAD, F32, False)
    sts = (prm, smp)

    pad_t = ((0, 0), (0, SAMPLE_PAD - ts), (0, 0))
    xp = x_prompt.reshape(prm.rows, D_MODEL)
    xs = jnp.pad(x_sample, pad_t).reshape(smp.rows, D_MODEL)
    p3s = (p_prompt.astype(BF16).reshape(depth, prm.rows, PLE_DIM),
           jnp.pad(p_sample, ((0, 0),) + pad_t).astype(BF16).reshape(depth, smp.rows, PLE_DIM))

    norm_g3 = norm_g[:, None, :]
    ple_g3 = ple_norm_g[:, None, :]
    ln_g3 = gm_ln_g[:, None, :]
    ln_b3 = gm_ln_b[:, None, :]
    fin_g3 = final_norm_g[None, None, :]

    ret_p, ret_s, gm_s = [], [], []
    win_p = [[], [], []]
    win_s = [[], [], []]
    hins = [_Normed(_rmsnorm(xp, norm_g3, 0, BF16, 512)), _Normed(_rmsnorm(xs, norm_g3, 0, BF16, smp.rows))]
    for i in range(depth):
        kind, jl = i % 3, i // 3
        if kind == 0:
            qkvg = _retention_proj(sts, hins, ret_w_in, jl)
            yp, sp_new = _retention_prompt(qkvg[0], bp, sp)
            ys, ss_new = _retention_sample(qkvg[1], state_ret, jl, bs_, ts)
            ret_p.append(sp_new)
            ret_s.append(ss_new)
            w_out = ret_w_out
        elif kind == 1:
            hp_orders = [_Normed(a) for a in _rmsnorm_orders(xp, norm_g3, i, bp, sp, DIL_RATES[1:], 256)]
            h_sample = _Normed(_rmsnorm(xs, norm_g3, i, BF16, smp.rows))
            qkvs, gates = _dilated_proj(sts, hp_orders, h_sample, dil_w_in, jl)
            yp = _dilated_prompt([a[0] for a in qkvs], gates[0], bp, sp)
            ys = _dilated_sample([a[1] for a in qkvs], gates[1], caches, jl, bs_, ts).astype(BF16)
            for g in range(DIL_GROUPS):
                win_p[g].append(_window_rows_prompt(qkvs[g][0], prm, DIL_RATES[g]))
                win_s[g].append(_cache_shift(caches[g], _window_rows_sample(qkvs[g][1], smp), jl, bs_, ts))
            w_out = dil_w_out
        else:
            tn = 1024
            uvg = _proj("gm_in", sts, hins, gm_w_in, jl, 0, 3 * GM_WIDTH, tn,
                        ((0, _ep_gelu), (2 * GM_WIDTH // tn, _ep_plain)))
            wm_p = jnp.tril(gm_w_s[jl]).astype(BF16)
            bs_p = gm_b_s[jl][:, :, None]
            (yp,) = _gmlp_core(uvg[0], ln_g3, ln_b3, jl, wm_p, bs_p, False)
            wm_t = jnp.pad(jnp.tril(gm_w_s[jl][:, :ts, :ts]), ((0, 0), (0, SAMPLE_PAD - ts), (0, SAMPLE_PAD - ts)))
            wm_s = jnp.einsum("ab,gij->gaibj", jnp.eye(bs_, dtype=F32), wm_t).reshape(GM_GROUPS, smp.rows, smp.rows)
            bs_s = jnp.tile(jnp.pad(gm_b_s[jl][:, :ts], ((0, 0), (0, SAMPLE_PAD - ts))), (1, bs_))[:, :, None]
            ys, vn = _gmlp_core(uvg[1], ln_g3, ln_b3, jl, wm_s.astype(BF16), bs_s, True)
            gm_s.append(vn.reshape(bs_, SAMPLE_PAD, GM_WIDTH)[:, :ts])
            w_out = gm_w_out
        fold_next = i + 1 < depth and (i + 1) % 3 != 1
        (xp, xs), hins = _out_and_ple(sts, (yp, ys.astype(BF16)), w_out, jl, (xp, xs), i, p3s, ple_w, ple_g3,
                                      ple_w_gate, norm_g3, i + 1 if fold_next else None)

    y_prompt = _rmsnorm(xp, fin_g3, 0, F32, 512).reshape(bp, sp, D_MODEL)
    y_sample = _rmsnorm(xs, fin_g3, 0, F32, smp.rows).reshape(bs_, SAMPLE_PAD, D_MODEL)[:, :ts]
    return (y_prompt, y_sample,
            jnp.stack(ret_p), jnp.stack(ret_s),
            jnp.stack(win_p[0]), jnp.stack(win_s[0]),
            jnp.stack(win_p[1]), jnp.stack(win_s[1]),
            jnp.stack(win_p[2]), jnp.stack(win_s[2]),
            jnp.stack(gm_s))
```

```python
import functools
from typing import NamedTuple

import jax
import jax.numpy as jnp
import numpy as np
from jax import lax
from jax.experimental import pallas as pl
from jax.experimental.pallas import tpu as pltpu

F32 = jnp.float32
BF16 = jnp.bfloat16

D_MODEL = 2048
PAST_LEN = 16384
PLE_DIM = 256
ROPE_THETA = 10000.0
EPS = 1e-6
NEG = -1e30

RET_HEADS = 8
RET_DK = 256
RET_DV = 512
RET_QK = RET_HEADS * RET_DK
RET_V = RET_HEADS * RET_DV
RET_CHUNK = 128

DIL_RATES = (1, 4, 16)
DIL_GROUPS = 3
DIL_SPAN = 128
DIL_HEADS = 16
DIL_HD = 128
DIL_W = DIL_HEADS * DIL_HD
DIL_Q_SCALE = DIL_HD ** -0.5 * 1.4426950408889634

GM_WIDTH = 2 * D_MODEL
GM_GROUPS = 16
GM_GD = GM_WIDTH // GM_GROUPS
GM_CHUNK = 128

LANE = 128
MXU_COLS = 256
SAMPLE_PAD = 16
VMEM_LIMIT = 56 * 1024 * 1024


def _params(n_axes, vmem=VMEM_LIMIT):
    return pltpu.CompilerParams(dimension_semantics=("arbitrary",) * n_axes, vmem_limit_bytes=vmem)


def _silu(x):
    return x * jax.nn.sigmoid(x)


def _rms_body(x_ref, g_ref, o_ref):
    x = x_ref[...]
    y = x * lax.rsqrt(jnp.mean(x * x, axis=-1, keepdims=True) + EPS)
    o_ref[...] = (y * g_ref[...]).astype(o_ref.dtype)


def _rmsnorm(x, g3, layer, out_dtype, tr):
    r, d = x.shape
    return pl.pallas_call(
        _rms_body,
        grid=(r // tr,),
        in_specs=[pl.BlockSpec((tr, d), lambda i: (i, 0)),
                  pl.BlockSpec((None, 1, d), lambda i: (layer, 0, 0))],
        out_specs=pl.BlockSpec((tr, d), lambda i: (i, 0)),
        out_shape=jax.ShapeDtypeStruct((r, d), out_dtype),
        compiler_params=_params(1),
        name="rmsnorm",
    )(x, g3)


def _rms_orders_body(x_ref, g_ref, perm_ref, o_ref, *perm_out_refs, rates):
    x = x_ref[...]
    y = (x * lax.rsqrt(jnp.mean(x * x, axis=-1, keepdims=True) + EPS) * g_ref[...]).astype(o_ref.dtype)
    o_ref[...] = y
    tr = x.shape[0]
    for k, (p_ref, d) in enumerate(zip(perm_out_refs, rates)):
        yp = jnp.dot(perm_ref[k], y, preferred_element_type=F32).astype(o_ref.dtype)
        for r in range(d):
            p_ref[r] = yp[r * (tr // d):(r + 1) * (tr // d), :]


def _rmsnorm_orders(x, g3, layer, batch, seq, rates, tr):
    r, dm = x.shape
    nt = seq // tr
    rows = jnp.arange(tr, dtype=jnp.int32)
    perms = jnp.stack([(rows[None, :] == (rows % (tr // d) * d + rows // (tr // d))[:, None]) for d in rates])
    out_specs = [pl.BlockSpec((tr, dm), lambda i: (i, 0))]
    out_shape = [jax.ShapeDtypeStruct((r, dm), BF16)]
    for d in rates:
        out_specs.append(pl.BlockSpec((None, d, tr // d, dm), lambda i: (i // nt, 0, i % nt, 0)))
        out_shape.append(jax.ShapeDtypeStruct((batch, d, seq // d, dm), BF16))
    res = pl.pallas_call(
        functools.partial(_rms_orders_body, rates=rates),
        grid=(r // tr,),
        in_specs=[pl.BlockSpec((tr, dm), lambda i: (i, 0)),
                  pl.BlockSpec((None, 1, dm), lambda i: (layer, 0, 0)),
                  pl.BlockSpec((len(rates), tr, tr), lambda i: (0, 0, 0))],
        out_specs=out_specs,
        out_shape=out_shape,
        compiler_params=_params(1),
        name="rmsnorm_orders",
    )(x, g3, perms.astype(BF16))
    return [res[0]] + [a.reshape(r, dm) for a in res[1:]]


class _Operand(NamedTuple):
    x: jax.Array
    tm: int
    tiled: bool
    epilogue: object
    extras: tuple = ()
    extra_specs: tuple = ()
    outs: tuple = (("tile", F32),)
    ssq: object = None


def _mm_body(w_ref, *refs, layout):
    n_scratch = 1 + sum(has_ssq for _, _, has_ssq, _, _ in layout)
    wb_ref = refs[-n_scratch]
    rinv_refs = list(refs[len(refs) - n_scratch + 1:])
    j = pl.program_id(0)
    i = pl.program_id(1)

    @pl.when(i == 0)
    def _():
        wb_ref[...] = w_ref[...].astype(BF16)

    def stream(x_ref, ssq_ref, rinv_ref, tile, extra, outs, epilogue):
        rinv = None
        if ssq_ref is not None:
            @pl.when(j == 0)
            def _():
                mean_sq = jnp.sum(jnp.sum(ssq_ref[...], axis=0), axis=-1, keepdims=True) / x_ref.shape[1]
                rinv_ref[tile] = lax.rsqrt(mean_sq + EPS)

            rinv = rinv_ref[tile]

        def run(ep):
            for c in range(wb_ref.shape[1] // MXU_COLS):
                cs = slice(c * MXU_COLS, (c + 1) * MXU_COLS)
                acc = jnp.dot(x_ref[...], wb_ref[:, cs], preferred_element_type=F32)
                ep(acc if rinv is None else acc * rinv, cs, extra, outs)

        if callable(epilogue):
            run(epilogue)
        else:
            starts = [s for s, _ in epilogue] + [None]
            for (lo, ep), hi in zip(epilogue, starts[1:]):
                pred = j >= lo if hi is None else jnp.logical_and(j >= lo, j < hi)
                pl.when(pred)(functools.partial(run, ep))

    n_in = sum(n for n, _, _, _, _ in layout)
    pos, opos = 0, n_in
    for n, n_out, has_ssq, tiled, epilogue in layout:
        x_ref = refs[pos]
        ssq_ref = refs[pos + 1] if has_ssq else None
        rinv_ref = rinv_refs.pop(0) if has_ssq else None
        extra = refs[pos + 1 + has_ssq:pos + n]
        outs = refs[opos:opos + n_out]
        pos, opos = pos + n, opos + n_out
        body = functools.partial(stream, x_ref, ssq_ref, rinv_ref, i if tiled else 0, extra, outs, epilogue)
        if tiled:
            body()
        else:
            pl.when(i == 0)(body)


def _matmul(w3, layer, n0, n, tn, operands, name, single_buffer_w=False):
    k = w3.shape[1]
    j0 = n0 // tn
    steps = [op.x.shape[0] // op.tm for op in operands if op.tiled]
    w_mode = dict(pipeline_mode=pl.Buffered(1)) if single_buffer_w else {}
    in_specs = [pl.BlockSpec((None, k, tn), lambda j, i: (layer, 0, j + j0), **w_mode)]
    args, out_specs, out_shape, layout = [w3], [], [], []
    scratch = [pltpu.VMEM((k, tn), BF16)]
    for op in operands:
        rows = op.x.shape[0]
        row = (lambda i: i) if op.tiled else (lambda i: 0)
        in_specs.append(pl.BlockSpec((op.tm, k), lambda j, i, row=row: (row(i), 0)))
        args.append(op.x)
        if op.ssq is not None:
            in_specs.append(pl.BlockSpec((op.ssq.shape[0], op.tm, LANE),
                                         lambda j, i, row=row: (0, jnp.where(j == 0, row(i), 0), 0)))
            args.append(op.ssq)
            scratch.append(pltpu.VMEM((rows // op.tm, op.tm, 1), F32))
        in_specs += list(op.extra_specs)
        args += list(op.extras)
        for out in op.outs:
            if out[0] == "tile":
                out_specs.append(pl.BlockSpec((op.tm, tn), lambda j, i, row=row: (row(i), j)))
                out_shape.append(jax.ShapeDtypeStruct((rows, n), out[1]))
            else:
                out_specs.append(pl.BlockSpec((None, op.tm, LANE), lambda j, i, row=row: (j, row(i), 0)))
                out_shape.append(jax.ShapeDtypeStruct((n // tn, rows, LANE), F32))
        layout.append((1 + (op.ssq is not None) + len(op.extras), len(op.outs), op.ssq is not None, op.tiled,
                       op.epilogue))
    res = pl.pallas_call(
        functools.partial(_mm_body, layout=tuple(layout)),
        grid=(n // tn, steps[0]),
        in_specs=in_specs,
        out_specs=out_specs,
        out_shape=out_shape,
        scratch_shapes=scratch,
        compiler_params=_params(2),
        name=name,
    )(*args)
    groups, pos = [], 0
    for op in operands:
        groups.append(tuple(res[pos:pos + len(op.outs)]))
        pos += len(op.outs)
    return groups


def _ep_plain(acc, cs, extra, outs):
    outs[0][:, cs] = acc.astype(outs[0].dtype)


def _ep_gelu(acc, cs, extra, outs):
    a = -2.0 * 1.4426950408889634 * 0.7978845608028654
    z = acc * (a + (a * 0.044715) * (acc * acc))
    outs[0][:, cs] = (acc / (1.0 + jnp.exp2(z))).astype(outs[0].dtype)


def _emit_stream(x_new, cs, g_ref, outs):
    outs[0][:, cs] = x_new
    if len(outs) == 1:
        return
    outs[1][:, cs] = (x_new * g_ref[:, cs]).astype(outs[1].dtype)
    sq = x_new * x_new
    part = sq[:, :LANE] + sq[:, LANE:]
    if cs.start == 0:
        outs[2][...] = part
    else:
        outs[2][...] += part


def _ep_residual(acc, cs, extra, outs):
    _emit_stream(extra[0][:, cs] + acc, cs, extra[1] if len(extra) > 1 else None, outs)


def _ep_rope_ret(acc, cs, extra, outs, *, scale):
    cos = extra[0][...]
    sin = extra[1][...]
    o = outs[0]
    half = RET_DK // 2
    for h in range(acc.shape[1] // RET_DK):
        x1 = acc[:, h * RET_DK:h * RET_DK + half]
        x2 = acc[:, h * RET_DK + half:(h + 1) * RET_DK]
        c0 = cs.start + h * RET_DK
        y1 = x1 * cos - x2 * sin
        y2 = x2 * cos + x1 * sin
        o[:, c0:c0 + half] = (y1 if scale == 1.0 else y1 * scale).astype(o.dtype)
        o[:, c0 + half:c0 + RET_DK] = (y2 if scale == 1.0 else y2 * scale).astype(o.dtype)


def _ep_rope_dil(acc, cs, extra, outs, *, scale):
    cos2 = extra[0][...]
    sin2 = extra[1][...]
    o = outs[0]
    for h in range(acc.shape[1] // DIL_HD):
        x = acc[:, h * DIL_HD:(h + 1) * DIL_HD]
        swapped = pltpu.roll(x, DIL_HD // 2, axis=1)
        c0 = cs.start + h * DIL_HD
        y = x * cos2 + swapped * sin2
        o[:, c0:c0 + DIL_HD] = (y if scale == 1.0 else y * scale).astype(o.dtype)


def _ep_ple(acc, cs, extra, outs):
    pe = jnp.dot(extra[1][...], extra[2][:, cs].astype(BF16), preferred_element_type=F32)
    _emit_stream(extra[0][:, cs] + pe * jax.nn.sigmoid(acc), cs, extra[3] if len(extra) > 3 else None, outs)


def _rope_tables(pos, half):
    inv = ROPE_THETA ** (-np.arange(half, dtype=np.float64) / half)
    ang = pos.astype(np.float64)[:, None] * inv[None, :]
    return np.cos(ang).astype(np.float32), np.sin(ang).astype(np.float32)


def _ret_tables(c, c_pad):
    lg = jnp.log1p(-jnp.exp2(-5.0 - jnp.arange(RET_HEADS, dtype=F32)))
    i = jnp.arange(c_pad, dtype=F32)
    diff = i[:, None] - i[None, :]
    dmat = jnp.where(diff[None] >= 0, jnp.exp(lg[:, None, None] * jnp.maximum(diff, 0.0)[None]), 0.0)
    xi = jnp.exp(lg[:, None] * (i[None, :] + 1.0))[:, :, None]
    zeta = jnp.exp(lg[:, None] * (c - 1.0 - i[None, :]))[:, :, None]
    gc = jnp.broadcast_to(jnp.exp(lg * c)[:, None, None], (RET_HEADS, 1, RET_DV))
    return dmat, xi, zeta, gc


def _ret_finish(o, g):
    on = o * lax.rsqrt(jnp.mean(o * o, axis=-1, keepdims=True) + EPS)
    return on * _silu(g)


_NT = (((1,), (1,)), ((), ()))
_TN = (((0,), (0,)), ((), ()))


def _ret_prompt_body(q_ref, k_ref, v_ref, g_ref, dmat_ref, xi_ref, zeta_ref, gc_ref, o_ref, s_ref,
                     sc_s, kv_s, sb_s, *, c, cb):
    @pl.when(pl.program_id(2) == 0)
    def _():
        s_ref[...] = jnp.zeros_like(s_ref)

    dmat = dmat_ref[...]
    xi = xi_ref[...]
    zeta = zeta_ref[...]
    gc = gc_ref[...]
    for i in range(cb):
        rows = slice(i * c, (i + 1) * c)
        k = k_ref[rows, :]
        sc = lax.dot_general(q_ref[rows, :], k, _NT, preferred_element_type=F32) * dmat
        sc_s[i] = sc.astype(BF16)
        kz = (k.astype(F32) * zeta).astype(BF16)
        kv_s[i] = lax.dot_general(kz, v_ref[rows, :], _TN, preferred_element_type=F32)
    for i in range(cb):
        s = s_ref[...]
        sb_s[i] = s.astype(BF16)
        s_ref[...] = s * gc + kv_s[i]
    for i in range(cb):
        rows = slice(i * c, (i + 1) * c)
        o = jnp.dot(sc_s[i], v_ref[rows, :], preferred_element_type=F32)
        o = o + jnp.dot(q_ref[rows, :], sb_s[i], preferred_element_type=F32) * xi
        o_ref[rows, :] = _ret_finish(o, g_ref[rows, :].astype(F32)).astype(o_ref.dtype)


def _retention_prompt(qkvg, batch, seq):
    c = RET_CHUNK
    cb = 16
    rb = c * cb
    ncb = seq // rb
    dmat, xi, zeta, gc = _ret_tables(c, c)
    row = lambda b, h, i: (b * ncb + i, h)
    tab = lambda b, h, i: (h, 0, 0)
    k0, v0 = RET_QK // RET_DK, 2 * RET_QK // RET_DV
    return pl.pallas_call(
        functools.partial(_ret_prompt_body, c=c, cb=cb),
        grid=(batch, RET_HEADS, ncb),
        in_specs=[pl.BlockSpec((rb, RET_DK), row),
                  pl.BlockSpec((rb, RET_DK), lambda b, h, i: (b * ncb + i, k0 + h)),
                  pl.BlockSpec((rb, RET_DV), lambda b, h, i: (b * ncb + i, v0 + h)),
                  pl.BlockSpec((rb, RET_DV), lambda b, h, i: (b * ncb + i, v0 + RET_HEADS + h)),
                  pl.BlockSpec((None, c, c), tab),
                  pl.BlockSpec((None, c, 1), tab),
                  pl.BlockSpec((None, c, 1), tab),
                  pl.BlockSpec((None, 1, RET_DV), tab)],
        out_specs=[pl.BlockSpec((rb, RET_DV), row),
                   pl.BlockSpec((None, None, RET_DK, RET_DV), lambda b, h, i: (b, h, 0, 0))],
        out_shape=[jax.ShapeDtypeStruct((batch * seq, RET_V), BF16),
                   jax.ShapeDtypeStruct((batch, RET_HEADS, RET_DK, RET_DV), F32)],
        scratch_shapes=[pltpu.VMEM((cb, c, c), BF16), pltpu.VMEM((cb, RET_DK, RET_DV), F32),
                        pltpu.VMEM((cb, RET_DK, RET_DV), BF16)],
        compiler_params=_params(3),
        name="retention_prompt",
    )(qkvg, qkvg, qkvg, qkvg, dmat, xi, zeta, gc)


def _ret_sample_body(q_ref, k_ref, v_ref, g_ref, s0_ref, dmat_ref, xi_ref, zeta_ref, gc_ref, o_ref, s_ref, *, nb):
    q = q_ref[...].astype(BF16)
    k = k_ref[...]
    v = v_ref[...].astype(BF16)
    xi = xi_ref[...]
    gc = gc_ref[...]
    sc = lax.dot_general(q, k.astype(BF16), _NT, preferred_element_type=F32) * dmat_ref[...]
    o = jnp.dot(sc.astype(BF16), v, preferred_element_type=F32)
    kz = (k * zeta_ref[...]).astype(BF16)
    row_batch = lax.broadcasted_iota(jnp.int32, kz.shape, 0) // SAMPLE_PAD
    cross = []
    for b in range(nb):
        s0 = s0_ref[b]
        qb = q[b * SAMPLE_PAD:(b + 1) * SAMPLE_PAD, :]
        cross.append(jnp.dot(qb, s0.astype(BF16), preferred_element_type=F32))
        kzb = jnp.where(row_batch == b, kz, jnp.zeros_like(kz))
        s_ref[b] = s0 * gc + lax.dot_general(kzb, v, _TN, preferred_element_type=F32)
    o = o + jnp.concatenate(cross, axis=0) * xi
    o_ref[...] = _ret_finish(o, g_ref[...]).astype(o_ref.dtype)


def _retention_sample(qkvg, state, layer, nb, t):
    rows = nb * SAMPLE_PAD
    k0, v0 = RET_QK // RET_DK, 2 * RET_QK // RET_DV
    dmat, xi, zeta, gc = _ret_tables(t, SAMPLE_PAD)
    eye = jnp.eye(nb, dtype=F32)
    dbig = jnp.einsum("ab,hij->haibj", eye, dmat).reshape(RET_HEADS, rows, rows)
    xib = jnp.tile(xi, (1, nb, 1))
    zetab = jnp.tile(zeta, (1, nb, 1))
    col = lambda h: (0, h)
    tab = lambda h: (h, 0, 0)
    return pl.pallas_call(
        functools.partial(_ret_sample_body, nb=nb),
        grid=(RET_HEADS,),
        in_specs=[pl.BlockSpec((rows, RET_DK), col),
                  pl.BlockSpec((rows, RET_DK), lambda h: (0, k0 + h)),
                  pl.BlockSpec((rows, RET_DV), lambda h: (0, v0 + h)),
                  pl.BlockSpec((rows, RET_DV), lambda h: (0, v0 + RET_HEADS + h)),
                  pl.BlockSpec((None, nb, None, RET_DK, RET_DV), lambda h: (layer, 0, h, 0, 0)),
                  pl.BlockSpec((None, rows, rows), tab),
                  pl.BlockSpec((None, rows, 1), tab),
                  pl.BlockSpec((None, rows, 1), tab),
                  pl.BlockSpec((None, 1, RET_DV), tab)],
        out_specs=[pl.BlockSpec((rows, RET_DV), col),
                   pl.BlockSpec((nb, None, RET_DK, RET_DV), lambda h: (0, h, 0, 0))],
        out_shape=[jax.ShapeDtypeStruct((rows, RET_V), BF16),
                   jax.ShapeDtypeStruct((nb, RET_HEADS, RET_DK, RET_DV), F32)],
        compiler_params=_params(1),
        name="retention_sample",
    )(qkvg, qkvg, qkvg, qkvg, state, dbig, xib, zetab, gc)


DIL_PROMPT_SPAN = DIL_SPAN * max(DIL_RATES)
DIL_PROMPT_HEADS = 2


def _dil_prompt_body(*refs, span, hpb):
    ng = DIL_GROUPS
    q_refs, kc_refs, kp_refs = refs[0:ng], refs[ng:2 * ng], refs[2 * ng:3 * ng]
    vc_refs, vp_refs = refs[3 * ng:4 * ng], refs[4 * ng:5 * ng]
    gate_ref, o_ref, acc_s, m_s, l_s, sc_s, sp_s, pc_s, pp_s = refs[5 * ng:]
    has_prev = pl.program_id(1) > 0
    key = lax.broadcasted_iota(jnp.int32, (DIL_SPAN, DIL_SPAN), 0)
    qry = lax.broadcasted_iota(jnp.int32, (DIL_SPAN, DIL_SPAN), 1)
    eye = jnp.where(key == qry, 1.0, 0.0).astype(BF16)
    bias_c = jnp.where(key <= qry, 0.0, NEG).astype(BF16)
    bias_p = jnp.where(key >= qry, 0.0, NEG).astype(BF16)
    bias_p0 = jnp.where(jnp.logical_and(key >= qry, has_prev), 0.0, NEG).astype(BF16)

    def scores(q, k, bias):
        return lax.dot_general(jnp.concatenate([q, eye], axis=1), jnp.concatenate([k, bias], axis=1), _NT,
                               preferred_element_type=F32)

    def prev_refs(g, r, n, cur_ref, prv_ref, hs):
        if n > 0:
            return cur_ref[r, (n - 1) * DIL_SPAN:n * DIL_SPAN, hs]
        return prv_ref[r, :, hs]

    for hl in range(hpb):
        hs = slice(hl * DIL_HD, (hl + 1) * DIL_HD)
        for g in range(ng):
            d = DIL_RATES[g]
            blocks = [(r, n) for r in range(d) for n in range(span // (DIL_SPAN * d))]
            for i, (r, n) in enumerate(blocks):
                rows = slice(n * DIL_SPAN, (n + 1) * DIL_SPAN)
                q = q_refs[g][r, rows, hs]
                kp = prev_refs(g, r, n, kc_refs[g], kp_refs[g], hs)
                sc_s[i] = scores(q, kc_refs[g][r, rows, hs], bias_c)
                sp_s[i] = scores(q, kp, bias_p if n > 0 else bias_p0)
            for i, (r, n) in enumerate(blocks):
                s_c = sc_s[i]
                s_p = sp_s[i]
                m = jnp.max(jnp.maximum(s_c, s_p), axis=-1, keepdims=True)
                p_c = jnp.exp2(s_c - m)
                p_p = jnp.exp2(s_p - m)
                den = jnp.sum(p_c + p_p, axis=-1, keepdims=True)
                pc_s[i] = p_c.astype(BF16)
                pp_s[i] = p_p.astype(BF16)
                dst = slice(n * DIL_SPAN, (n + 1) * DIL_SPAN) if d == 1 else pl.ds(n * DIL_SPAN * d + r, DIL_SPAN, stride=d)
                m_s[g, dst, :] = jnp.broadcast_to(m, (DIL_SPAN, LANE))
                l_s[g, dst, :] = jnp.broadcast_to(den, (DIL_SPAN, LANE))
            for i, (r, n) in enumerate(blocks):
                rows = slice(n * DIL_SPAN, (n + 1) * DIL_SPAN)
                vp = prev_refs(g, r, n, vc_refs[g], vp_refs[g], hs)
                acc = jnp.dot(pc_s[i], vc_refs[g][r, rows, hs], preferred_element_type=F32)
                acc = acc + jnp.dot(pp_s[i], vp, preferred_element_type=F32)
                dst = rows if d == 1 else pl.ds(n * DIL_SPAN * d + r, DIL_SPAN, stride=d)
                acc_s[g, dst, :] = acc
        mr = 64
        for c in range(span // mr):
            rs = slice(c * mr, (c + 1) * mr)
            ms = [m_s[g, rs, :] for g in range(ng)]
            m_all = jnp.maximum(jnp.maximum(ms[0], ms[1]), ms[2])
            es = [jnp.exp2(m - m_all) for m in ms]
            tot = es[0] * l_s[0, rs, :] + es[1] * l_s[1, rs, :] + es[2] * l_s[2, rs, :]
            num = es[0] * acc_s[0, rs, :] + es[1] * acc_s[1, rs, :] + es[2] * acc_s[2, rs, :]
            o_ref[rs, hs] = (num / tot * _silu(gate_ref[rs, hs].astype(F32))).astype(o_ref.dtype)


def _dilated_prompt(qkvs, gate, batch, seq):
    span, hpb = DIL_PROMPT_SPAN, DIL_PROMPT_HEADS
    wc = hpb * DIL_HD
    kcol, vcol = DIL_W // wc, 2 * DIL_W // wc
    nsp = seq // span
    views = [a.reshape(batch, d, seq // d, 3 * DIL_W) for a, d in zip(qkvs, DIL_RATES)]
    q_specs, kc_specs, kp_specs, vc_specs, vp_specs = [], [], [], [], []
    for d in DIL_RATES:
        cur = (None, d, span // d, wc)
        prv = (None, d, DIL_SPAN, wc)
        nblk = span // (d * DIL_SPAN)
        q_specs.append(pl.BlockSpec(cur, lambda b, s, h: (b, 0, s, h)))
        kc_specs.append(pl.BlockSpec(cur, lambda b, s, h: (b, 0, s, kcol + h)))
        vc_specs.append(pl.BlockSpec(cur, lambda b, s, h: (b, 0, s, vcol + h)))
        kp_specs.append(pl.BlockSpec(prv, lambda b, s, h, nblk=nblk: (b, 0, jnp.maximum(s * nblk - 1, 0), kcol + h)))
        vp_specs.append(pl.BlockSpec(prv, lambda b, s, h, nblk=nblk: (b, 0, jnp.maximum(s * nblk - 1, 0), vcol + h)))
    rows_spec = pl.BlockSpec((span, wc), lambda b, s, h: (b * nsp + s, h))
    return pl.pallas_call(
        functools.partial(_dil_prompt_body, span=span, hpb=hpb),
        grid=(batch, nsp, DIL_W // wc),
        in_specs=q_specs + kc_specs + kp_specs + vc_specs + vp_specs + [rows_spec],
        out_specs=rows_spec,
        out_shape=jax.ShapeDtypeStruct((batch * seq, DIL_W), BF16),
        scratch_shapes=([pltpu.VMEM((DIL_GROUPS, span, LANE), F32)] * 3
                        + [pltpu.VMEM((span // DIL_SPAN, DIL_SPAN, DIL_SPAN), F32)] * 2
                        + [pltpu.VMEM((span // DIL_SPAN, DIL_SPAN, DIL_SPAN), BF16)] * 2),
        compiler_params=_params(3),
        name="dilated_prompt",
    )(*(views * 5), gate)


def _dil_sample_body(qkv0_ref, qkv1_ref, qkv2_ref, gate_ref, c0_ref, c1_ref, c2_ref, o_ref, *, n_new):
    t = pl.program_id(1)

    @pl.when(t == 0)
    def _():
        o_ref[...] = jnp.zeros_like(o_ref)

    key_row = lax.broadcasted_iota(jnp.int32, (DIL_SPAN, 1, 1), 0)
    ms, dens, accs = [], [], []
    for g, (qkv_ref, c_ref) in enumerate(zip((qkv0_ref, qkv1_ref, qkv2_ref), (c0_ref, c1_ref, c2_ref))):
        k_heads = slice(DIL_HEADS, 2 * DIL_HEADS)
        v_heads = slice(2 * DIL_HEADS, 3 * DIL_HEADS)
        q = qkv_ref[t, 0:DIL_HEADS, :]
        s = jnp.sum(c_ref[:, 0] * q[None], axis=-1, keepdims=True)
        if DIL_RATES[g] == 1:
            s = jnp.where(key_row >= t, s, NEG)
            new_rows = [(tn, tn <= t) for tn in range(n_new)]
        else:
            new_rows = [(t, None)]
        s_new = []
        for tn, valid in new_rows:
            sn = jnp.sum(qkv_ref[tn, k_heads, :] * q, axis=-1, keepdims=True)
            s_new.append(sn if valid is None else jnp.where(valid, sn, NEG))
        m = jnp.max(s, axis=0)
        for sn in s_new:
            m = jnp.maximum(m, sn)
        p = jnp.exp2(s - m[None])
        den = jnp.sum(p, axis=0)
        acc = jnp.sum(p * c_ref[:, 1], axis=0)
        for (tn, _), sn in zip(new_rows, s_new):
            pn = jnp.exp2(sn - m)
            den = den + pn
            acc = acc + pn * qkv_ref[tn, v_heads, :]
        ms.append(m)
        dens.append(den)
        accs.append(acc)
    m_all = jnp.maximum(jnp.maximum(ms[0], ms[1]), ms[2])
    es = [jnp.exp2(m - m_all) for m in ms]
    tot = es[0] * dens[0] + es[1] * dens[1] + es[2] * dens[2]
    merged = (es[0] * accs[0] + es[1] * accs[1] + es[2] * accs[2]) / tot
    o_ref[t] = merged * _silu(gate_ref[t])


def _dilated_sample(qkvs, gate, caches, layer, nb, t):
    rows = nb * SAMPLE_PAD
    views, specs = [], []
    for g, d in enumerate(DIL_RATES):
        c = caches[g]
        assert c.shape[2] == DIL_SPAN * d and t <= d * (1 if d > 1 else DIL_SPAN)
        views.append(c.reshape(c.shape[0], nb, DIL_SPAN, d, 2, DIL_HEADS, DIL_HD))
        blk = (None, None, DIL_SPAN, None, 2, DIL_HEADS, DIL_HD)
        if d == 1:
            specs.append(pl.BlockSpec(blk, lambda b, i: (layer, b, 0, 0, 0, 0, 0)))
        else:
            specs.append(pl.BlockSpec(blk, lambda b, i: (layer, b, 0, i, 0, 0, 0)))
    rows3 = pl.BlockSpec((SAMPLE_PAD, 3 * DIL_HEADS, DIL_HD), lambda b, i: (b, 0, 0))
    rows1 = pl.BlockSpec((SAMPLE_PAD, DIL_HEADS, DIL_HD), lambda b, i: (b, 0, 0))
    out = pl.pallas_call(
        functools.partial(_dil_sample_body, n_new=t),
        grid=(nb, t),
        in_specs=[rows3, rows3, rows3, rows1] + specs,
        out_specs=rows1,
        out_shape=jax.ShapeDtypeStruct((rows, DIL_HEADS, DIL_HD), F32),
        compiler_params=_params(2),
        name="dilated_sample",
    )(*(a.reshape(rows, 3 * DIL_HEADS, DIL_HD) for a in qkvs), gate.reshape(rows, DIL_HEADS, DIL_HD), *views)
    return out.reshape(rows, DIL_W)


def _cache_shift_body(a_ref, nxt_ref, new_ref, o_ref, *, t, tb):
    last = pl.program_id(1) == pl.num_programs(1) - 1
    o_ref[0:tb - t] = a_ref[t:tb]

    @pl.when(last)
    def _():
        o_ref[tb - t:tb] = new_ref[...]

    @pl.when(jnp.logical_not(last))
    def _():
        o_ref[tb - t:tb] = nxt_ref[...]


def _cache_shift(cache, new, layer, nb, t):
    wb = cache.shape[2]
    tail = cache.shape[3:]
    tb = min(wb, 512)
    assert wb % tb == 0 and tb % t == 0
    zeros = (0,) * len(tail)
    return pl.pallas_call(
        functools.partial(_cache_shift_body, t=t, tb=tb),
        grid=(nb, wb // tb),
        in_specs=[pl.BlockSpec((None, None, tb) + tail, lambda b, i: (layer, b, i) + zeros),
                  pl.BlockSpec((None, None, t) + tail,
                               lambda b, i: (layer, b, jnp.minimum((i + 1) * (tb // t), wb // t - 1)) + zeros),
                  pl.BlockSpec((None, t) + tail, lambda b, i: (b, 0) + zeros)],
        out_specs=pl.BlockSpec((None, tb) + tail, lambda b, i: (b, i) + zeros),
        out_shape=jax.ShapeDtypeStruct(cache.shape[1:], cache.dtype),
        compiler_params=_params(2),
        name="cache_shift",
    )(cache, cache, new)


def _gmlp_body(u_ref, v_ref, gate_ref, lng_ref, lnb_ref, wm_ref, bs_ref, o_ref, *vn_ref):
    c = wm_ref.shape[1]
    for k in range(u_ref.shape[0] // c):
        rows = slice(k * c, (k + 1) * c)
        v = v_ref[rows, :].astype(F32)
        mu = jnp.mean(v, axis=-1, keepdims=True)
        xc = v - mu
        vn = xc * lax.rsqrt(jnp.mean(xc * xc, axis=-1, keepdims=True) + EPS) * lng_ref[...] + lnb_ref[...]
        if vn_ref:
            vn_ref[0][rows, :] = vn
        vb = vn.astype(BF16)
        for g in range(GM_GROUPS):
            gs = slice(g * GM_GD, (g + 1) * GM_GD)
            mixed = jnp.dot(wm_ref[g], vb[:, gs], preferred_element_type=F32) + bs_ref[g]
            o_ref[rows, gs] = (u_ref[rows, gs].astype(F32) * mixed
                               * _silu(gate_ref[rows, gs].astype(F32))).astype(o_ref.dtype)


def _gmlp_core(uvg, ln_g3, ln_b3, layer, wm, bs, want_vn):
    rows = uvg.shape[0]
    c = wm.shape[1]
    rb = min(rows, 2 * c)
    row = lambda i: (i, 0)
    out_specs = [pl.BlockSpec((rb, GM_WIDTH), row)]
    out_shape = [jax.ShapeDtypeStruct((rows, GM_WIDTH), BF16)]
    if want_vn:
        out_specs.append(pl.BlockSpec((rb, GM_WIDTH), row))
        out_shape.append(jax.ShapeDtypeStruct((rows, GM_WIDTH), F32))
    res = pl.pallas_call(
        _gmlp_body,
        grid=(rows // rb,),
        in_specs=[pl.BlockSpec((rb, GM_WIDTH), row),
                  pl.BlockSpec((rb, GM_WIDTH), lambda i: (i, 1)),
                  pl.BlockSpec((rb, GM_WIDTH), lambda i: (i, 2)),
                  pl.BlockSpec((None, 1, GM_WIDTH), lambda i: (layer, 0, 0)),
                  pl.BlockSpec((None, 1, GM_WIDTH), lambda i: (layer, 0, 0)),
                  pl.BlockSpec((GM_GROUPS, c, c), lambda i: (0, 0, 0)),
                  pl.BlockSpec((GM_GROUPS, c, 1), lambda i: (0, 0, 0))],
        out_specs=out_specs,
        out_shape=out_shape,
        compiler_params=_params(1),
        name="gmlp_core",
    )(uvg, uvg, uvg, ln_g3, ln_b3, wm, bs)
    return res


class _Stream:
    def __init__(self, batch, t, t_pad, pos0, tm, tm_proj, act_dtype, tiled):
        self.batch, self.t, self.t_pad, self.act, self.tiled = batch, t, t_pad, act_dtype, tiled
        self.tm, self.tm_proj = tm, tm_proj
        self.rows = batch * t_pad
        assert tiled or tm == tm_proj == self.rows
        self.row = (lambda i: i) if tiled else (lambda i: 0)
        pos = pos0 + np.arange(t_pad)
        cos, sin = _rope_tables(pos, RET_DK // 2)
        self.ret_rope = (cos, sin)
        cos, sin = _rope_tables(pos, DIL_HD // 2)
        self.dil_rope = (np.concatenate([cos, cos], axis=-1), np.concatenate([-sin, sin], axis=-1))
        if t_pad < tm_proj:
            rep = tm_proj // t_pad
            self.ret_rope = tuple(np.tile(a, (rep, 1)) for a in self.ret_rope)
            self.dil_rope = tuple(np.tile(a, (rep, 1)) for a in self.dil_rope)

    def proj_tile(self, h):
        return self.tm_proj if h.ssq is None else self.tm

    def rope_specs(self, tm):
        nt, row = max(self.t_pad // tm, 1), self.row
        return (pl.BlockSpec((tm, LANE), lambda j, i: (row(i) % nt, 0)),) * 2

    def tile_spec(self, tm, tn):
        row = self.row
        return pl.BlockSpec((tm, tn), lambda j, i: (row(i), j))

    def dil_rope_residue_major(self, d):
        t = self.t_pad
        return tuple(a.reshape(t // d, d, LANE).transpose(1, 0, 2).reshape(t, LANE) for a in self.dil_rope)


class _Normed(NamedTuple):
    x: jax.Array
    ssq: object = None


def _proj(name, sts, hins, w3, layer, n0, n, tn, epilogue, extras=None, extra_specs=None):
    ops = []
    for k, (st, h) in enumerate(zip(sts, hins)):
        ops.append(_Operand(h.x, st.proj_tile(h), st.tiled, epilogue, extras[k] if extras else (),
                            extra_specs[k] if extra_specs else (), (("tile", st.act),), h.ssq))
    return [r[0] for r in _matmul(w3, layer, n0, n, tn, ops, name)]


def _out_and_ple(sts, ys, w_out3, jl, xs, i, p3s, ple_w, ple_g3, ple_w_gate, next_g3, next_layer):
    gain = lambda l, tn: pl.BlockSpec((None, 1, tn), lambda j, m: (l, 0, j))
    stream_outs = (("tile", F32), ("tile", BF16), ("ssq",))
    tn = 1024
    half = lambda st: st.tm // 2 if st.tiled else st.tm
    ops = [_Operand(y, half(st), st.tiled, _ep_residual, (x, ple_g3), (st.tile_spec(half(st), tn), gain(i, tn)),
                    stream_outs) for st, y, x in zip(sts, ys, xs)]
    mids = _matmul(w_out3, jl, 0, D_MODEL, tn, ops, "out_proj", single_buffer_w=w_out3.shape[1] > D_MODEL)
    ops = []
    for st, (x1, xg1, ssq1), p3 in zip(sts, mids, p3s):
        row = st.row
        tm = half(st)
        extras = [x1, p3, ple_w]
        specs = [st.tile_spec(tm, tn),
                 pl.BlockSpec((None, tm, PLE_DIM), lambda j, m, row=row: (i, row(m), 0)),
                 pl.BlockSpec((None, PLE_DIM, tn), lambda j, m: (i, 0, j))]
        if next_layer is not None:
            extras.append(next_g3)
            specs.append(gain(next_layer, tn))
        ops.append(_Operand(xg1, tm, st.tiled, _ep_ple, tuple(extras), tuple(specs),
                            stream_outs if next_layer is not None else (("tile", F32),), ssq1))
    res = _matmul(ple_w_gate, i, 0, D_MODEL, tn, ops, "ple")
    x_new = [r[0] for r in res]
    h_next = [_Normed(r[1], r[2]) for r in res] if next_layer is not None else None
    return x_new, h_next


def _retention_proj(sts, hins, w_in, jl):
    tn = 1024
    tiles = RET_QK // tn
    epilogues = ((0, functools.partial(_ep_rope_ret, scale=1.0)),
                 (tiles, functools.partial(_ep_rope_ret, scale=RET_DK ** -0.5)),
                 (2 * tiles, _ep_plain))
    ropes = [st.ret_rope for st in sts]
    rope_specs = [st.rope_specs(st.proj_tile(h)) for st, h in zip(sts, hins)]
    return _proj("ret_in", sts, hins, w_in, jl, 0, 2 * RET_QK + 2 * RET_V, tn, epilogues, ropes, rope_specs)


def _dilated_proj(sts, hs_by_rate, h_sample, w_in, jl):
    prm, smp = sts
    tn = 1024
    tiles = DIL_W // tn
    epilogues = ((0, functools.partial(_ep_rope_dil, scale=DIL_Q_SCALE)),
                 (tiles, functools.partial(_ep_rope_dil, scale=1.0)),
                 (2 * tiles, _ep_plain))
    qkvs = []
    for g, d in enumerate(DIL_RATES):
        rope = prm.dil_rope if d == 1 else prm.dil_rope_residue_major(d)
        qkvs.append(_proj(f"dil_qkv_g{g}", sts, (hs_by_rate[g], h_sample), w_in, jl, g * 3 * DIL_W, 3 * DIL_W, tn,
                          epilogues, (rope, smp.dil_rope),
                          (prm.rope_specs(prm.proj_tile(hs_by_rate[g])), smp.rope_specs(smp.proj_tile(h_sample)))))
    gates = _proj("dil_gate", sts, (hs_by_rate[0], h_sample), w_in, jl, DIL_GROUPS * 3 * DIL_W, DIL_W, tn, _ep_plain)
    return qkvs, gates


def _window_prompt_body(k_ref, v_ref, o_ref, *, d):
    mb = k_ref.shape[1]
    for r in range(d):
        for kv, ref in enumerate((k_ref, v_ref)):
            for h in range(DIL_HEADS):
                val = ref[r, :, h * DIL_HD:(h + 1) * DIL_HD].astype(F32)
                o_ref[pl.ds((r * 2 + kv) * DIL_HEADS + h, mb, stride=d * 2 * DIL_HEADS), :] = val


def _window_rows_prompt(qkv, st, d):
    mb = min(DIL_SPAN, 512 // d)
    per_row = 2 * DIL_HEADS
    first = (st.t // d - DIL_SPAN) // mb
    view = qkv.reshape(st.batch, d, st.t // d, 3 * DIL_W)
    out = pl.pallas_call(
        functools.partial(_window_prompt_body, d=d),
        grid=(st.batch, DIL_SPAN // mb),
        in_specs=[pl.BlockSpec((None, d, mb, DIL_W), lambda b, i: (b, 0, first + i, 1)),
                  pl.BlockSpec((None, d, mb, DIL_W), lambda b, i: (b, 0, first + i, 2))],
        out_specs=pl.BlockSpec((None, mb * d * per_row, DIL_HD), lambda b, i: (b, i, 0)),
        out_shape=jax.ShapeDtypeStruct((st.batch, DIL_SPAN * d * per_row, DIL_HD), F32),
        compiler_params=_params(2),
        name="window_prompt",
    )(view, view)
    return out.reshape(st.batch, DIL_SPAN * d, 2, DIL_HEADS, DIL_HD)


def _window_rows_sample(qkv, st):
    kv = qkv.reshape(st.batch, st.t_pad, 3, DIL_HEADS, DIL_HD)[:, :st.t, 1:]
    return kv.astype(F32)


def kernel(x_prompt, x_sample, state_ret, cache_win_g0, cache_win_g1, cache_win_g2, p_prompt, p_sample, norm_g,
           ret_w_in, ret_w_out, dil_w_in, dil_w_out, gm_w_in, gm_ln_g, gm_ln_b, gm_w_s, gm_b_s, gm_w_out, ple_w,
           ple_norm_g, ple_w_gate, final_norm_g):
    depth = norm_g.shape[0]
    bp, sp, _ = x_prompt.shape
    bs_, ts, _ = x_sample.shape
    assert ts <= SAMPLE_PAD and ts % GM_CHUNK != 0 and sp % (DIL_SPAN * max(DIL_RATES)) == 0
    caches = (cache_win_g0, cache_win_g1, cache_win_g2)

    prm = _Stream(bp, sp, sp, 0, 1024, 2048, BF16, True)
    smp = _Stream(bs_, ts, SAMPLE_PAD, PAST_LEN, bs_ * SAMPLE_PAD, bs_ * SAMPLE_PAD, F32, False)
    sts = (prm, smp)

    pad_t = ((0, 0), (0, SAMPLE_PAD - ts), (0, 0))
    xp = x_prompt.reshape(prm.rows, D_MODEL)
    xs = jnp.pad(x_sample, pad_t).reshape(smp.rows, D_MODEL)
    p3s = (p_prompt.astype(BF16).reshape(depth, prm.rows, PLE_DIM),
           jnp.pad(p_sample, ((0, 0),) + pad_t).astype(BF16).reshape(depth, smp.rows, PLE_DIM))

    norm_g3 = norm_g[:, None, :]
    ple_g3 = ple_norm_g[:, None, :]
    ln_g3 = gm_ln_g[:, None, :]
    ln_b3 = gm_ln_b[:, None, :]
    fin_g3 = final_norm_g[None, None, :]

    ret_p, ret_s, gm_s = [], [], []
    win_p = [[], [], []]
    win_s = [[], [], []]
    hins = [_Normed(_rmsnorm(xp, norm_g3, 0, BF16, 512)), _Normed(_rmsnorm(xs, norm_g3, 0, BF16, smp.rows))]
    for i in range(depth):
        kind, jl = i % 3, i // 3
        if kind == 0:
            qkvg = _retention_proj(sts, hins, ret_w_in, jl)
            yp, sp_new = _retention_prompt(qkvg[0], bp, sp)
            ys, ss_new = _retention_sample(qkvg[1], state_ret, jl, bs_, ts)
            ret_p.append(sp_new)
            ret_s.append(ss_new)
            w_out = ret_w_out
        elif kind == 1:
            hp_orders = [_Normed(a) for a in _rmsnorm_orders(xp, norm_g3, i, bp, sp, DIL_RATES[1:], 256)]
            h_sample = _Normed(_rmsnorm(xs, norm_g3, i, BF16, smp.rows))
            qkvs, gates = _dilated_proj(sts, hp_orders, h_sample, dil_w_in, jl)
            yp = _dilated_prompt([a[0] for a in qkvs], gates[0], bp, sp)
            ys = _dilated_sample([a[1] for a in qkvs], gates[1], caches, jl, bs_, ts).astype(BF16)
            for g in range(DIL_GROUPS):
                win_p[g].append(_window_rows_prompt(qkvs[g][0], prm, DIL_RATES[g]))
                win_s[g].append(_cache_shift(caches[g], _window_rows_sample(qkvs[g][1], smp), jl, bs_, ts))
            w_out = dil_w_out
        else:
            tn = 1024
            uvg = _proj("gm_in", sts, hins, gm_w_in, jl, 0, 3 * GM_WIDTH, tn,
                        ((0, _ep_gelu), (2 * GM_WIDTH // tn, _ep_plain)))
            wm_p = jnp.tril(gm_w_s[jl]).astype(BF16)
            bs_p = gm_b_s[jl][:, :, None]
            (yp,) = _gmlp_core(uvg[0], ln_g3, ln_b3, jl, wm_p, bs_p, False)
            wm_t = jnp.pad(jnp.tril(gm_w_s[jl][:, :ts, :ts]), ((0, 0), (0, SAMPLE_PAD - ts), (0, SAMPLE_PAD - ts)))
            wm_s = jnp.einsum("ab,gij->gaibj", jnp.eye(bs_, dtype=F32), wm_t).reshape(GM_GROUPS, smp.rows, smp.rows)
            bs_s = jnp.tile(jnp.pad(gm_b_s[jl][:, :ts], ((0, 0), (0, SAMPLE_PAD - ts))), (1, bs_))[:, :, None]
            ys, vn = _gmlp_core(uvg[1], ln_g3, ln_b3, jl, wm_s.astype(BF16), bs_s, True)
            gm_s.append(vn.reshape(bs_, SAMPLE_PAD, GM_WIDTH)[:, :ts])
            w_out = gm_w_out
        fold_next = i + 1 < depth and (i + 1) % 3 != 1
        (xp, xs), hins = _out_and_ple(sts, (yp, ys.astype(BF16)), w_out, jl, (xp, xs), i, p3s, ple_w, ple_g3,
                                      ple_w_gate, norm_g3, i + 1 if fold_next else None)

    y_prompt = _rmsnorm(xp, fin_g3, 0, F32, 512).reshape(bp, sp, D_MODEL)
    y_sample = _rmsnorm(xs, fin_g3, 0, F32, smp.rows).reshape(bs_, SAMPLE_PAD, D_MODEL)[:, :ts]
    return (y_prompt, y_sample,
            jnp.stack(ret_p), jnp.stack(ret_s),
            jnp.stack(win_p[0]), jnp.stack(win_s[0]),
            jnp.stack(win_p[1]), jnp.stack(win_s[1]),
            jnp.stack(win_p[2]), jnp.stack(win_s[2]),
            jnp.stack(gm_s))
```

```python
import functools
from typing import NamedTuple

import jax
import jax.numpy as jnp
import numpy as np
from jax import lax
from jax.experimental import pallas as pl
from jax.experimental.pallas import tpu as pltpu

F32 = jnp.float32
BF16 = jnp.bfloat16

D_MODEL = 2048
PAST_LEN = 16384
PLE_DIM = 256
ROPE_THETA = 10000.0
EPS = 1e-6
NEG = -1e30

RET_HEADS = 8
RET_DK = 256
RET_DV = 512
RET_QK = RET_HEADS * RET_DK
RET_V = RET_HEADS * RET_DV
RET_CHUNK = 128

DIL_RATES = (1, 4, 16)
DIL_GROUPS = 3
DIL_SPAN = 128
DIL_HEADS = 16
DIL_HD = 128
DIL_W = DIL_HEADS * DIL_HD
DIL_Q_SCALE = DIL_HD ** -0.5 * 1.4426950408889634

GM_WIDTH = 2 * D_MODEL
GM_GROUPS = 16
GM_GD = GM_WIDTH // GM_GROUPS
GM_CHUNK = 128

LANE = 128
MXU_COLS = 256
SAMPLE_PAD = 16
VMEM_LIMIT = 56 * 1024 * 1024


def _params(n_axes, vmem=VMEM_LIMIT):
    return pltpu.CompilerParams(dimension_semantics=("arbitrary",) * n_axes, vmem_limit_bytes=vmem)


def _silu(x):
    return x * jax.nn.sigmoid(x)


def _rms_body(x_ref, g_ref, o_ref):
    x = x_ref[...]
    y = x * lax.rsqrt(jnp.mean(x * x, axis=-1, keepdims=True) + EPS)
    o_ref[...] = (y * g_ref[...]).astype(o_ref.dtype)


def _rmsnorm(x, g3, layer, out_dtype, tr):
    r, d = x.shape
    return pl.pallas_call(
        _rms_body,
        grid=(r // tr,),
        in_specs=[pl.BlockSpec((tr, d), lambda i: (i, 0)),
                  pl.BlockSpec((None, 1, d), lambda i: (layer, 0, 0))],
        out_specs=pl.BlockSpec((tr, d), lambda i: (i, 0)),
        out_shape=jax.ShapeDtypeStruct((r, d), out_dtype),
        compiler_params=_params(1),
        name="rmsnorm",
    )(x, g3)


def _rms_orders_body(x_ref, g_ref, perm_ref, o_ref, *perm_out_refs, rates):
    x = x_ref[...]
    y = (x * lax.rsqrt(jnp.mean(x * x, axis=-1, keepdims=True) + EPS) * g_ref[...]).astype(o_ref.dtype)
    o_ref[...] = y
    tr = x.shape[0]
    for k, (p_ref, d) in enumerate(zip(perm_out_refs, rates)):
        yp = jnp.dot(perm_ref[k], y, preferred_element_type=F32).astype(o_ref.dtype)
        for r in range(d):
            p_ref[r] = yp[r * (tr // d):(r + 1) * (tr // d), :]


def _rmsnorm_orders(x, g3, layer, batch, seq, rates, tr):
    r, dm = x.shape
    nt = seq // tr
    rows = jnp.arange(tr, dtype=jnp.int32)
    perms = jnp.stack([(rows[None, :] == (rows % (tr // d) * d + rows // (tr // d))[:, None]) for d in rates])
    out_specs = [pl.BlockSpec((tr, dm), lambda i: (i, 0))]
    out_shape = [jax.ShapeDtypeStruct((r, dm), BF16)]
    for d in rates:
        out_specs.append(pl.BlockSpec((None, d, tr // d, dm), lambda i: (i // nt, 0, i % nt, 0)))
        out_shape.append(jax.ShapeDtypeStruct((batch, d, seq // d, dm), BF16))
    res = pl.pallas_call(
        functools.partial(_rms_orders_body, rates=rates),
        grid=(r // tr,),
        in_specs=[pl.BlockSpec((tr, dm), lambda i: (i, 0)),
                  pl.BlockSpec((None, 1, dm), lambda i: (layer, 0, 0)),
                  pl.BlockSpec((len(rates), tr, tr), lambda i: (0, 0, 0))],
        out_specs=out_specs,
        out_shape=out_shape,
        compiler_params=_params(1),
        name="rmsnorm_orders",
    )(x, g3, perms.astype(BF16))
    return [res[0]] + [a.reshape(r, dm) for a in res[1:]]


class _Operand(NamedTuple):
    x: jax.Array
    tm: int
    tiled: bool
    epilogue: object
    extras: tuple = ()
    extra_specs: tuple = ()
    outs: tuple = (("tile", F32),)
    ssq: object = None


def _mm_body(w_ref, *refs, layout):
    n_scratch = 1 + sum(has_ssq for _, _, has_ssq, _, _ in layout)
    wb_ref = refs[-n_scratch]
    rinv_refs = list(refs[len(refs) - n_scratch + 1:])
    j = pl.program_id(0)
    i = pl.program_id(1)

    @pl.when(i == 0)
    def _():
        wb_ref[...] = w_ref[...].astype(BF16)

    def stream(x_ref, ssq_ref, rinv_ref, tile, extra, outs, epilogue):
        rinv = None
        if ssq_ref is not None:
            @pl.when(j == 0)
            def _():
                mean_sq = jnp.sum(jnp.sum(ssq_ref[...], axis=0), axis=-1, keepdims=True) / x_ref.shape[1]
                rinv_ref[tile] = lax.rsqrt(mean_sq + EPS)

            rinv = rinv_ref[tile]

        def run(ep):
            n_rows = x_ref.shape[0]
            parts = 2 if ep in ROW_SPLIT_EPILOGUES else 1
            for c in range(wb_ref.shape[1] // MXU_COLS):
                cs = slice(c * MXU_COLS, (c + 1) * MXU_COLS)
                for r in range(parts):
                    rs = slice(r * n_rows // parts, (r + 1) * n_rows // parts)
                    acc = jnp.dot(x_ref[rs, :], wb_ref[:, cs], preferred_element_type=F32)
                    if rinv is not None:
                        acc = acc * rinv[rs]
                    if parts == 1:
                        ep(acc, cs, extra, outs)
                    else:
                        ep(acc, cs, extra, outs, rs)

        if callable(epilogue):
            run(epilogue)
        else:
            starts = [s for s, _ in epilogue] + [None]
            for (lo, ep), hi in zip(epilogue, starts[1:]):
                pred = j >= lo if hi is None else jnp.logical_and(j >= lo, j < hi)
                pl.when(pred)(functools.partial(run, ep))

    n_in = sum(n for n, _, _, _, _ in layout)
    pos, opos = 0, n_in
    for n, n_out, has_ssq, tiled, epilogue in layout:
        x_ref = refs[pos]
        ssq_ref = refs[pos + 1] if has_ssq else None
        rinv_ref = rinv_refs.pop(0) if has_ssq else None
        extra = refs[pos + 1 + has_ssq:pos + n]
        outs = refs[opos:opos + n_out]
        pos, opos = pos + n, opos + n_out
        body = functools.partial(stream, x_ref, ssq_ref, rinv_ref, i if tiled else 0, extra, outs, epilogue)
        if tiled:
            body()
        else:
            pl.when(i == 0)(body)


def _matmul(w3, layer, n0, n, tn, operands, name, single_buffer_w=False):
    k = w3.shape[1]
    j0 = n0 // tn
    steps = [op.x.shape[0] // op.tm for op in operands if op.tiled]
    w_mode = dict(pipeline_mode=pl.Buffered(1)) if single_buffer_w else {}
    in_specs = [pl.BlockSpec((None, k, tn), lambda j, i: (layer, 0, j + j0), **w_mode)]
    args, out_specs, out_shape, layout = [w3], [], [], []
    scratch = [pltpu.VMEM((k, tn), BF16)]
    for op in operands:
        rows = op.x.shape[0]
        row = (lambda i: i) if op.tiled else (lambda i: 0)
        in_specs.append(pl.BlockSpec((op.tm, k), lambda j, i, row=row: (row(i), 0)))
        args.append(op.x)
        if op.ssq is not None:
            in_specs.append(pl.BlockSpec((op.ssq.shape[0], op.tm, LANE),
                                         lambda j, i, row=row: (0, jnp.where(j == 0, row(i), 0), 0)))
            args.append(op.ssq)
            scratch.append(pltpu.VMEM((rows // op.tm, op.tm, 1), F32))
        in_specs += list(op.extra_specs)
        args += list(op.extras)
        for out in op.outs:
            if out[0] == "tile":
                out_specs.append(pl.BlockSpec((op.tm, tn), lambda j, i, row=row: (row(i), j)))
                out_shape.append(jax.ShapeDtypeStruct((rows, n), out[1]))
            else:
                out_specs.append(pl.BlockSpec((None, op.tm, LANE), lambda j, i, row=row: (j, row(i), 0)))
                out_shape.append(jax.ShapeDtypeStruct((n // tn, rows, LANE), F32))
        layout.append((1 + (op.ssq is not None) + len(op.extras), len(op.outs), op.ssq is not None, op.tiled,
                       op.epilogue))
    res = pl.pallas_call(
        functools.partial(_mm_body, layout=tuple(layout)),
        grid=(n // tn, steps[0]),
        in_specs=in_specs,
        out_specs=out_specs,
        out_shape=out_shape,
        scratch_shapes=scratch,
        compiler_params=_params(2),
        name=name,
    )(*args)
    groups, pos = [], 0
    for op in operands:
        groups.append(tuple(res[pos:pos + len(op.outs)]))
        pos += len(op.outs)
    return groups


def _ep_plain(acc, cs, extra, outs):
    outs[0][:, cs] = acc.astype(outs[0].dtype)


def _ep_gelu(acc, cs, extra, outs, rs=slice(None)):
    a = -2.0 * 1.4426950408889634 * 0.7978845608028654
    z = acc * (a + (a * 0.044715) * (acc * acc))
    outs[0][rs, cs] = (acc / (1.0 + jnp.exp2(z))).astype(outs[0].dtype)


ROW_SPLIT_EPILOGUES = (_ep_gelu,)


def _emit_stream(x_new, cs, g_ref, outs):
    outs[0][:, cs] = x_new
    if len(outs) == 1:
        return
    outs[1][:, cs] = (x_new * g_ref[:, cs]).astype(outs[1].dtype)
    sq = x_new * x_new
    part = sq[:, :LANE] + sq[:, LANE:]
    if cs.start == 0:
        outs[2][...] = part
    else:
        outs[2][...] += part


def _ep_residual(acc, cs, extra, outs):
    _emit_stream(extra[0][:, cs] + acc, cs, extra[1] if len(extra) > 1 else None, outs)


def _ep_rope_ret(acc, cs, extra, outs, *, scale):
    cos = extra[0][...]
    sin = extra[1][...]
    o = outs[0]
    half = RET_DK // 2
    for h in range(acc.shape[1] // RET_DK):
        x1 = acc[:, h * RET_DK:h * RET_DK + half]
        x2 = acc[:, h * RET_DK + half:(h + 1) * RET_DK]
        c0 = cs.start + h * RET_DK
        y1 = x1 * cos - x2 * sin
        y2 = x2 * cos + x1 * sin
        o[:, c0:c0 + half] = (y1 if scale == 1.0 else y1 * scale).astype(o.dtype)
        o[:, c0 + half:c0 + RET_DK] = (y2 if scale == 1.0 else y2 * scale).astype(o.dtype)


def _ep_rope_dil(acc, cs, extra, outs, *, scale):
    cos2 = extra[0][...]
    sin2 = extra[1][...]
    o = outs[0]
    for h in range(acc.shape[1] // DIL_HD):
        x = acc[:, h * DIL_HD:(h + 1) * DIL_HD]
        swapped = pltpu.roll(x, DIL_HD // 2, axis=1)
        c0 = cs.start + h * DIL_HD
        y = x * cos2 + swapped * sin2
        o[:, c0:c0 + DIL_HD] = (y if scale == 1.0 else y * scale).astype(o.dtype)


def _ep_ple(acc, cs, extra, outs):
    pe = jnp.dot(extra[1][...], extra[2][:, cs].astype(BF16), preferred_element_type=F32)
    _emit_stream(extra[0][:, cs] + pe * jax.nn.sigmoid(acc), cs, extra[3] if len(extra) > 3 else None, outs)


def _rope_tables(pos, half):
    inv = ROPE_THETA ** (-np.arange(half, dtype=np.float64) / half)
    ang = pos.astype(np.float64)[:, None] * inv[None, :]
    return np.cos(ang).astype(np.float32), np.sin(ang).astype(np.float32)


def _ret_tables(c, c_pad):
    lg = jnp.log1p(-jnp.exp2(-5.0 - jnp.arange(RET_HEADS, dtype=F32)))
    i = jnp.arange(c_pad, dtype=F32)
    diff = i[:, None] - i[None, :]
    dmat = jnp.where(diff[None] >= 0, jnp.exp(lg[:, None, None] * jnp.maximum(diff, 0.0)[None]), 0.0)
    xi = jnp.exp(lg[:, None] * (i[None, :] + 1.0))[:, :, None]
    zeta = jnp.exp(lg[:, None] * (c - 1.0 - i[None, :]))[:, :, None]
    gc = jnp.broadcast_to(jnp.exp(lg * c)[:, None, None], (RET_HEADS, 1, RET_DV))
    return dmat, xi, zeta, gc


def _ret_finish(o, g):
    on = o * lax.rsqrt(jnp.mean(o * o, axis=-1, keepdims=True) + EPS)
    return on * _silu(g)


_NT = (((1,), (1,)), ((), ()))
_TN = (((0,), (0,)), ((), ()))


def _ret_prompt_body(q_ref, k_ref, v_ref, g_ref, dmat_ref, xi_ref, zeta_ref, gc_ref, o_ref, s_ref,
                     sc_s, kv_s, sb_s, *, c, cb):
    @pl.when(pl.program_id(2) == 0)
    def _():
        s_ref[...] = jnp.zeros_like(s_ref)

    dmat = dmat_ref[...]
    xi = xi_ref[...]
    zeta = zeta_ref[...]
    gc = gc_ref[...]
    for i in range(cb):
        rows = slice(i * c, (i + 1) * c)
        k = k_ref[rows, :]
        sc = lax.dot_general(q_ref[rows, :], k, _NT, preferred_element_type=F32) * dmat
        sc_s[i] = sc.astype(BF16)
        kz = (k.astype(F32) * zeta).astype(BF16)
        kv_s[i] = lax.dot_general(kz, v_ref[rows, :], _TN, preferred_element_type=F32)
    for i in range(cb):
        s = s_ref[...]
        sb_s[i] = s.astype(BF16)
        s_ref[...] = s * gc + kv_s[i]
    for i in range(cb):
        rows = slice(i * c, (i + 1) * c)
        o = jnp.dot(sc_s[i], v_ref[rows, :], preferred_element_type=F32)
        o = o + jnp.dot(q_ref[rows, :], sb_s[i], preferred_element_type=F32) * xi
        o_ref[rows, :] = _ret_finish(o, g_ref[rows, :].astype(F32)).astype(o_ref.dtype)


def _retention_prompt(qkvg, batch, seq):
    c = RET_CHUNK
    cb = 16
    rb = c * cb
    ncb = seq // rb
    dmat, xi, zeta, gc = _ret_tables(c, c)
    row = lambda b, h, i: (b * ncb + i, h)
    tab = lambda b, h, i: (h, 0, 0)
    k0, v0 = RET_QK // RET_DK, 2 * RET_QK // RET_DV
    return pl.pallas_call(
        functools.partial(_ret_prompt_body, c=c, cb=cb),
        grid=(batch, RET_HEADS, ncb),
        in_specs=[pl.BlockSpec((rb, RET_DK), row),
                  pl.BlockSpec((rb, RET_DK), lambda b, h, i: (b * ncb + i, k0 + h)),
                  pl.BlockSpec((rb, RET_DV), lambda b, h, i: (b * ncb + i, v0 + h)),
                  pl.BlockSpec((rb, RET_DV), lambda b, h, i: (b * ncb + i, v0 + RET_HEADS + h)),
                  pl.BlockSpec((None, c, c), tab),
                  pl.BlockSpec((None, c, 1), tab),
                  pl.BlockSpec((None, c, 1), tab),
                  pl.BlockSpec((None, 1, RET_DV), tab)],
        out_specs=[pl.BlockSpec((rb, RET_DV), row),
                   pl.BlockSpec((None, None, RET_DK, RET_DV), lambda b, h, i: (b, h, 0, 0))],
        out_shape=[jax.ShapeDtypeStruct((batch * seq, RET_V), BF16),
                   jax.ShapeDtypeStruct((batch, RET_HEADS, RET_DK, RET_DV), F32)],
        scratch_shapes=[pltpu.VMEM((cb, c, c), BF16), pltpu.VMEM((cb, RET_DK, RET_DV), F32),
                        pltpu.VMEM((cb, RET_DK, RET_DV), BF16)],
        compiler_params=_params(3),
        name="retention_prompt",
    )(qkvg, qkvg, qkvg, qkvg, dmat, xi, zeta, gc)


def _ret_sample_body(q_ref, k_ref, v_ref, g_ref, s0_ref, dmat_ref, xi_ref, zeta_ref, gc_ref, o_ref, s_ref, *, nb):
    q = q_ref[...].astype(BF16)
    k = k_ref[...]
    v = v_ref[...].astype(BF16)
    xi = xi_ref[...]
    gc = gc_ref[...]
    sc = lax.dot_general(q, k.astype(BF16), _NT, preferred_element_type=F32) * dmat_ref[...]
    o = jnp.dot(sc.astype(BF16), v, preferred_element_type=F32)
    kz = (k * zeta_ref[...]).astype(BF16)
    row_batch = lax.broadcasted_iota(jnp.int32, kz.shape, 0) // SAMPLE_PAD
    cross = []
    for b in range(nb):
        s0 = s0_ref[b]
        qb = q[b * SAMPLE_PAD:(b + 1) * SAMPLE_PAD, :]
        cross.append(jnp.dot(qb, s0.astype(BF16), preferred_element_type=F32))
        kzb = jnp.where(row_batch == b, kz, jnp.zeros_like(kz))
        s_ref[b] = s0 * gc + lax.dot_general(kzb, v, _TN, preferred_element_type=F32)
    o = o + jnp.concatenate(cross, axis=0) * xi
    o_ref[...] = _ret_finish(o, g_ref[...]).astype(o_ref.dtype)


def _retention_sample(qkvg, state, layer, nb, t):
    rows = nb * SAMPLE_PAD
    k0, v0 = RET_QK // RET_DK, 2 * RET_QK // RET_DV
    dmat, xi, zeta, gc = _ret_tables(t, SAMPLE_PAD)
    eye = jnp.eye(nb, dtype=F32)
    dbig = jnp.einsum("ab,hij->haibj", eye, dmat).reshape(RET_HEADS, rows, rows)
    xib = jnp.tile(xi, (1, nb, 1))
    zetab = jnp.tile(zeta, (1, nb, 1))
    col = lambda h: (0, h)
    tab = lambda h: (h, 0, 0)
    return pl.pallas_call(
        functools.partial(_ret_sample_body, nb=nb),
        grid=(RET_HEADS,),
        in_specs=[pl.BlockSpec((rows, RET_DK), col),
                  pl.BlockSpec((rows, RET_DK), lambda h: (0, k0 + h)),
                  pl.BlockSpec((rows, RET_DV), lambda h: (0, v0 + h)),
                  pl.BlockSpec((rows, RET_DV), lambda h: (0, v0 + RET_HEADS + h)),
                  pl.BlockSpec((None, nb, None, RET_DK, RET_DV), lambda h: (layer, 0, h, 0, 0)),
                  pl.BlockSpec((None, rows, rows), tab),
                  pl.BlockSpec((None, rows, 1), tab),
                  pl.BlockSpec((None, rows, 1), tab),
                  pl.BlockSpec((None, 1, RET_DV), tab)],
        out_specs=[pl.BlockSpec((rows, RET_DV), col),
                   pl.BlockSpec((nb, None, RET_DK, RET_DV), lambda h: (0, h, 0, 0))],
        out_shape=[jax.ShapeDtypeStruct((rows, RET_V), BF16),
                   jax.ShapeDtypeStruct((nb, RET_HEADS, RET_DK, RET_DV), F32)],
        compiler_params=_params(1),
        name="retention_sample",
    )(qkvg, qkvg, qkvg, qkvg, state, dbig, xib, zetab, gc)


DIL_PROMPT_SPAN = DIL_SPAN * max(DIL_RATES)
DIL_PROMPT_HEADS = 2


def _dil_prompt_body(*refs, span, hpb):
    ng = DIL_GROUPS
    q_refs, kc_refs, kp_refs = refs[0:ng], refs[ng:2 * ng], refs[2 * ng:3 * ng]
    vc_refs, vp_refs = refs[3 * ng:4 * ng], refs[4 * ng:5 * ng]
    gate_ref, o_ref, acc_s, m_s, l_s, sc_s, sp_s, pc_s, pp_s = refs[5 * ng:]
    has_prev = pl.program_id(1) > 0
    key = lax.broadcasted_iota(jnp.int32, (DIL_SPAN, DIL_SPAN), 0)
    qry = lax.broadcasted_iota(jnp.int32, (DIL_SPAN, DIL_SPAN), 1)
    eye = jnp.where(key == qry, 1.0, 0.0).astype(BF16)
    bias_c = jnp.where(key <= qry, 0.0, NEG).astype(BF16)
    bias_p = jnp.where(key >= qry, 0.0, NEG).astype(BF16)
    bias_p0 = jnp.where(jnp.logical_and(key >= qry, has_prev), 0.0, NEG).astype(BF16)

    def scores(q, k, bias):
        return lax.dot_general(jnp.concatenate([q, eye], axis=1), jnp.concatenate([k, bias], axis=1), _NT,
                               preferred_element_type=F32)

    def prev_refs(g, r, n, cur_ref, prv_ref, hs):
        if n > 0:
            return cur_ref[r, (n - 1) * DIL_SPAN:n * DIL_SPAN, hs]
        return prv_ref[r, :, hs]

    for hl in range(hpb):
        hs = slice(hl * DIL_HD, (hl + 1) * DIL_HD)
        for g in range(ng):
            d = DIL_RATES[g]
            blocks = [(r, n) for r in range(d) for n in range(span // (DIL_SPAN * d))]
            for i, (r, n) in enumerate(blocks):
                rows = slice(n * DIL_SPAN, (n + 1) * DIL_SPAN)
                q = q_refs[g][r, rows, hs]
                kp = prev_refs(g, r, n, kc_refs[g], kp_refs[g], hs)
                sc_s[i] = scores(q, kc_refs[g][r, rows, hs], bias_c)
                sp_s[i] = scores(q, kp, bias_p if n > 0 else bias_p0)
            for i, (r, n) in enumerate(blocks):
                s_c = sc_s[i]
                s_p = sp_s[i]
                m = jnp.max(jnp.maximum(s_c, s_p), axis=-1, keepdims=True)
                p_c = jnp.exp2(s_c - m)
                p_p = jnp.exp2(s_p - m)
                den = jnp.sum(p_c + p_p, axis=-1, keepdims=True)
                pc_s[i] = p_c.astype(BF16)
                pp_s[i] = p_p.astype(BF16)
                dst = slice(n * DIL_SPAN, (n + 1) * DIL_SPAN) if d == 1 else pl.ds(n * DIL_SPAN * d + r, DIL_SPAN, stride=d)
                m_s[g, dst, :] = jnp.broadcast_to(m, (DIL_SPAN, LANE))
                l_s[g, dst, :] = jnp.broadcast_to(den, (DIL_SPAN, LANE))
            for i, (r, n) in enumerate(blocks):
                rows = slice(n * DIL_SPAN, (n + 1) * DIL_SPAN)
                vp = prev_refs(g, r, n, vc_refs[g], vp_refs[g], hs)
                acc = jnp.dot(pc_s[i], vc_refs[g][r, rows, hs], preferred_element_type=F32)
                acc = acc + jnp.dot(pp_s[i], vp, preferred_element_type=F32)
                dst = rows if d == 1 else pl.ds(n * DIL_SPAN * d + r, DIL_SPAN, stride=d)
                acc_s[g, dst, :] = acc
        mr = 64
        for c in range(span // mr):
            rs = slice(c * mr, (c + 1) * mr)
            ms = [m_s[g, rs, :] for g in range(ng)]
            m_all = jnp.maximum(jnp.maximum(ms[0], ms[1]), ms[2])
            es = [jnp.exp2(m - m_all) for m in ms]
            tot = es[0] * l_s[0, rs, :] + es[1] * l_s[1, rs, :] + es[2] * l_s[2, rs, :]
            num = es[0] * acc_s[0, rs, :] + es[1] * acc_s[1, rs, :] + es[2] * acc_s[2, rs, :]
            o_ref[rs, hs] = (num / tot * _silu(gate_ref[rs, hs].astype(F32))).astype(o_ref.dtype)


def _dilated_prompt(qkvs, gate, batch, seq):
    span, hpb = DIL_PROMPT_SPAN, DIL_PROMPT_HEADS
    wc = hpb * DIL_HD
    kcol, vcol = DIL_W // wc, 2 * DIL_W // wc
    nsp = seq // span
    views = [a.reshape(batch, d, seq // d, 3 * DIL_W) for a, d in zip(qkvs, DIL_RATES)]
    q_specs, kc_specs, kp_specs, vc_specs, vp_specs = [], [], [], [], []
    for d in DIL_RATES:
        cur = (None, d, span // d, wc)
        prv = (None, d, DIL_SPAN, wc)
        nblk = span // (d * DIL_SPAN)
        q_specs.append(pl.BlockSpec(cur, lambda b, s, h: (b, 0, s, h)))
        kc_specs.append(pl.BlockSpec(cur, lambda b, s, h: (b, 0, s, kcol + h)))
        vc_specs.append(pl.BlockSpec(cur, lambda b, s, h: (b, 0, s, vcol + h)))
        kp_specs.append(pl.BlockSpec(prv, lambda b, s, h, nblk=nblk: (b, 0, jnp.maximum(s * nblk - 1, 0), kcol + h)))
        vp_specs.append(pl.BlockSpec(prv, lambda b, s, h, nblk=nblk: (b, 0, jnp.maximum(s * nblk - 1, 0), vcol + h)))
    rows_spec = pl.BlockSpec((span, wc), lambda b, s, h: (b * nsp + s, h))
    return pl.pallas_call(
        functools.partial(_dil_prompt_body, span=span, hpb=hpb),
        grid=(batch, nsp, DIL_W // wc),
        in_specs=q_specs + kc_specs + kp_specs + vc_specs + vp_specs + [rows_spec],
        out_specs=rows_spec,
        out_shape=jax.ShapeDtypeStruct((batch * seq, DIL_W), BF16),
        scratch_shapes=([pltpu.VMEM((DIL_GROUPS, span, LANE), F32)] * 3
                        + [pltpu.VMEM((span // DIL_SPAN, DIL_SPAN, DIL_SPAN), F32)] * 2
                        + [pltpu.VMEM((span // DIL_SPAN, DIL_SPAN, DIL_SPAN), BF16)] * 2),
        compiler_params=_params(3),
        name="dilated_prompt",
    )(*(views * 5), gate)


def _dil_sample_body(qkv0_ref, qkv1_ref, qkv2_ref, gate_ref, c0_ref, c1_ref, c2_ref, o_ref, *, n_new):
    t = pl.program_id(1)

    @pl.when(t == 0)
    def _():
        o_ref[...] = jnp.zeros_like(o_ref)

    key_row = lax.broadcasted_iota(jnp.int32, (DIL_SPAN, 1, 1), 0)
    ms, dens, accs = [], [], []
    for g, (qkv_ref, c_ref) in enumerate(zip((qkv0_ref, qkv1_ref, qkv2_ref), (c0_ref, c1_ref, c2_ref))):
        k_heads = slice(DIL_HEADS, 2 * DIL_HEADS)
        v_heads = slice(2 * DIL_HEADS, 3 * DIL_HEADS)
        q = qkv_ref[t, 0:DIL_HEADS, :]
        s = jnp.sum(c_ref[:, 0] * q[None], axis=-1, keepdims=True)
        if DIL_RATES[g] == 1:
            s = jnp.where(key_row >= t, s, NEG)
            new_rows = [(tn, tn <= t) for tn in range(n_new)]
        else:
            new_rows = [(t, None)]
        s_new = []
        for tn, valid in new_rows:
            sn = jnp.sum(qkv_ref[tn, k_heads, :] * q, axis=-1, keepdims=True)
            s_new.append(sn if valid is None else jnp.where(valid, sn, NEG))
        m = jnp.max(s, axis=0)
        for sn in s_new:
            m = jnp.maximum(m, sn)
        p = jnp.exp2(s - m[None])
        den = jnp.sum(p, axis=0)
        acc = jnp.sum(p * c_ref[:, 1], axis=0)
        for (tn, _), sn in zip(new_rows, s_new):
            pn = jnp.exp2(sn - m)
            den = den + pn
            acc = acc + pn * qkv_ref[tn, v_heads, :]
        ms.append(m)
        dens.append(den)
        accs.append(acc)
    m_all = jnp.maximum(jnp.maximum(ms[0], ms[1]), ms[2])
    es = [jnp.exp2(m - m_all) for m in ms]
    tot = es[0] * dens[0] + es[1] * dens[1] + es[2] * dens[2]
    merged = (es[0] * accs[0] + es[1] * accs[1] + es[2] * accs[2]) / tot
    o_ref[t] = merged * _silu(gate_ref[t])


def _dilated_sample(qkvs, gate, caches, layer, nb, t):
    rows = nb * SAMPLE_PAD
    views, specs = [], []
    for g, d in enumerate(DIL_RATES):
        c = caches[g]
        assert c.shape[2] == DIL_SPAN * d and t <= d * (1 if d > 1 else DIL_SPAN)
        views.append(c.reshape(c.shape[0], nb, DIL_SPAN, d, 2, DIL_HEADS, DIL_HD))
        blk = (None, None, DIL_SPAN, None, 2, DIL_HEADS, DIL_HD)
        if d == 1:
            specs.append(pl.BlockSpec(blk, lambda b, i: (layer, b, 0, 0, 0, 0, 0)))
        else:
            specs.append(pl.BlockSpec(blk, lambda b, i: (layer, b, 0, i, 0, 0, 0)))
    rows3 = pl.BlockSpec((SAMPLE_PAD, 3 * DIL_HEADS, DIL_HD), lambda b, i: (b, 0, 0))
    rows1 = pl.BlockSpec((SAMPLE_PAD, DIL_HEADS, DIL_HD), lambda b, i: (b, 0, 0))
    out = pl.pallas_call(
        functools.partial(_dil_sample_body, n_new=t),
        grid=(nb, t),
        in_specs=[rows3, rows3, rows3, rows1] + specs,
        out_specs=rows1,
        out_shape=jax.ShapeDtypeStruct((rows, DIL_HEADS, DIL_HD), F32),
        compiler_params=_params(2),
        name="dilated_sample",
    )(*(a.reshape(rows, 3 * DIL_HEADS, DIL_HD) for a in qkvs), gate.reshape(rows, DIL_HEADS, DIL_HD), *views)
    return out.reshape(rows, DIL_W)


def _cache_shift_body(a_ref, nxt_ref, new_ref, o_ref, *, t, tb):
    last = pl.program_id(1) == pl.num_programs(1) - 1
    o_ref[0:tb - t] = a_ref[t:tb]

    @pl.when(last)
    def _():
        o_ref[tb - t:tb] = new_ref[...]

    @pl.when(jnp.logical_not(last))
    def _():
        o_ref[tb - t:tb] = nxt_ref[...]


def _cache_shift(cache, new, layer, nb, t):
    wb = cache.shape[2]
    tail = cache.shape[3:]
    tb = min(wb, 512)
    assert wb % tb == 0 and tb % t == 0
    zeros = (0,) * len(tail)
    return pl.pallas_call(
        functools.partial(_cache_shift_body, t=t, tb=tb),
        grid=(nb, wb // tb),
        in_specs=[pl.BlockSpec((None, None, tb) + tail, lambda b, i: (layer, b, i) + zeros),
                  pl.BlockSpec((None, None, t) + tail,
                               lambda b, i: (layer, b, jnp.minimum((i + 1) * (tb // t), wb // t - 1)) + zeros),
                  pl.BlockSpec((None, t) + tail, lambda b, i: (b, 0) + zeros)],
        out_specs=pl.BlockSpec((None, tb) + tail, lambda b, i: (b, i) + zeros),
        out_shape=jax.ShapeDtypeStruct(cache.shape[1:], cache.dtype),
        compiler_params=_params(2),
        name="cache_shift",
    )(cache, cache, new)


def _gmlp_body(u_ref, v_ref, gate_ref, lng_ref, lnb_ref, wm_ref, bs_ref, o_ref, *vn_ref):
    c = wm_ref.shape[1]
    for k in range(u_ref.shape[0] // c):
        rows = slice(k * c, (k + 1) * c)
        v = v_ref[rows, :].astype(F32)
        mu = jnp.mean(v, axis=-1, keepdims=True)
        xc = v - mu
        vn = xc * lax.rsqrt(jnp.mean(xc * xc, axis=-1, keepdims=True) + EPS) * lng_ref[...] + lnb_ref[...]
        if vn_ref:
            vn_ref[0][rows, :] = vn
        vb = vn.astype(BF16)
        for g in range(GM_GROUPS):
            gs = slice(g * GM_GD, (g + 1) * GM_GD)
            mixed = jnp.dot(wm_ref[g], vb[:, gs], preferred_element_type=F32) + bs_ref[g]
            o_ref[rows, gs] = (u_ref[rows, gs].astype(F32) * mixed
                               * _silu(gate_ref[rows, gs].astype(F32))).astype(o_ref.dtype)


def _gmlp_core(uvg, ln_g3, ln_b3, layer, wm, bs, want_vn):
    rows = uvg.shape[0]
    c = wm.shape[1]
    rb = min(rows, 2 * c)
    row = lambda i: (i, 0)
    out_specs = [pl.BlockSpec((rb, GM_WIDTH), row)]
    out_shape = [jax.ShapeDtypeStruct((rows, GM_WIDTH), BF16)]
    if want_vn:
        out_specs.append(pl.BlockSpec((rb, GM_WIDTH), row))
        out_shape.append(jax.ShapeDtypeStruct((rows, GM_WIDTH), F32))
    res = pl.pallas_call(
        _gmlp_body,
        grid=(rows // rb,),
        in_specs=[pl.BlockSpec((rb, GM_WIDTH), row),
                  pl.BlockSpec((rb, GM_WIDTH), lambda i: (i, 1)),
                  pl.BlockSpec((rb, GM_WIDTH), lambda i: (i, 2)),
                  pl.BlockSpec((None, 1, GM_WIDTH), lambda i: (layer, 0, 0)),
                  pl.BlockSpec((None, 1, GM_WIDTH), lambda i: (layer, 0, 0)),
                  pl.BlockSpec((GM_GROUPS, c, c), lambda i: (0, 0, 0)),
                  pl.BlockSpec((GM_GROUPS, c, 1), lambda i: (0, 0, 0))],
        out_specs=out_specs,
        out_shape=out_shape,
        compiler_params=_params(1),
        name="gmlp_core",
    )(uvg, uvg, uvg, ln_g3, ln_b3, wm, bs)
    return res


class _Stream:
    def __init__(self, batch, t, t_pad, pos0, tm, tm_proj, act_dtype, tiled):
        self.batch, self.t, self.t_pad, self.act, self.tiled = batch, t, t_pad, act_dtype, tiled
        self.tm, self.tm_proj = tm, tm_proj
        self.rows = batch * t_pad
        assert tiled or tm == tm_proj == self.rows
        self.row = (lambda i: i) if tiled else (lambda i: 0)
        pos = pos0 + np.arange(t_pad)
        cos, sin = _rope_tables(pos, RET_DK // 2)
        self.ret_rope = (cos, sin)
        cos, sin = _rope_tables(pos, DIL_HD // 2)
        self.dil_rope = (np.concatenate([cos, cos], axis=-1), np.concatenate([-sin, sin], axis=-1))
        if t_pad < tm_proj:
            rep = tm_proj // t_pad
            self.ret_rope = tuple(np.tile(a, (rep, 1)) for a in self.ret_rope)
            self.dil_rope = tuple(np.tile(a, (rep, 1)) for a in self.dil_rope)

    def proj_tile(self, h):
        return self.tm_proj if h.ssq is None else self.tm

    def rope_specs(self, tm):
        nt, row = max(self.t_pad // tm, 1), self.row
        return (pl.BlockSpec((tm, LANE), lambda j, i: (row(i) % nt, 0)),) * 2

    def tile_spec(self, tm, tn):
        row = self.row
        return pl.BlockSpec((tm, tn), lambda j, i: (row(i), j))

    def dil_rope_residue_major(self, d):
        t = self.t_pad
        return tuple(a.reshape(t // d, d, LANE).transpose(1, 0, 2).reshape(t, LANE) for a in self.dil_rope)


class _Normed(NamedTuple):
    x: jax.Array
    ssq: object = None


def _proj(name, sts, hins, w3, layer, n0, n, tn, epilogue, extras=None, extra_specs=None):
    ops = []
    for k, (st, h) in enumerate(zip(sts, hins)):
        ops.append(_Operand(h.x, st.proj_tile(h), st.tiled, epilogue, extras[k] if extras else (),
                            extra_specs[k] if extra_specs else (), (("tile", st.act),), h.ssq))
    return [r[0] for r in _matmul(w3, layer, n0, n, tn, ops, name)]


def _out_and_ple(sts, ys, w_out3, jl, xs, i, p3s, ple_w, ple_g3, ple_w_gate, next_g3, next_layer):
    gain = lambda l, tn: pl.BlockSpec((None, 1, tn), lambda j, m: (l, 0, j))
    stream_outs = (("tile", F32), ("tile", BF16), ("ssq",))
    tn = 1024
    half = lambda st: st.tm // 2 if st.tiled else st.tm
    ops = [_Operand(y, half(st), st.tiled, _ep_residual, (x, ple_g3), (st.tile_spec(half(st), tn), gain(i, tn)),
                    stream_outs) for st, y, x in zip(sts, ys, xs)]
    mids = _matmul(w_out3, jl, 0, D_MODEL, tn, ops, "out_proj", single_buffer_w=w_out3.shape[1] > D_MODEL)
    ops = []
    for st, (x1, xg1, ssq1), p3 in zip(sts, mids, p3s):
        row = st.row
        tm = half(st)
        extras = [x1, p3, ple_w]
        specs = [st.tile_spec(tm, tn),
                 pl.BlockSpec((None, tm, PLE_DIM), lambda j, m, row=row: (i, row(m), 0)),
                 pl.BlockSpec((None, PLE_DIM, tn), lambda j, m: (i, 0, j))]
        if next_layer is not None:
            extras.append(next_g3)
            specs.append(gain(next_layer, tn))
        ops.append(_Operand(xg1, tm, st.tiled, _ep_ple, tuple(extras), tuple(specs),
                            stream_outs if next_layer is not None else (("tile", F32),), ssq1))
    res = _matmul(ple_w_gate, i, 0, D_MODEL, tn, ops, "ple")
    x_new = [r[0] for r in res]
    h_next = [_Normed(r[1], r[2]) for r in res] if next_layer is not None else None
    return x_new, h_next


def _retention_proj(sts, hins, w_in, jl):
    tn = 1024
    tiles = RET_QK // tn
    epilogues = ((0, functools.partial(_ep_rope_ret, scale=1.0)),
                 (tiles, functools.partial(_ep_rope_ret, scale=RET_DK ** -0.5)),
                 (2 * tiles, _ep_plain))
    ropes = [st.ret_rope for st in sts]
    rope_specs = [st.rope_specs(st.proj_tile(h)) for st, h in zip(sts, hins)]
    return _proj("ret_in", sts, hins, w_in, jl, 0, 2 * RET_QK + 2 * RET_V, tn, epilogues, ropes, rope_specs)


def _dilated_proj(sts, hs_by_rate, h_sample, w_in, jl):
    prm, smp = sts
    tn = 1024
    tiles = DIL_W // tn
    epilogues = ((0, functools.partial(_ep_rope_dil, scale=DIL_Q_SCALE)),
                 (tiles, functools.partial(_ep_rope_dil, scale=1.0)),
                 (2 * tiles, _ep_plain))
    qkvs = []
    for g, d in enumerate(DIL_RATES):
        rope = prm.dil_rope if d == 1 else prm.dil_rope_residue_major(d)
        qkvs.append(_proj(f"dil_qkv_g{g}", sts, (hs_by_rate[g], h_sample), w_in, jl, g * 3 * DIL_W, 3 * DIL_W, tn,
                          epilogues, (rope, smp.dil_rope),
                          (prm.rope_specs(prm.proj_tile(hs_by_rate[g])), smp.rope_specs(smp.proj_tile(h_sample)))))
    gates = _proj("dil_gate", sts, (hs_by_rate[0], h_sample), w_in, jl, DIL_GROUPS * 3 * DIL_W, DIL_W, tn, _ep_plain)
    return qkvs, gates


def _window_prompt_body(k_ref, v_ref, o_ref, *, d):
    mb = k_ref.shape[1]
    for r in range(d):
        for kv, ref in enumerate((k_ref, v_ref)):
            for h in range(DIL_HEADS):
                val = ref[r, :, h * DIL_HD:(h + 1) * DIL_HD].astype(F32)
                o_ref[pl.ds((r * 2 + kv) * DIL_HEADS + h, mb, stride=d * 2 * DIL_HEADS), :] = val


def _window_rows_prompt(qkv, st, d):
    mb = min(DIL_SPAN, 512 // d)
    per_row = 2 * DIL_HEADS
    first = (st.t // d - DIL_SPAN) // mb
    view = qkv.reshape(st.batch, d, st.t // d, 3 * DIL_W)
    out = pl.pallas_call(
        functools.partial(_window_prompt_body, d=d),
        grid=(st.batch, DIL_SPAN // mb),
        in_specs=[pl.BlockSpec((None, d, mb, DIL_W), lambda b, i: (b, 0, first + i, 1)),
                  pl.BlockSpec((None, d, mb, DIL_W), lambda b, i: (b, 0, first + i, 2))],
        out_specs=pl.BlockSpec((None, mb * d * per_row, DIL_HD), lambda b, i: (b, i, 0)),
        out_shape=jax.ShapeDtypeStruct((st.batch, DIL_SPAN * d * per_row, DIL_HD), F32),
        compiler_params=_params(2),
        name="window_prompt",
    )(view, view)
    return out.reshape(st.batch, DIL_SPAN * d, 2, DIL_HEADS, DIL_HD)


def _window_rows_sample(qkv, st):
    kv = qkv.reshape(st.batch, st.t_pad, 3, DIL_HEADS, DIL_HD)[:, :st.t, 1:]
    return kv.astype(F32)


def kernel(x_prompt, x_sample, state_ret, cache_win_g0, cache_win_g1, cache_win_g2, p_prompt, p_sample, norm_g,
           ret_w_in, ret_w_out, dil_w_in, dil_w_out, gm_w_in, gm_ln_g, gm_ln_b, gm_w_s, gm_b_s, gm_w_out, ple_w,
           ple_norm_g, ple_w_gate, final_norm_g):
    depth = norm_g.shape[0]
    bp, sp, _ = x_prompt.shape
    bs_, ts, _ = x_sample.shape
    assert ts <= SAMPLE_PAD and ts % GM_CHUNK != 0 and sp % (DIL_SPAN * max(DIL_RATES)) == 0
    caches = (cache_win_g0, cache_win_g1, cache_win_g2)

    prm = _Stream(bp, sp, sp, 0, 1024, 2048, BF16, True)
    smp = _Stream(bs_, ts, SAMPLE_PAD, PAST_LEN, bs_ * SAMPLE_PAD, bs_ * SAMPLE_PAD, F32, False)
    sts = (prm, smp)

    pad_t = ((0, 0), (0, SAMPLE_PAD - ts), (0, 0))
    xp = x_prompt.reshape(prm.rows, D_MODEL)
    xs = jnp.pad(x_sample, pad_t).reshape(smp.rows, D_MODEL)
    p3s = (p_prompt.astype(BF16).reshape(depth, prm.rows, PLE_DIM),
           jnp.pad(p_sample, ((0, 0),) + pad_t).astype(BF16).reshape(depth, smp.rows, PLE_DIM))

    norm_g3 = norm_g[:, None, :]
    ple_g3 = ple_norm_g[:, None, :]
    ln_g3 = gm_ln_g[:, None, :]
    ln_b3 = gm_ln_b[:, None, :]
    fin_g3 = final_norm_g[None, None, :]

    ret_p, ret_s, gm_s = [], [], []
    win_p = [[], [], []]
    win_s = [[], [], []]
    hins = [_Normed(_rmsnorm(xp, norm_g3, 0, BF16, 512)), _Normed(_rmsnorm(xs, norm_g3, 0, BF16, smp.rows))]
    for i in range(depth):
        kind, jl = i % 3, i // 3
        if kind == 0:
            qkvg = _retention_proj(sts, hins, ret_w_in, jl)
            yp, sp_new = _retention_prompt(qkvg[0], bp, sp)
            ys, ss_new = _retention_sample(qkvg[1], state_ret, jl, bs_, ts)
            ret_p.append(sp_new)
            ret_s.append(ss_new)
            w_out = ret_w_out
        elif kind == 1:
            hp_orders = [_Normed(a) for a in _rmsnorm_orders(xp, norm_g3, i, bp, sp, DIL_RATES[1:], 256)]
            h_sample = _Normed(_rmsnorm(xs, norm_g3, i, BF16, smp.rows))
            qkvs, gates = _dilated_proj(sts, hp_orders, h_sample, dil_w_in, jl)
            yp = _dilated_prompt([a[0] for a in qkvs], gates[0], bp, sp)
            ys = _dilated_sample([a[1] for a in qkvs], gates[1], caches, jl, bs_, ts).astype(BF16)
            for g in range(DIL_GROUPS):
                win_p[g].append(_window_rows_prompt(qkvs[g][0], prm, DIL_RATES[g]))
                win_s[g].append(_cache_shift(caches[g], _window_rows_sample(qkvs[g][1], smp), jl, bs_, ts))
            w_out = dil_w_out
        else:
            tn = 1024
            uvg = _proj("gm_in", sts, hins, gm_w_in, jl, 0, 3 * GM_WIDTH, tn,
                        ((0, _ep_gelu), (2 * GM_WIDTH // tn, _ep_plain)))
            wm_p = jnp.tril(gm_w_s[jl]).astype(BF16)
            bs_p = gm_b_s[jl][:, :, None]
            (yp,) = _gmlp_core(uvg[0], ln_g3, ln_b3, jl, wm_p, bs_p, False)
            wm_t = jnp.pad(jnp.tril(gm_w_s[jl][:, :ts, :ts]), ((0, 0), (0, SAMPLE_PAD - ts), (0, SAMPLE_PAD - ts)))
            wm_s = jnp.einsum("ab,gij->gaibj", jnp.eye(bs_, dtype=F32), wm_t).reshape(GM_GROUPS, smp.rows, smp.rows)
            bs_s = jnp.tile(jnp.pad(gm_b_s[jl][:, :ts], ((0, 0), (0, SAMPLE_PAD - ts))), (1, bs_))[:, :, None]
            ys, vn = _gmlp_core(uvg[1], ln_g3, ln_b3, jl, wm_s.astype(BF16), bs_s, True)
            gm_s.append(vn.reshape(bs_, SAMPLE_PAD, GM_WIDTH)[:, :ts])
            w_out = gm_w_out
        fold_next = i + 1 < depth and (i + 1) % 3 != 1
        (xp, xs), hins = _out_and_ple(sts, (yp, ys.astype(BF16)), w_out, jl, (xp, xs), i, p3s, ple_w, ple_g3,
                                      ple_w_gate, norm_g3, i + 1 if fold_next else None)

    y_prompt = _rmsnorm(xp, fin_g3, 0, F32, 512).reshape(bp, sp, D_MODEL)
    y_sample = _rmsnorm(xs, fin_g3, 0, F32, smp.rows).reshape(bs_, SAMPLE_PAD, D_MODEL)[:, :ts]
    return (y_prompt, y_sample,
            jnp.stack(ret_p), jnp.stack(ret_s),
            jnp.stack(win_p[0]), jnp.stack(win_s[0]),
            jnp.stack(win_p[1]), jnp.stack(win_s[1]),
            jnp.stack(win_p[2]), jnp.stack(win_s[2]),
            jnp.stack(gm_s))
```
